```python
import jax, jax.numpy as jnp
from jax import lax
import numpy as np

D_MODEL = 2048
BATCH = 4
SEQ = 4096
DEPTH = 2

CONV_WIDTH = D_MODEL // 2
DW_CONV_SIZE = 31
SG_WIDTH = D_MODEL // 2
SG_CHUNK = 128
SG_GROUPS = 8
SG_GROUP_DIM = SG_WIDTH // SG_GROUPS
N_HEADS = D_MODEL // 128
N_KV_GROUPS = 4
HEADS_PER_GROUP = N_HEADS // N_KV_GROUPS
HEAD_DIM = D_MODEL // 32
CMP_BLOCK = 32
CMP_STRIDE = 16
CMP_HIDDEN = 4 * HEAD_DIM
SLC_BLOCK = 64
SLC_TOP_N = 16
WINDOW = 512
NSA_Q_BLOCK = 64
D_FF = 5632
FFN_CONV_SIZE = 3
A_IN = 2 * CONV_WIDTH
B_IN = 2 * SG_WIDTH
Q_IN = N_HEADS * HEAD_DIM
KV_IN = 6 * N_KV_GROUPS * HEAD_DIM
NSA_GATE_IN = 3 * N_HEADS
MERGE_IN = 3 * D_MODEL
N_IN = A_IN + B_IN + Q_IN + KV_IN + NSA_GATE_IN + MERGE_IN

EPS = 1e-6
NEG = -1e30
FORCE = 1e4

kernel_name = "hybrid_conformer_gmlp_nsa_block"


def rms_norm(x, w):
    x32 = x.astype(jnp.float32)
    y = x32 * lax.rsqrt(jnp.mean(x32 * x32, axis=-1, keepdims=True) + EPS)
    return (y * w.astype(jnp.float32)).astype(x.dtype)


def layer_norm(x, g, b):
    x32 = x.astype(jnp.float32)
    xc = x32 - jnp.mean(x32, axis=-1, keepdims=True)
    var = jnp.mean(xc * xc, axis=-1, keepdims=True)
    return (xc * lax.rsqrt(var + EPS) * g.astype(jnp.float32) + b.astype(jnp.float32)).astype(x.dtype)


def causal_depthwise_conv(x, w, b):
    k = w.shape[0]
    y = lax.conv_general_dilated(
        x, w[:, None, :].astype(x.dtype), window_strides=(1,), padding=[(k - 1, 0)],
        dimension_numbers=("NWC", "WIO", "NWC"), feature_group_count=x.shape[-1])
    return y + b


def masked_softmax(s, mask):
    s = jnp.where(mask, s.astype(jnp.float32), NEG)
    m = jnp.max(s, axis=-1, keepdims=True)
    e = jnp.where(mask, jnp.exp(s - m), 0.0)
    return e / jnp.maximum(jnp.sum(e, axis=-1, keepdims=True), 1e-30)


def _split_points():
    sizes = (A_IN, B_IN, Q_IN, KV_IN, NSA_GATE_IN, MERGE_IN)
    return [int(v) for v in np.cumsum(sizes)[:-1]]


def conformer_conv(glu_in, conv_w, conv_b, ln_g, ln_b, w_out):
    a, g = jnp.split(glu_in, 2, axis=-1)
    h = a * jax.nn.sigmoid(g)
    h = causal_depthwise_conv(h, conv_w, conv_b)
    h = jax.nn.silu(layer_norm(h, ln_g, ln_b))
    return h @ w_out


def chunked_spatial_gating(uv, ln_g, ln_b, sg_w, sg_b, w_out):
    bsz, seq, _ = uv.shape
    z = jax.nn.gelu(uv, approximate=False)
    u, v = jnp.split(z, 2, axis=-1)
    v = layer_norm(v, ln_g, ln_b)
    v = v.reshape(bsz, seq // SG_CHUNK, SG_CHUNK, SG_GROUPS, SG_GROUP_DIM)
    causal = jnp.tril(jnp.ones((SG_CHUNK, SG_CHUNK), dtype=bool))
    w = jnp.where(causal, sg_w, 0.0).astype(v.dtype)
    f = jnp.einsum("gts,bcsgd->bctgd", w, v) + sg_b.T[None, None, :, :, None]
    return (u * f.reshape(bsz, seq, SG_WIDTH)) @ w_out


def compress_blocks(k, pe, w1, w2):
    bsz, seq, g, dh = k.shape
    r = CMP_BLOCK // CMP_STRIDE
    n_piece = seq // CMP_STRIDE
    pieces = k.reshape(bsz, n_piece, CMP_STRIDE, g, dh)
    blocks = jnp.concatenate([pieces[:, i:n_piece - r + 1 + i] for i in range(r)], axis=2)
    blocks = blocks + pe[None, None, :, None, :]
    hid = jax.nn.silu(jnp.einsum("bnlgd,ldh->bngh", blocks, w1))
    return jnp.einsum("bngh,hd->bngd", hid, w2)


def _cmp_to_slc_matrix(n_cmp, n_slc):
    r = CMP_BLOCK // CMP_STRIDE
    a = SLC_BLOCK // CMP_STRIDE
    piece = np.arange(n_cmp)[:, None] + np.arange(r)[None, :]
    rows = np.broadcast_to(np.arange(n_cmp)[:, None], piece.shape)
    m = np.zeros((n_cmp, n_slc), np.float32)
    np.add.at(m, (rows, piece // a), 1.0)
    return jnp.asarray(m)


def native_sparse_attention(q, kv, gate_logits, pe_k, w1_k, w2_k, pe_v, w1_v, w2_v, w_out):
    bsz, seq, _ = q.shape
    g, hpg, dh = N_KV_GROUPS, HEADS_PER_GROUP, HEAD_DIM
    q = q.reshape(bsz, seq, g, hpg, dh) * (dh ** -0.5)
    k_cmp, v_cmp, k_slc, v_slc, k_win, v_win = [
        t.reshape(bsz, seq, g, dh) for t in jnp.split(kv, 6, axis=-1)]
    gates = jax.nn.sigmoid(gate_logits).reshape(bsz, seq, g, hpg, 3)

    kc = compress_blocks(k_cmp, pe_k, w1_k, w2_k)
    vc = compress_blocks(v_cmp, pe_v, w1_v, w2_v)
    n_cmp = kc.shape[1]
    cmp_end = jnp.arange(n_cmp) * CMP_STRIDE + CMP_BLOCK - 1
    n_slc = seq // SLC_BLOCK
    top_n = min(SLC_TOP_N, n_slc)
    slc_map = _cmp_to_slc_matrix(n_cmp, n_slc)
    ks_blk = k_slc.reshape(bsz, n_slc, SLC_BLOCK, g, dh).transpose(0, 3, 1, 2, 4)
    vs_blk = v_slc.reshape(bsz, n_slc, SLC_BLOCK, g, dh).transpose(0, 3, 1, 2, 4)
    kw_pad = jnp.pad(k_win, ((0, 0), (WINDOW, 0), (0, 0), (0, 0)))
    vw_pad = jnp.pad(v_win, ((0, 0), (WINDOW, 0), (0, 0), (0, 0)))
    b_idx = jnp.arange(bsz)[:, None, None, None]
    g_idx = jnp.arange(g)[None, :, None, None]
    blk_j = jnp.arange(n_slc)
    offs = jnp.arange(SLC_BLOCK)

    def block_step(start):
        t = start + jnp.arange(NSA_Q_BLOCK)
        qb = lax.dynamic_slice_in_dim(q, start, NSA_Q_BLOCK, axis=1)
        s = jnp.einsum("bqghd,bngd->bghqn", qb, kc)
        p_cmp = masked_softmax(s, cmp_end[None, :] <= t[:, None])
        o_cmp = jnp.einsum("bghqn,bngd->bqghd", p_cmp.astype(vc.dtype), vc)
        imp = jnp.einsum("bghqn,nj->bgqj", p_cmp, slc_map)
        cur = (t // SLC_BLOCK)[:, None]
        valid = blk_j[None, :] <= cur
        forced = (blk_j[None, :] == 0) | (blk_j[None, :] == cur) | (blk_j[None, :] == cur - 1)
        score = jnp.where(valid, jnp.where(forced, FORCE, imp), NEG)
        _, idx = lax.top_k(score, top_n)
        kg = ks_blk[b_idx, g_idx, idx].reshape(bsz, g, NSA_Q_BLOCK, top_n * SLC_BLOCK, dh)
        vg = vs_blk[b_idx, g_idx, idx].reshape(bsz, g, NSA_Q_BLOCK, top_n * SLC_BLOCK, dh)
        pos = (idx[..., None] * SLC_BLOCK + offs).reshape(bsz, g, NSA_Q_BLOCK, top_n * SLC_BLOCK)
        s = jnp.einsum("bqghd,bgqkd->bghqk", qb, kg)
        p = masked_softmax(s, (pos <= t[:, None])[:, :, None])
        o_slc = jnp.einsum("bghqk,bgqkd->bqghd", p.astype(vg.dtype), vg)
        kw = lax.dynamic_slice_in_dim(kw_pad, start, WINDOW + NSA_Q_BLOCK, axis=1)
        vw = lax.dynamic_slice_in_dim(vw_pad, start, WINDOW + NSA_Q_BLOCK, axis=1)
        kpos = start - WINDOW + jnp.arange(WINDOW + NSA_Q_BLOCK)
        wmask = ((kpos[None, :] <= t[:, None]) & (kpos[None, :] > t[:, None] - WINDOW)
                 & (kpos[None, :] >= 0))
        s = jnp.einsum("bqghd,bkgd->bghqk", qb, kw)
        p = masked_softmax(s, wmask)
        o_win = jnp.einsum("bghqk,bkgd->bqghd", p.astype(vw.dtype), vw)
        gb = lax.dynamic_slice_in_dim(gates, start, NSA_Q_BLOCK, axis=1)
        return gb[..., 0:1] * o_cmp + gb[..., 1:2] * o_slc + gb[..., 2:3] * o_win

    starts = jnp.arange(seq // NSA_Q_BLOCK, dtype=jnp.int32) * NSA_Q_BLOCK
    o = lax.map(block_step, starts)
    o = o.transpose(1, 0, 2, 3, 4, 5).reshape(bsz, seq, N_HEADS * HEAD_DIM)
    return o @ w_out


def setup_inputs(seed: int = 0) -> dict:
    key = jax.random.key(seed)
    ks = jax.random.split(key, 32)
    L, D = DEPTH, D_MODEL

    def nrm(k, shape, scale):
        return jax.random.normal(k, shape, jnp.float32) * scale

    def gain(k, n):
        return 1.0 + nrm(k, (L, n), 0.05)

    return {
        "x": nrm(ks[0], (BATCH, SEQ, D), 1.0),
        "norm_mix_pre": gain(ks[1], D),
        "norm_mix_post": gain(ks[2], D),
        "norm_ffn_pre": gain(ks[3], D),
        "norm_ffn_post": gain(ks[4], D),
        "w_in": nrm(ks[5], (L, D, N_IN), D ** -0.5),
        "conv_a_w": nrm(ks[6], (L, DW_CONV_SIZE, CONV_WIDTH), DW_CONV_SIZE ** -0.5),
        "conv_a_b": nrm(ks[7], (L, CONV_WIDTH), 0.02),
        "ln_a_g": gain(ks[8], CONV_WIDTH),
        "ln_a_b": nrm(ks[9], (L, CONV_WIDTH), 0.02),
        "w_a_out": nrm(ks[10], (L, CONV_WIDTH, D), CONV_WIDTH ** -0.5),
        "ln_b_g": gain(ks[11], SG_WIDTH),
        "ln_b_b": nrm(ks[12], (L, SG_WIDTH), 0.02),
        "sg_w": nrm(ks[13], (L, SG_GROUPS, SG_CHUNK, SG_CHUNK), SG_CHUNK ** -0.5),
        "sg_b": 1.0 + nrm(ks[14], (L, SG_GROUPS, SG_CHUNK), 0.05),
        "w_b_out": nrm(ks[15], (L, SG_WIDTH, D), SG_WIDTH ** -0.5),
        "cmp_pe_k": nrm(ks[16], (L, CMP_BLOCK, HEAD_DIM), 0.1),
        "cmp_w1_k": nrm(ks[17], (L, CMP_BLOCK, HEAD_DIM, CMP_HIDDEN), (CMP_BLOCK * HEAD_DIM) ** -0.5),
        "cmp_w2_k": nrm(ks[18], (L, CMP_HIDDEN, HEAD_DIM), CMP_HIDDEN ** -0.5),
        "cmp_pe_v": nrm(ks[19], (L, CMP_BLOCK, HEAD_DIM), 0.1),
        "cmp_w1_v": nrm(ks[20], (L, CMP_BLOCK, HEAD_DIM, CMP_HIDDEN), (CMP_BLOCK * HEAD_DIM) ** -0.5),
        "cmp_w2_v": nrm(ks[21], (L, CMP_HIDDEN, HEAD_DIM), CMP_HIDDEN ** -0.5),
        "w_c_out": nrm(ks[22], (L, N_HEADS * HEAD_DIM, D), (N_HEADS * HEAD_DIM) ** -0.5),
        "w_o": nrm(ks[23], (L, D, D), D ** -0.5),
        "w_up": nrm(ks[24], (L, D, 2 * D_FF), D ** -0.5),
        "ffn_conv_w": nrm(ks[25], (L, FFN_CONV_SIZE, 2 * D_FF), FFN_CONV_SIZE ** -0.5),
        "ffn_conv_b": nrm(ks[26], (L, 2 * D_FF), 0.02),
        "w_down": nrm(ks[27], (L, D_FF, D), D_FF ** -0.5),
    }


def reference(x, norm_mix_pre, norm_mix_post, norm_ffn_pre, norm_ffn_post, w_in,
              conv_a_w, conv_a_b, ln_a_g, ln_a_b, w_a_out,
              ln_b_g, ln_b_b, sg_w, sg_b, w_b_out,
              cmp_pe_k, cmp_w1_k, cmp_w2_k, cmp_pe_v, cmp_w1_v, cmp_w2_v, w_c_out,
              w_o, w_up, ffn_conv_w, ffn_conv_b, w_down):
    splits = _split_points()
    for l in range(DEPTH):
        h = rms_norm(x, norm_mix_pre[l])
        proj = h @ w_in[l]
        a_in, b_in, q, kv, nsa_g, merge_g = jnp.split(proj, splits, axis=-1)
        y_a = conformer_conv(a_in, conv_a_w[l], conv_a_b[l], ln_a_g[l], ln_a_b[l], w_a_out[l])
        y_b = chunked_spatial_gating(b_in, ln_b_g[l], ln_b_b[l], sg_w[l], sg_b[l], w_b_out[l])
        y_c = native_sparse_attention(q, kv, nsa_g, cmp_pe_k[l], cmp_w1_k[l], cmp_w2_k[l],
                                      cmp_pe_v[l], cmp_w1_v[l], cmp_w2_v[l], w_c_out[l])
        g_a, g_b, g_c = jnp.split(jax.nn.sigmoid(merge_g), 3, axis=-1)
        mixed = (g_a * y_a + g_b * y_b + g_c * y_c) @ w_o[l]
        x = x + rms_norm(mixed, norm_mix_post[l])
        h = rms_norm(x, norm_ffn_pre[l])
        u = causal_depthwise_conv(h @ w_up[l], ffn_conv_w[l], ffn_conv_b[l])
        u_g, u_v = jnp.split(u, 2, axis=-1)
        x = x + rms_norm((jax.nn.silu(u_g) * u_v) @ w_down[l], norm_ffn_post[l])
    return x
```

```python
import functools

import numpy as np
import jax
import jax.numpy as jnp
from jax import lax
from jax.experimental import pallas as pl
from jax.experimental.pallas import tpu as pltpu

F32 = jnp.float32
MXU_DTYPE = jnp.bfloat16

EPS = 1e-6
NEG = -1e30
FORCE = 1e4

CONV_WIDTH = 1024
DW_CONV_SIZE = 31
CONV_HALO = 32
SG_WIDTH = 1024
SG_CHUNK = 128
SG_GROUPS = 8
N_HEADS = 16
N_KV_GROUPS = 4
HEADS_PER_GROUP = 4
HEAD_DIM = 64
CMP_BLOCK = 32
CMP_STRIDE = 16
CMP_HIDDEN = 256
SLC_BLOCK = 64
SLC_TOP_N = 16
WINDOW = 512
D_FF = 5632
FFN_HALO = 16
GATE_PAD = 128

VMEM_LIMIT = 56 * 1024 * 1024


def _cparams(sem):
    return pltpu.CompilerParams(dimension_semantics=sem, vmem_limit_bytes=VMEM_LIMIT)


def _sigmoid(x):
    return jax.nn.sigmoid(x)


def _rmsnorm_kernel(x_ref, w_ref, o_ref):
    x = x_ref[...]
    ms = jnp.mean(x * x, axis=-1, keepdims=True)
    o_ref[...] = (x * lax.rsqrt(ms + EPS) * w_ref[...]).astype(o_ref.dtype)


def _rmsnorm_cast(x2d, w):
    m, d = x2d.shape
    tm = min(512, m)
    return pl.pallas_call(
        _rmsnorm_kernel,
        grid=(m // tm,),
        in_specs=[pl.BlockSpec((tm, d), lambda i: (i, 0)),
                  pl.BlockSpec((1, d), lambda i: (0, 0))],
        out_specs=pl.BlockSpec((tm, d), lambda i: (i, 0)),
        out_shape=jax.ShapeDtypeStruct((m, d), MXU_DTYPE),
        compiler_params=_cparams(("parallel",)),
        name="rmsnorm_cast",
    )(x2d, w.reshape(1, d))


def _mm_kernel(a_ref, w_ref, o_ref):
    o_ref[...] = jnp.dot(a_ref[...], w_ref[...], preferred_element_type=F32).astype(o_ref.dtype)


def _pick(n, cands):
    for c in cands:
        if n % c == 0:
            return c
    raise ValueError(f"no tile for {n}")


def _matmul(a, w, out_dtype=F32, name="matmul"):
    m, k = a.shape
    n = w.shape[1]
    tm = min(1024, m)
    tn = _pick(n, (512, 384, 256, 128))
    return pl.pallas_call(
        _mm_kernel,
        grid=(m // tm, n // tn),
        in_specs=[pl.BlockSpec((tm, k), lambda i, j: (i, 0)),
                  pl.BlockSpec((k, tn), lambda i, j: (0, j))],
        out_specs=pl.BlockSpec((tm, tn), lambda i, j: (i, j)),
        out_shape=jax.ShapeDtypeStruct((m, n), out_dtype),
        compiler_params=_cparams(("parallel", "arbitrary")),
        name=name,
    )(a, w)


def _conformer_kernel(cur_ref, halo_ref, cw_ref, cb_ref, lg_ref, lb_ref, wout_ref, o_ref, hbuf, cbuf, *, ts):
    c = CONV_WIDTH
    i = pl.program_id(1)
    cur = cur_ref[...]
    hbuf[pl.ds(CONV_HALO, ts), :] = cur[:, :c] * _sigmoid(cur[:, c:])
    hal = halo_ref[...]
    hh = hal[:, :c] * _sigmoid(hal[:, c:])
    hbuf[pl.ds(0, CONV_HALO), :] = jnp.where(i == 0, 0.0, hh)

    def col_body(cc, carry):
        off = pl.multiple_of(cc * 128, 128)
        acc = jnp.broadcast_to(cb_ref[:, pl.ds(off, 128)], (ts, 128))
        for k in range(DW_CONV_SIZE):
            acc = acc + cw_ref[pl.ds(k, 1), pl.ds(off, 128)] * hbuf[pl.ds(CONV_HALO - (DW_CONV_SIZE - 1) + k, ts),
                                                                   pl.ds(off, 128)]
        cbuf[:, pl.ds(off, 128)] = acc
        return carry

    lax.fori_loop(0, c // 128, col_body, 0)
    y = cbuf[...]
    mean = jnp.mean(y, axis=-1, keepdims=True)
    yc = y - mean
    var = jnp.mean(yc * yc, axis=-1, keepdims=True)
    z = yc * lax.rsqrt(var + EPS) * lg_ref[...] + lb_ref[...]
    z = z * _sigmoid(z)
    o_ref[...] = jnp.dot(z.astype(MXU_DTYPE), wout_ref[...], preferred_element_type=F32)


def _conformer(a_in, conv_w, conv_b, ln_g, ln_b, w_out):
    bsz, seq, _ = a_in.shape
    c = CONV_WIDTH
    d = w_out.shape[1]
    ts = min(256, seq)
    per = ts // CONV_HALO
    kern = functools.partial(_conformer_kernel, ts=ts)
    return pl.pallas_call(
        kern,
        grid=(bsz, seq // ts),
        in_specs=[pl.BlockSpec((None, ts, 2 * c), lambda b, i: (b, i, 0)),
                  pl.BlockSpec((None, CONV_HALO, 2 * c), lambda b, i: (b, jnp.maximum(i * per - 1, 0), 0)),
                  pl.BlockSpec((DW_CONV_SIZE, c), lambda b, i: (0, 0)),
                  pl.BlockSpec((1, c), lambda b, i: (0, 0)),
                  pl.BlockSpec((1, c), lambda b, i: (0, 0)),
                  pl.BlockSpec((1, c), lambda b, i: (0, 0)),
                  pl.BlockSpec((c, d), lambda b, i: (0, 0))],
        out_specs=pl.BlockSpec((None, ts, d), lambda b, i: (b, i, 0)),
        out_shape=jax.ShapeDtypeStruct((bsz, seq, d), F32),
        scratch_shapes=[pltpu.VMEM((CONV_HALO + ts, c), F32), pltpu.VMEM((ts, c), F32)],
        compiler_params=_cparams(("parallel", "parallel")),
        name="conformer_conv",
    )(a_in, a_in, conv_w, conv_b.reshape(1, c), ln_g.reshape(1, c), ln_b.reshape(1, c), w_out.astype(MXU_DTYPE))


def _gmlp_kernel(x_ref, lg_ref, lb_ref, sgw_ref, sgbt_ref, wout_ref, o_ref, fbuf, *, ts):
    c = SG_WIDTH
    gd = c // SG_GROUPS
    x = x_ref[...]
    z = 0.5 * x * (1.0 + lax.erf(x * np.float32(np.sqrt(0.5))))
    u = z[:, :c]
    v = z[:, c:]
    mean = jnp.mean(v, axis=-1, keepdims=True)
    vc = v - mean
    var = jnp.mean(vc * vc, axis=-1, keepdims=True)
    v = vc * lax.rsqrt(var + EPS) * lg_ref[...] + lb_ref[...]
    row = lax.broadcasted_iota(jnp.int32, (SG_CHUNK, SG_CHUNK), 0)
    col = lax.broadcasted_iota(jnp.int32, (SG_CHUNK, SG_CHUNK), 1)
    causal = col <= row
    for g in range(SG_GROUPS):
        w = jnp.where(causal, sgw_ref[g], 0.0).astype(MXU_DTYPE)
        bias = sgbt_ref[:, g:g + 1]
        for ch in range(ts // SG_CHUNK):
            r0 = ch * SG_CHUNK
            vg = v[r0:r0 + SG_CHUNK, g * gd:(g + 1) * gd].astype(MXU_DTYPE)
            f = jnp.dot(w, vg, preferred_element_type=F32) + bias
            fbuf[r0:r0 + SG_CHUNK, g * gd:(g + 1) * gd] = u[r0:r0 + SG_CHUNK, g * gd:(g + 1) * gd] * f
    o_ref[...] = jnp.dot(fbuf[...].astype(MXU_DTYPE), wout_ref[...], preferred_element_type=F32)


def _gmlp(b_in, ln_g, ln_b, sg_w, sg_b, w_out):
    m, _ = b_in.shape
    c = SG_WIDTH
    d = w_out.shape[1]
    ts = 256
    kern = functools.partial(_gmlp_kernel, ts=ts)
    return pl.pallas_call(
        kern,
        grid=(m // ts,),
        in_specs=[pl.BlockSpec((ts, 2 * c), lambda i: (i, 0)),
                  pl.BlockSpec((1, c), lambda i: (0, 0)),
                  pl.BlockSpec((1, c), lambda i: (0, 0)),
                  pl.BlockSpec((SG_GROUPS, SG_CHUNK, SG_CHUNK), lambda i: (0, 0, 0)),
                  pl.BlockSpec((SG_CHUNK, SG_GROUPS), lambda i: (0, 0)),
                  pl.BlockSpec((c, d), lambda i: (0, 0))],
        out_specs=pl.BlockSpec((ts, d), lambda i: (i, 0)),
        out_shape=jax.ShapeDtypeStruct((m, d), F32),
        scratch_shapes=[pltpu.VMEM((ts, c), F32)],
        compiler_params=_cparams(("parallel",)),
        name="gmlp_gating",
    )(b_in, ln_g.reshape(1, c), ln_b.reshape(1, c), sg_w, sg_b.T, w_out.astype(MXU_DTYPE))


def _compress_kernel(p_ref, pelo_ref, pehi_ref, w1a_ref, w1b_ref, w2_ref, o_ref):
    p = p_ref[...]
    a = jnp.dot((p + pelo_ref[...]).astype(MXU_DTYPE), w1a_ref[...], preferred_element_type=F32)
    b = jnp.dot((p + pehi_ref[...]).astype(MXU_DTYPE), w1b_ref[...], preferred_element_type=F32)
    n = p.shape[0]
    b_next = pltpu.roll(b, n - 1, 0)
    hid = a + b_next
    hid = hid * _sigmoid(hid)
    o_ref[...] = jnp.dot(hid.astype(MXU_DTYPE), w2_ref[...], preferred_element_type=F32)


def _compress(pieces, pe, w1, w2):
    bsz, g, n_piece, width = pieces.shape
    half = CMP_BLOCK // 2
    pe_lo = pe[:half].reshape(1, width)
    pe_hi = pe[half:].reshape(1, width)
    w1a = w1[:half].reshape(width, CMP_HIDDEN).astype(MXU_DTYPE)
    w1b = w1[half:].reshape(width, CMP_HIDDEN).astype(MXU_DTYPE)
    return pl.pallas_call(
        _compress_kernel,
        grid=(bsz, g),
        in_specs=[pl.BlockSpec((None, None, n_piece, width), lambda b, gg: (b, gg, 0, 0)),
                  pl.BlockSpec((1, width), lambda b, gg: (0, 0)),
                  pl.BlockSpec((1, width), lambda b, gg: (0, 0)),
                  pl.BlockSpec((width, CMP_HIDDEN), lambda b, gg: (0, 0)),
                  pl.BlockSpec((width, CMP_HIDDEN), lambda b, gg: (0, 0)),
                  pl.BlockSpec((CMP_HIDDEN, HEAD_DIM), lambda b, gg: (0, 0))],
        out_specs=pl.BlockSpec((None, None, n_piece, HEAD_DIM), lambda b, gg: (b, gg, 0, 0)),
        out_shape=jax.ShapeDtypeStruct((bsz, g, n_piece, HEAD_DIM), F32),
        compiler_params=_cparams(("parallel", "parallel")),
        name="nsa_compress",
    )(pieces, pe_lo, pe_hi, w1a, w1b, w2.astype(MXU_DTYPE))


def _masked_softmax(s, mask):
    s = jnp.where(mask, s, NEG)
    m = jnp.max(s, axis=-1, keepdims=True)
    e = jnp.where(mask, jnp.exp(s - m), 0.0)
    return e / jnp.maximum(jnp.sum(e, axis=-1, keepdims=True), 1e-30)


def _qk(q, k):
    return lax.dot_general(q, k, (((1,), (1,)), ((), ())), preferred_element_type=F32)


def _nsa_kernel(q_ref, kc_ref, vc_ref, ks_ref, vs_ref, kw_ref, vw_ref, gate_ref, map_ref, o_ref, *, tq, ck, n_slc):
    hpg = HEADS_PER_GROUP
    dh = HEAD_DIM
    scale = np.float32(dh ** -0.5)
    qi = pl.program_id(2)
    q0 = qi * tq
    q = q_ref[...].reshape(hpg * tq, dh).astype(MXU_DTYPE)
    t = q0 + lax.broadcasted_iota(jnp.int32, (tq, 1), 0)

    n_cmp = kc_ref.shape[0]
    s = _qk(q, kc_ref[...].astype(MXU_DTYPE)) * scale
    cmp_end = lax.broadcasted_iota(jnp.int32, (1, n_cmp), 1) * CMP_STRIDE + (CMP_BLOCK - 1)
    cmask = cmp_end <= t
    p = _masked_softmax(s.reshape(hpg, tq, n_cmp), cmask[None])
    o_cmp = jnp.dot(p.reshape(hpg * tq, n_cmp).astype(MXU_DTYPE), vc_ref[...].astype(MXU_DTYPE),
                    preferred_element_type=F32)

    psum = p[0]
    for h in range(1, hpg):
        psum = psum + p[h]
    imp = jnp.dot(psum, map_ref[...], preferred_element_type=F32, precision=lax.Precision.HIGHEST)
    jio = lax.broadcasted_iota(jnp.int32, (tq, n_slc), 1)
    cur = t >> 6
    valid = jio <= cur
    forced = (jio == 0) | (jio == cur) | (jio == cur - 1)
    score = jnp.where(valid, jnp.where(forced, FORCE, imp), NEG)
    rank = jnp.zeros((tq, n_slc), jnp.int32)
    for i in range(n_slc):
        ci = score[:, i:i + 1]
        ahead = (ci > score) | ((ci == score) & (jio > i))
        rank = rank + ahead.astype(jnp.int32)
    sel = (rank < SLC_TOP_N).astype(MXU_DTYPE)

    blocks_per_chunk = ck // SLC_BLOCK
    n_chunks = (q0 + tq + ck - 1) // ck

    def slc_body(c, carry):
        m, l, acc = carry
        k0 = pl.multiple_of(c * ck, ck)
        k = ks_ref[pl.ds(k0, ck), :].astype(MXU_DTYPE)
        v = vs_ref[pl.ds(k0, ck), :].astype(MXU_DTYPE)
        sc = (_qk(q, k) * scale).reshape(hpg, tq, ck)
        kk = lax.broadcasted_iota(jnp.int32, (n_slc, ck), 1)
        jj = lax.broadcasted_iota(jnp.int32, (n_slc, ck), 0)
        expand = (jj == c * blocks_per_chunk + (kk >> 6)).astype(MXU_DTYPE)
        selc = jnp.dot(sel, expand, preferred_element_type=F32)
        kpos = k0 + lax.broadcasted_iota(jnp.int32, (1, ck), 1)
        mask = ((selc > 0.5) & (kpos <= t))[None]
        sc = jnp.where(mask, sc, NEG)
        m_new = jnp.maximum(m, jnp.max(sc, axis=-1, keepdims=True))
        alpha = jnp.exp(m - m_new)
        e = jnp.where(mask, jnp.exp(sc - m_new), 0.0)
        l = alpha * l + jnp.sum(e, axis=-1, keepdims=True)
        pv = jnp.dot(e.reshape(hpg * tq, ck).astype(MXU_DTYPE), v, preferred_element_type=F32)
        acc = alpha.reshape(hpg * tq, 1) * acc + pv
        return m_new, l, acc

    m0 = jnp.full((hpg, tq, 1), NEG, F32)
    l0 = jnp.zeros((hpg, tq, 1), F32)
    a0 = jnp.zeros((hpg * tq, dh), F32)
    _, l_s, acc_s = lax.fori_loop(0, n_chunks, slc_body, (m0, l0, a0))
    o_slc = acc_s / jnp.maximum(l_s, 1e-30).reshape(hpg * tq, 1)

    wlen = WINDOW + tq
    w0 = pl.multiple_of(jnp.maximum(q0 - WINDOW, 0), tq)
    kw = kw_ref[pl.ds(w0, wlen), :].astype(MXU_DTYPE)
    vw = vw_ref[pl.ds(w0, wlen), :].astype(MXU_DTYPE)
    sw = (_qk(q, kw) * scale).reshape(hpg, tq, wlen)
    kpos = w0 + lax.broadcasted_iota(jnp.int32, (1, wlen), 1)
    wmask = (kpos <= t) & (kpos > t - WINDOW)
    pw = _masked_softmax(sw, wmask[None])
    o_win = jnp.dot(pw.reshape(hpg * tq, wlen).astype(MXU_DTYPE), vw, preferred_element_type=F32)

    gl = _sigmoid(gate_ref[...])
    for h in range(hpg):
        r = slice(h * tq, (h + 1) * tq)
        o = (gl[:, 3 * h:3 * h + 1] * o_cmp[r] + gl[:, 3 * h + 1:3 * h + 2] * o_slc[r]
             + gl[:, 3 * h + 2:3 * h + 3] * o_win[r])
        o_ref[h] = o.astype(o_ref.dtype)


def _slc_map(n_piece, n_slc):
    r = CMP_BLOCK // CMP_STRIDE
    a = SLC_BLOCK // CMP_STRIDE
    m = np.zeros((n_piece, n_slc), np.float32)
    for n in range(n_piece - r + 1):
        for i in range(r):
            m[n, (n + i) // a] += 1.0
    return m


def _nsa_attention(q_r, kc, vc, kvr, gate_logits):
    bsz, g, hpg, seq, dh = q_r.shape
    n_piece = kc.shape[2]
    n_slc = seq // SLC_BLOCK
    tq = 128
    ck = 512
    kern = functools.partial(_nsa_kernel, tq=tq, ck=ck, n_slc=n_slc)

    def kv_spec(which):
        return pl.BlockSpec((None, None, None, seq, dh), lambda b, gg, i, w=which: (w, b, gg, 0, 0))

    return pl.pallas_call(
        kern,
        grid=(bsz, g, seq // tq),
        in_specs=[pl.BlockSpec((None, None, hpg, tq, dh), lambda b, gg, i: (b, gg, 0, i, 0)),
                  pl.BlockSpec((None, None, n_piece, dh), lambda b, gg, i: (b, gg, 0, 0)),
                  pl.BlockSpec((None, None, n_piece, dh), lambda b, gg, i: (b, gg, 0, 0)),
                  kv_spec(2), kv_spec(3), kv_spec(4), kv_spec(5),
                  pl.BlockSpec((None, tq, GATE_PAD), lambda b, gg, i: (b, i, gg)),
                  pl.BlockSpec((n_piece, n_slc), lambda b, gg, i: (0, 0))],
        out_specs=pl.BlockSpec((None, None, hpg, tq, dh), lambda b, gg, i: (b, gg, 0, i, 0)),
        out_shape=jax.ShapeDtypeStruct((bsz, g, hpg, seq, dh), MXU_DTYPE),
        compiler_params=_cparams(("parallel", "parallel", "arbitrary")),
        name="nsa_attention",
    )(q_r, kc, vc, kvr, kvr, kvr, kvr, gate_logits, jnp.asarray(_slc_map(n_piece, n_slc)))


def _merge_kernel(ya_ref, yb_ref, yc_ref, gl_ref, x_ref, wo_ref, nw_ref, o_ref):
    d = ya_ref.shape[1]
    gl = gl_ref[...]
    mixed = (_sigmoid(gl[:, :d]) * ya_ref[...] + _sigmoid(gl[:, d:2 * d]) * yb_ref[...]
             + _sigmoid(gl[:, 2 * d:]) * yc_ref[...])
    r = jnp.dot(mixed.astype(MXU_DTYPE), wo_ref[...], preferred_element_type=F32)
    ms = jnp.mean(r * r, axis=-1, keepdims=True)
    o_ref[...] = x_ref[...] + r * lax.rsqrt(ms + EPS) * nw_ref[...]


def _merge(ya, yb, yc, merge_logits, x2d, w_o, norm_w):
    m, d = x2d.shape
    tm = 128
    row = lambda i: (i, 0)
    return pl.pallas_call(
        _merge_kernel,
        grid=(m // tm,),
        in_specs=[pl.BlockSpec((tm, d), row), pl.BlockSpec((tm, d), row), pl.BlockSpec((tm, d), row),
                  pl.BlockSpec((tm, 3 * d), row), pl.BlockSpec((tm, d), row),
                  pl.BlockSpec((d, d), lambda i: (0, 0)),
                  pl.BlockSpec((1, d), lambda i: (0, 0))],
        out_specs=pl.BlockSpec((tm, d), row),
        out_shape=jax.ShapeDtypeStruct((m, d), F32),
        compiler_params=_cparams(("parallel",)),
        name="merge_out_proj",
    )(ya, yb, yc, merge_logits, x2d, w_o.astype(MXU_DTYPE), norm_w.reshape(1, d))


def _ffn_up_kernel(h_ref, halo_ref, wg_ref, wv_ref, cwg_ref, cwv_ref, cbg_ref, cbv_ref, o_ref,
                   hbuf, ugbuf, uvbuf, *, tm, seq):
    i = pl.program_id(0)
    j = pl.program_id(1)

    @pl.when(j == 0)
    def _():
        hbuf[pl.ds(FFN_HALO, tm), :] = h_ref[...]
        first = (i * tm) % seq == 0
        hbuf[pl.ds(0, FFN_HALO), :] = jnp.where(first, jnp.zeros_like(halo_ref[...]), halo_ref[...])

    lhs = hbuf[...]
    ugbuf[...] = jnp.dot(lhs, wg_ref[...], preferred_element_type=F32)
    uvbuf[...] = jnp.dot(lhs, wv_ref[...], preferred_element_type=F32)

    def conv(buf, cw_ref, cb_ref):
        return (cb_ref[...] + cw_ref[pl.ds(2, 1), :] * buf[pl.ds(FFN_HALO, tm), :]
                + cw_ref[pl.ds(1, 1), :] * buf[pl.ds(FFN_HALO - 1, tm), :]
                + cw_ref[pl.ds(0, 1), :] * buf[pl.ds(FFN_HALO - 2, tm), :])

    cg = conv(ugbuf, cwg_ref, cbg_ref)
    cv = conv(uvbuf, cwv_ref, cbv_ref)
    o_ref[...] = (cg * _sigmoid(cg) * cv).astype(o_ref.dtype)


def _ffn_up(h, w_up, conv_w, conv_b, seq):
    m, d = h.shape
    dff = w_up.shape[1] // 2
    tm = min(1024, seq)
    tn = _pick(dff, (512, 256, 128))
    nj = dff // tn
    per = tm // FFN_HALO
    kern = functools.partial(_ffn_up_kernel, tm=tm, seq=seq)
    cb = conv_b.reshape(1, 2 * dff)
    return pl.pallas_call(
        kern,
        grid=(m // tm, nj),
        in_specs=[pl.BlockSpec((tm, d), lambda i, j: (i, 0)),
                  pl.BlockSpec((FFN_HALO, d), lambda i, j: (jnp.maximum(i * per - 1, 0), 0)),
                  pl.BlockSpec((d, tn), lambda i, j: (0, j)),
                  pl.BlockSpec((d, tn), lambda i, j: (0, j + nj)),
                  pl.BlockSpec((3, tn), lambda i, j: (0, j)),
                  pl.BlockSpec((3, tn), lambda i, j: (0, j + nj)),
                  pl.BlockSpec((1, tn), lambda i, j: (0, j)),
                  pl.BlockSpec((1, tn), lambda i, j: (0, j + nj))],
        out_specs=pl.BlockSpec((tm, tn), lambda i, j: (i, j)),
        out_shape=jax.ShapeDtypeStruct((m, dff), MXU_DTYPE),
        scratch_shapes=[pltpu.VMEM((FFN_HALO + tm, d), MXU_DTYPE),
                        pltpu.VMEM((FFN_HALO + tm, tn), F32),
                        pltpu.VMEM((FFN_HALO + tm, tn), F32)],
        compiler_params=_cparams(("parallel", "arbitrary")),
        name="ffn_up_conv_gate",
    )(h, h, w_up, w_up, conv_w, conv_w, cb, cb)


def _ffn_down_kernel(a_ref, w_ref, x_ref, nw_ref, o_ref, acc):
    k = pl.program_id(1)

    @pl.when(k == 0)
    def _():
        acc[...] = jnp.zeros_like(acc)

    acc[...] += jnp.dot(a_ref[...], w_ref[...], preferred_element_type=F32)

    @pl.when(k == pl.num_programs(1) - 1)
    def _():
        r = acc[...]
        ms = jnp.mean(r * r, axis=-1, keepdims=True)
        o_ref[...] = x_ref[...] + r * lax.rsqrt(ms + EPS) * nw_ref[...]


def _ffn_down(act, w_down, x2d, norm_w):
    m, dff = act.shape
    d = w_down.shape[1]
    tm = min(512, m)
    tk = _pick(dff, (512, 256, 128))
    return pl.pallas_call(
        _ffn_down_kernel,
        grid=(m // tm, dff // tk),
        in_specs=[pl.BlockSpec((tm, tk), lambda i, k: (i, k)),
                  pl.BlockSpec((tk, d), lambda i, k: (k, 0)),
                  pl.BlockSpec((tm, d), lambda i, k: (i, 0)),
                  pl.BlockSpec((1, d), lambda i, k: (0, 0))],
        out_specs=pl.BlockSpec((tm, d), lambda i, k: (i, 0)),
        out_shape=jax.ShapeDtypeStruct((m, d), F32),
        scratch_shapes=[pltpu.VMEM((tm, d), F32)],
        compiler_params=_cparams(("parallel", "arbitrary")),
        name="ffn_down_norm_res",
    )(act, w_down, x2d, norm_w.reshape(1, d))


def _gate_weight(w_gate):
    d = w_gate.shape[0]
    per = HEADS_PER_GROUP * 3
    w = w_gate.reshape(d, N_KV_GROUPS, per)
    w = jnp.pad(w, ((0, 0), (0, 0), (0, GATE_PAD - per)))
    return w.reshape(d, N_KV_GROUPS * GATE_PAD)


def _mixer_layer(x, p, l):
    bsz, seq, d = x.shape
    m = bsz * seq
    x2d = x.reshape(m, d)
    g, hpg, dh = N_KV_GROUPS, HEADS_PER_GROUP, HEAD_DIM
    a_in_w = 2 * CONV_WIDTH
    b_in_w = 2 * SG_WIDTH
    q_w = N_HEADS * dh
    kv_w = 6 * g * dh
    gate_w = 3 * N_HEADS
    o0 = 0
    o1 = o0 + a_in_w
    o2 = o1 + b_in_w
    o3 = o2 + q_w
    o4 = o3 + kv_w
    o5 = o4 + gate_w
    w_in = p["w_in"][l]

    h = _rmsnorm_cast(x2d, p["norm_mix_pre"][l])
    wc = lambda w: w.astype(MXU_DTYPE)
    a_in = _matmul(h, wc(w_in[:, o0:o1]), name="proj_a")
    b_in = _matmul(h, wc(w_in[:, o1:o2]), name="proj_b")
    q = _matmul(h, wc(w_in[:, o2:o3]), name="proj_q")
    kv = _matmul(h, wc(w_in[:, o3:o4]), name="proj_kv")
    gate_logits = _matmul(h, wc(_gate_weight(w_in[:, o4:o5])), name="proj_gate")
    merge_logits = _matmul(h, wc(w_in[:, o5:]), name="proj_merge")

    y_a = _conformer(a_in.reshape(bsz, seq, a_in_w), p["conv_a_w"][l], p["conv_a_b"][l],
                     p["ln_a_g"][l], p["ln_a_b"][l], p["w_a_out"][l]).reshape(m, d)
    y_b = _gmlp(b_in, p["ln_b_g"][l], p["ln_b_b"][l], p["sg_w"][l], p["sg_b"][l], p["w_b_out"][l])

    q_r = q.reshape(bsz, seq, g, hpg, dh).transpose(0, 2, 3, 1, 4)
    kvr = kv.reshape(bsz, seq, 6, g, dh).transpose(2, 0, 3, 1, 4)
    n_piece = seq // CMP_STRIDE
    kc = _compress(kvr[0].reshape(bsz, g, n_piece, CMP_STRIDE * dh), p["cmp_pe_k"][l], p["cmp_w1_k"][l],
                   p["cmp_w2_k"][l])
    vc = _compress(kvr[1].reshape(bsz, g, n_piece, CMP_STRIDE * dh), p["cmp_pe_v"][l], p["cmp_w1_v"][l],
                   p["cmp_w2_v"][l])
    o = _nsa_attention(q_r, kc, vc, kvr, gate_logits.reshape(bsz, seq, g * GATE_PAD))
    o2d = o.transpose(0, 3, 1, 2, 4).reshape(m, N_HEADS * dh)
    y_c = _matmul(o2d, wc(p["w_c_out"][l]), name="proj_c_out")

    return _merge(y_a, y_b, y_c, merge_logits, x2d, p["w_o"][l], p["norm_mix_post"][l]).reshape(bsz, seq, d)


def _ffn_layer(x, p, l):
    bsz, seq, d = x.shape
    m = bsz * seq
    x2d = x.reshape(m, d)
    h = _rmsnorm_cast(x2d, p["norm_ffn_pre"][l])
    act = _ffn_up(h, p["w_up"][l].astype(MXU_DTYPE), p["ffn_conv_w"][l], p["ffn_conv_b"][l], seq)
    return _ffn_down(act, p["w_down"][l].astype(MXU_DTYPE), x2d, p["norm_ffn_post"][l]).reshape(bsz, seq, d)


def kernel(x, norm_mix_pre, norm_mix_post, norm_ffn_pre, norm_ffn_post, w_in, conv_a_w, conv_a_b, ln_a_g, ln_a_b,
           w_a_out, ln_b_g, ln_b_b, sg_w, sg_b, w_b_out, cmp_pe_k, cmp_w1_k, cmp_w2_k, cmp_pe_v, cmp_w1_v,
           cmp_w2_v, w_c_out, w_o, w_up, ffn_conv_w, ffn_conv_b, w_down):
    p = dict(norm_mix_pre=norm_mix_pre, norm_mix_post=norm_mix_post, norm_ffn_pre=norm_ffn_pre,
             norm_ffn_post=norm_ffn_post, w_in=w_in, conv_a_w=conv_a_w, conv_a_b=conv_a_b, ln_a_g=ln_a_g,
             ln_a_b=ln_a_b, w_a_out=w_a_out, ln_b_g=ln_b_g, ln_b_b=ln_b_b, sg_w=sg_w, sg_b=sg_b, w_b_out=w_b_out,
             cmp_pe_k=cmp_pe_k, cmp_w1_k=cmp_w1_k, cmp_w2_k=cmp_w2_k, cmp_pe_v=cmp_pe_v, cmp_w1_v=cmp_w1_v,
             cmp_w2_v=cmp_w2_v, w_c_out=w_c_out, w_o=w_o, w_up=w_up, ffn_conv_w=ffn_conv_w,
             ffn_conv_b=ffn_conv_b, w_down=w_down)
    for l in range(w_in.shape[0]):
        x = _mixer_layer(x, p, l)
        x = _ffn_layer(x, p, l)
    return x
```

```python
import functools

import numpy as np
import jax
import jax.numpy as jnp
from jax import lax
from jax.experimental import pallas as pl
from jax.experimental.pallas import tpu as pltpu

F32 = jnp.float32
MXU_DTYPE = jnp.bfloat16

EPS = 1e-6
NEG = -1e30
FORCE = 1e4

CONV_WIDTH = 1024
DW_CONV_SIZE = 31
CONV_HALO = 32
SG_WIDTH = 1024
SG_CHUNK = 128
SG_GROUPS = 8
N_HEADS = 16
N_KV_GROUPS = 4
HEADS_PER_GROUP = 4
HEAD_DIM = 64
CMP_BLOCK = 32
CMP_STRIDE = 16
CMP_HIDDEN = 256
SLC_BLOCK = 64
SLC_TOP_N = 16
WINDOW = 512
D_FF = 5632
FFN_HALO = 16
GATE_PAD = 128

VMEM_LIMIT = 56 * 1024 * 1024


def _cparams(sem):
    return pltpu.CompilerParams(dimension_semantics=sem, vmem_limit_bytes=VMEM_LIMIT)


def _sigmoid(x):
    return jax.nn.sigmoid(x)


def _rmsnorm_kernel(x_ref, w_ref, o_ref):
    x = x_ref[...]
    ms = jnp.mean(x * x, axis=-1, keepdims=True)
    o_ref[...] = (x * lax.rsqrt(ms + EPS) * w_ref[...]).astype(o_ref.dtype)


def _rmsnorm_cast(x2d, w):
    m, d = x2d.shape
    tm = min(512, m)
    return pl.pallas_call(
        _rmsnorm_kernel,
        grid=(m // tm,),
        in_specs=[pl.BlockSpec((tm, d), lambda i: (i, 0)),
                  pl.BlockSpec((1, d), lambda i: (0, 0))],
        out_specs=pl.BlockSpec((tm, d), lambda i: (i, 0)),
        out_shape=jax.ShapeDtypeStruct((m, d), MXU_DTYPE),
        compiler_params=_cparams(("parallel",)),
        name="rmsnorm_cast",
    )(x2d, w.reshape(1, d))


def _mm_kernel(a_ref, w_ref, o_ref):
    o_ref[...] = jnp.dot(a_ref[...], w_ref[...], preferred_element_type=F32).astype(o_ref.dtype)


def _pick(n, cands):
    for c in cands:
        if n % c == 0:
            return c
    raise ValueError(f"no tile for {n}")


def _matmul(a, w, out_dtype=F32, name="matmul"):
    m, k = a.shape
    n = w.shape[1]
    tm = min(1024, m)
    tn = _pick(n, (512, 384, 256, 128))
    return pl.pallas_call(
        _mm_kernel,
        grid=(m // tm, n // tn),
        in_specs=[pl.BlockSpec((tm, k), lambda i, j: (i, 0)),
                  pl.BlockSpec((k, tn), lambda i, j: (0, j))],
        out_specs=pl.BlockSpec((tm, tn), lambda i, j: (i, j)),
        out_shape=jax.ShapeDtypeStruct((m, n), out_dtype),
        compiler_params=_cparams(("parallel", "arbitrary")),
        name=name,
    )(a, w)


def _conformer_kernel(cur_ref, halo_ref, cw_ref, cb_ref, lg_ref, lb_ref, wout_ref, o_ref, hbuf, cbuf, *, ts):
    c = CONV_WIDTH
    i = pl.program_id(1)
    cur = cur_ref[...]
    hbuf[pl.ds(CONV_HALO, ts), :] = cur[:, :c] * _sigmoid(cur[:, c:])
    hal = halo_ref[...]
    hh = hal[:, :c] * _sigmoid(hal[:, c:])
    hbuf[pl.ds(0, CONV_HALO), :] = jnp.where(i == 0, 0.0, hh)

    def col_body(cc, carry):
        off = pl.multiple_of(cc * 128, 128)
        acc = jnp.broadcast_to(cb_ref[:, pl.ds(off, 128)], (ts, 128))
        for k in range(DW_CONV_SIZE):
            acc = acc + cw_ref[pl.ds(k, 1), pl.ds(off, 128)] * hbuf[pl.ds(CONV_HALO - (DW_CONV_SIZE - 1) + k, ts),
                                                                   pl.ds(off, 128)]
        cbuf[:, pl.ds(off, 128)] = acc
        return carry

    lax.fori_loop(0, c // 128, col_body, 0)
    y = cbuf[...]
    mean = jnp.mean(y, axis=-1, keepdims=True)
    yc = y - mean
    var = jnp.mean(yc * yc, axis=-1, keepdims=True)
    z = yc * lax.rsqrt(var + EPS) * lg_ref[...] + lb_ref[...]
    z = z * _sigmoid(z)
    o_ref[...] = jnp.dot(z.astype(MXU_DTYPE), wout_ref[...], preferred_element_type=F32)


def _conformer(a_in, conv_w, conv_b, ln_g, ln_b, w_out):
    bsz, seq, _ = a_in.shape
    c = CONV_WIDTH
    d = w_out.shape[1]
    ts = min(256, seq)
    per = ts // CONV_HALO
    kern = functools.partial(_conformer_kernel, ts=ts)
    return pl.pallas_call(
        kern,
        grid=(bsz, seq // ts),
        in_specs=[pl.BlockSpec((None, ts, 2 * c), lambda b, i: (b, i, 0)),
                  pl.BlockSpec((None, CONV_HALO, 2 * c), lambda b, i: (b, jnp.maximum(i * per - 1, 0), 0)),
                  pl.BlockSpec((DW_CONV_SIZE, c), lambda b, i: (0, 0)),
                  pl.BlockSpec((1, c), lambda b, i: (0, 0)),
                  pl.BlockSpec((1, c), lambda b, i: (0, 0)),
                  pl.BlockSpec((1, c), lambda b, i: (0, 0)),
                  pl.BlockSpec((c, d), lambda b, i: (0, 0))],
        out_specs=pl.BlockSpec((None, ts, d), lambda b, i: (b, i, 0)),
        out_shape=jax.ShapeDtypeStruct((bsz, seq, d), F32),
        scratch_shapes=[pltpu.VMEM((CONV_HALO + ts, c), F32), pltpu.VMEM((ts, c), F32)],
        compiler_params=_cparams(("parallel", "parallel")),
        name="conformer_conv",
    )(a_in, a_in, conv_w, conv_b.reshape(1, c), ln_g.reshape(1, c), ln_b.reshape(1, c), w_out.astype(MXU_DTYPE))


def _gmlp_kernel(x_ref, lg_ref, lb_ref, sgw_ref, sgbt_ref, wout_ref, o_ref, fbuf, *, ts):
    c = SG_WIDTH
    gd = c // SG_GROUPS
    x = x_ref[...]
    z = 0.5 * x * (1.0 + lax.erf(x * np.float32(np.sqrt(0.5))))
    u = z[:, :c]
    v = z[:, c:]
    mean = jnp.mean(v, axis=-1, keepdims=True)
    vc = v - mean
    var = jnp.mean(vc * vc, axis=-1, keepdims=True)
    v = vc * lax.rsqrt(var + EPS) * lg_ref[...] + lb_ref[...]
    row = lax.broadcasted_iota(jnp.int32, (SG_CHUNK, SG_CHUNK), 0)
    col = lax.broadcasted_iota(jnp.int32, (SG_CHUNK, SG_CHUNK), 1)
    causal = col <= row
    for g in range(SG_GROUPS):
        w = jnp.where(causal, sgw_ref[g], 0.0).astype(MXU_DTYPE)
        bias = sgbt_ref[:, g:g + 1]
        for ch in range(ts // SG_CHUNK):
            r0 = ch * SG_CHUNK
            vg = v[r0:r0 + SG_CHUNK, g * gd:(g + 1) * gd].astype(MXU_DTYPE)
            f = jnp.dot(w, vg, preferred_element_type=F32) + bias
            fbuf[r0:r0 + SG_CHUNK, g * gd:(g + 1) * gd] = u[r0:r0 + SG_CHUNK, g * gd:(g + 1) * gd] * f
    o_ref[...] = jnp.dot(fbuf[...].astype(MXU_DTYPE), wout_ref[...], preferred_element_type=F32)


def _gmlp(b_in, ln_g, ln_b, sg_w, sg_b, w_out):
    m, _ = b_in.shape
    c = SG_WIDTH
    d = w_out.shape[1]
    ts = 256
    kern = functools.partial(_gmlp_kernel, ts=ts)
    return pl.pallas_call(
        kern,
        grid=(m // ts,),
        in_specs=[pl.BlockSpec((ts, 2 * c), lambda i: (i, 0)),
                  pl.BlockSpec((1, c), lambda i: (0, 0)),
                  pl.BlockSpec((1, c), lambda i: (0, 0)),
                  pl.BlockSpec((SG_GROUPS, SG_CHUNK, SG_CHUNK), lambda i: (0, 0, 0)),
                  pl.BlockSpec((SG_CHUNK, SG_GROUPS), lambda i: (0, 0)),
                  pl.BlockSpec((c, d), lambda i: (0, 0))],
        out_specs=pl.BlockSpec((ts, d), lambda i: (i, 0)),
        out_shape=jax.ShapeDtypeStruct((m, d), F32),
        scratch_shapes=[pltpu.VMEM((ts, c), F32)],
        compiler_params=_cparams(("parallel",)),
        name="gmlp_gating",
    )(b_in, ln_g.reshape(1, c), ln_b.reshape(1, c), sg_w, sg_b.T, w_out.astype(MXU_DTYPE))


def _compress_kernel(p_ref, pelo_ref, pehi_ref, w1a_ref, w1b_ref, w2_ref, o_ref):
    p = p_ref[...]
    a = jnp.dot((p + pelo_ref[...]).astype(MXU_DTYPE), w1a_ref[...], preferred_element_type=F32)
    b = jnp.dot((p + pehi_ref[...]).astype(MXU_DTYPE), w1b_ref[...], preferred_element_type=F32)
    n = p.shape[0]
    b_next = pltpu.roll(b, n - 1, 0)
    hid = a + b_next
    hid = hid * _sigmoid(hid)
    o_ref[...] = jnp.dot(hid.astype(MXU_DTYPE), w2_ref[...], preferred_element_type=F32)


def _compress(pieces, pe, w1, w2):
    bsz, g, n_piece, width = pieces.shape
    half = CMP_BLOCK // 2
    pe_lo = pe[:half].reshape(1, width)
    pe_hi = pe[half:].reshape(1, width)
    w1a = w1[:half].reshape(width, CMP_HIDDEN).astype(MXU_DTYPE)
    w1b = w1[half:].reshape(width, CMP_HIDDEN).astype(MXU_DTYPE)
    return pl.pallas_call(
        _compress_kernel,
        grid=(bsz, g),
        in_specs=[pl.BlockSpec((None, None, n_piece, width), lambda b, gg: (b, gg, 0, 0)),
                  pl.BlockSpec((1, width), lambda b, gg: (0, 0)),
                  pl.BlockSpec((1, width), lambda b, gg: (0, 0)),
                  pl.BlockSpec((width, CMP_HIDDEN), lambda b, gg: (0, 0)),
                  pl.BlockSpec((width, CMP_HIDDEN), lambda b, gg: (0, 0)),
                  pl.BlockSpec((CMP_HIDDEN, HEAD_DIM), lambda b, gg: (0, 0))],
        out_specs=pl.BlockSpec((None, None, n_piece, HEAD_DIM), lambda b, gg: (b, gg, 0, 0)),
        out_shape=jax.ShapeDtypeStruct((bsz, g, n_piece, HEAD_DIM), F32),
        compiler_params=_cparams(("parallel", "parallel")),
        name="nsa_compress",
    )(pieces, pe_lo, pe_hi, w1a, w1b, w2.astype(MXU_DTYPE))


ROW_BLK = 32
M_INIT = -3e38


def _masked_softmax(s, mask):
    s = jnp.where(mask, s, NEG)
    m = jnp.max(s, axis=-1, keepdims=True)
    e = jnp.where(mask, jnp.exp(s - m), 0.0)
    return e / jnp.maximum(jnp.sum(e, axis=-1, keepdims=True), 1e-30)


def _nt_dot(a, b, precision=None):
    return lax.dot_general(a, b, (((1,), (1,)), ((), ())), preferred_element_type=F32, precision=precision)


def _nsa_kernel(q_ref, kc_ref, vc_ref, ks_ref, vs_ref, kw_ref, vw_ref, gate_ref, mapt_ref, bt_ref, bandt_ref, o_ref,
                s_scr, p_scr, m_scr, al_scr, acc_scr, psum_scr, sc_scr, *, tq, ck):
    hpg = HEADS_PER_GROUP
    dh = HEAD_DIM
    aug = 2 * dh
    rows = hpg * tq
    n_blk = mapt_ref.shape[0]
    qi = pl.program_id(2)
    q0 = qi * tq
    qp = (q_ref[...] * np.float32(dh ** -0.5).astype(q_ref.dtype)).reshape(rows, aug)

    def online_chunk(lhs, rhs, v_aug, nk):
        s_scr[:, :nk] = _nt_dot(lhs, rhs)
        for rb in range(rows // ROW_BLK):
            r = pl.ds(rb * ROW_BLK, ROW_BLK)
            s = s_scr[r, :nk]
            m_old = m_scr[r, :]
            m_new = jnp.maximum(m_old, jnp.max(s, axis=-1, keepdims=True))
            p_scr[r, :nk] = jnp.exp(s - m_new).astype(p_scr.dtype)
            al_scr[r, :] = jnp.exp(m_old - m_new)
            m_scr[r, :] = m_new
        acc_scr[...] = al_scr[...] * acc_scr[...] + jnp.dot(p_scr[:, :nk], v_aug, preferred_element_type=F32)

    def reset_online():
        m_scr[...] = jnp.full(m_scr.shape, M_INIT, F32)
        acc_scr[...] = jnp.zeros(acc_scr.shape, F32)

    def finish_online():
        acc = acc_scr[...]
        return acc[:, :dh] / jnp.maximum(acc[:, dh:dh + 1], 1e-30)

    n_cmp = kc_ref.shape[0]
    s_scr[:, :n_cmp] = _nt_dot(qp[:, :dh], kc_ref[...].astype(MXU_DTYPE))
    cmp_end = lax.broadcasted_iota(jnp.int32, (1, n_cmp), 1) * CMP_STRIDE + (CMP_BLOCK - 1)
    for tb in range(tq // ROW_BLK):
        t_rb = q0 + tb * ROW_BLK + lax.broadcasted_iota(jnp.int32, (ROW_BLK, 1), 0)
        cmask = cmp_end <= t_rb
        ps = jnp.zeros((ROW_BLK, n_cmp), F32)
        for h in range(hpg):
            r = pl.ds(h * tq + tb * ROW_BLK, ROW_BLK)
            p = _masked_softmax(s_scr[r, :n_cmp], cmask)
            p_scr[r, :n_cmp] = p.astype(p_scr.dtype)
            ps = ps + p
        psum_scr[pl.ds(tb * ROW_BLK, ROW_BLK), :] = ps
    o_cmp = jnp.dot(p_scr[:, :n_cmp], vc_ref[...].astype(MXU_DTYPE), preferred_element_type=F32)

    imp_t = _nt_dot(mapt_ref[...], psum_scr[...], precision=lax.Precision.HIGHEST)
    j_t = lax.broadcasted_iota(jnp.int32, (n_blk, tq), 0)
    cur_t = (q0 + lax.broadcasted_iota(jnp.int32, (n_blk, tq), 1)) >> 6
    forced = (j_t == 0) | (j_t == cur_t) | (j_t == cur_t - 1)
    score_t = jnp.where(j_t <= cur_t, jnp.where(forced, FORCE, imp_t), NEG)
    sc_scr[...] = score_t
    sel_rows = []
    for v in range(n_blk // 8):
        sj = score_t[8 * v:8 * v + 8]
        jv = j_t[8 * v:8 * v + 8]
        rank = jnp.zeros((8, tq), jnp.int32)
        for i in range(n_blk):
            ci = sc_scr[pl.ds(i, 1), :]
            if i < 8 * v:
                ahead = ci >= sj
            elif i > 8 * v + 7:
                ahead = ci > sj
            else:
                ahead = (ci > sj) | ((ci == sj) & (jv > i))
            rank = rank + ahead.astype(jnp.int32)
        sel_rows.append(jnp.where(rank < SLC_TOP_N, 0.0, NEG))
    selb_t = jnp.concatenate([jnp.zeros((dh, tq), F32)] + sel_rows, axis=0).astype(MXU_DTYPE)
    eye4 = ((lax.broadcasted_iota(jnp.int32, (rows, tq), 0) & (tq - 1))
            == lax.broadcasted_iota(jnp.int32, (rows, tq), 1)).astype(MXU_DTYPE)
    selb = _nt_dot(eye4[:tq], selb_t).astype(MXU_DTYPE)
    lhs_slc = (qp.reshape(hpg, tq, aug) + selb[None]).reshape(rows, aug)

    reset_online()
    n_full = q0 // ck

    def slc_body(c, carry):
        k0 = pl.multiple_of(c * ck, ck)
        online_chunk(lhs_slc, ks_ref[pl.ds(k0, ck), :], vs_ref[pl.ds(k0, ck), :], ck)
        return carry

    lax.fori_loop(0, n_full, slc_body, 0)
    qa = pl.multiple_of(n_full * ck, ck)
    off = pl.multiple_of(q0 - qa, tq)
    rhs_diag = jnp.concatenate([ks_ref[pl.ds(qa, ck), :], bt_ref[:, pl.ds(off, tq)]], axis=1)
    online_chunk(jnp.concatenate([lhs_slc, eye4], axis=1), rhs_diag, vs_ref[pl.ds(qa, ck), :], ck)
    o_slc = finish_online()

    reset_online()
    wlen = WINDOW + tq
    w0 = pl.multiple_of(q0, tq)
    pad_flag = (lax.broadcasted_iota(jnp.int32, (1, aug), 1) == dh).astype(MXU_DTYPE)
    lhs_win = jnp.concatenate([qp + pad_flag, eye4], axis=1)
    rhs_win = jnp.concatenate([kw_ref[pl.ds(w0, wlen), :], bandt_ref[...]], axis=1)
    online_chunk(lhs_win, rhs_win, vw_ref[pl.ds(w0, wlen), :], wlen)
    o_win = finish_online()

    gl = _sigmoid(gate_ref[...])
    for h in range(hpg):
        r = slice(h * tq, (h + 1) * tq)
        o = (gl[:, 3 * h:3 * h + 1] * o_cmp[r] + gl[:, 3 * h + 1:3 * h + 2] * o_slc[r]
             + gl[:, 3 * h + 2:3 * h + 3] * o_win[r])
        o_ref[h] = o.astype(o_ref.dtype)


def _slc_map_t(n_piece, n_slc, n_blk):
    r = CMP_BLOCK // CMP_STRIDE
    a = SLC_BLOCK // CMP_STRIDE
    m = np.zeros((n_blk, n_piece), np.float32)
    for n in range(n_piece - r + 1):
        for i in range(r):
            m[(n + i) // a, n] += 1.0
    return m


def _nsa_attention(q, kvr, kc, vc, gate_logits):
    bsz, seq, _ = q.shape
    g, hpg, dh = N_KV_GROUPS, HEADS_PER_GROUP, HEAD_DIM
    aug = 2 * dh
    n_piece = kc.shape[2]
    n_slc = seq // SLC_BLOCK
    n_blk = dh
    assert n_slc <= n_blk
    tq = 128
    ck = min(512, seq)
    wlen = WINDOW + tq
    rows = hpg * tq

    q_r = jnp.pad(q.reshape(bsz, seq, g, hpg, dh), ((0, 0),) * 4 + ((0, dh),)).transpose(0, 2, 3, 1, 4)
    q_r = q_r.astype(MXU_DTYPE)
    blk = np.arange(seq) // SLC_BLOCK
    onehot = (blk[:, None] == np.arange(dh)[None, :]).astype(np.float32)
    ones_col = np.zeros((seq, dh), np.float32)
    ones_col[:, 0] = 1.0
    bc = lambda c: jnp.broadcast_to(jnp.asarray(c, MXU_DTYPE), (bsz, g) + c.shape)
    ks = jnp.concatenate([kvr[2].astype(MXU_DTYPE), bc(onehot)], axis=-1)
    vs = jnp.concatenate([kvr[3].astype(MXU_DTYPE), bc(ones_col)], axis=-1)
    pad_k = np.zeros((WINDOW, aug), np.float32)
    pad_k[:, dh] = NEG
    kw = jnp.concatenate([kvr[4].astype(MXU_DTYPE), jnp.zeros((bsz, g, seq, dh), MXU_DTYPE)], axis=-1)
    kw = jnp.concatenate([bc(pad_k), kw], axis=2)
    vw = jnp.concatenate([kvr[5].astype(MXU_DTYPE), bc(ones_col)], axis=-1)
    vw = jnp.concatenate([jnp.zeros((bsz, g, WINDOW, aug), MXU_DTYPE), vw], axis=2)

    kk = np.arange(ck)[:, None]
    bt = np.where(kk <= np.arange(ck)[None, :], 0.0, NEG).astype(np.float32)
    kw_i = np.arange(wlen)[:, None]
    tt = np.arange(tq)[None, :]
    bandt = np.where((kw_i <= tt + WINDOW) & (kw_i > tt), 0.0, NEG).astype(np.float32)

    kern = functools.partial(_nsa_kernel, tq=tq, ck=ck)
    per_group = lambda n: pl.BlockSpec((None, None, n, aug), lambda b, gg, i: (b, gg, 0, 0))
    const = lambda shape: pl.BlockSpec(shape, lambda b, gg, i: (0, 0))
    o = pl.pallas_call(
        kern,
        grid=(bsz, g, seq // tq),
        in_specs=[pl.BlockSpec((None, None, hpg, tq, aug), lambda b, gg, i: (b, gg, 0, i, 0)),
                  pl.BlockSpec((None, None, n_piece, dh), lambda b, gg, i: (b, gg, 0, 0)),
                  pl.BlockSpec((None, None, n_piece, dh), lambda b, gg, i: (b, gg, 0, 0)),
                  per_group(seq), per_group(seq), per_group(WINDOW + seq), per_group(WINDOW + seq),
                  pl.BlockSpec((None, tq, GATE_PAD), lambda b, gg, i: (b, i, gg)),
                  const((n_blk, n_piece)), const((ck, ck)), const((wlen, tq))],
        out_specs=pl.BlockSpec((None, None, hpg, tq, dh), lambda b, gg, i: (b, gg, 0, i, 0)),
        out_shape=jax.ShapeDtypeStruct((bsz, g, hpg, seq, dh), MXU_DTYPE),
        scratch_shapes=[pltpu.VMEM((rows, wlen), F32),
                        pltpu.VMEM((rows, wlen), MXU_DTYPE),
                        pltpu.VMEM((rows, 1), F32),
                        pltpu.VMEM((rows, 1), F32),
                        pltpu.VMEM((rows, aug), F32),
                        pltpu.VMEM((tq, n_piece), F32),
                        pltpu.VMEM((n_blk, tq), F32)],
        compiler_params=_cparams(("parallel", "parallel", "arbitrary")),
        name="nsa_attention",
    )(q_r, kc, vc, ks, vs, kw, vw, gate_logits, jnp.asarray(_slc_map_t(n_piece, n_slc, n_blk)),
      jnp.asarray(bt, MXU_DTYPE), jnp.asarray(bandt, MXU_DTYPE))
    return o.transpose(0, 3, 1, 2, 4).reshape(bsz, seq, g * hpg * dh)


def _merge_kernel(ya_ref, yb_ref, yc_ref, gl_ref, x_ref, wo_ref, nw_ref, o_ref):
    d = ya_ref.shape[1]
    gl = gl_ref[...]
    mixed = (_sigmoid(gl[:, :d]) * ya_ref[...] + _sigmoid(gl[:, d:2 * d]) * yb_ref[...]
             + _sigmoid(gl[:, 2 * d:]) * yc_ref[...])
    r = jnp.dot(mixed.astype(MXU_DTYPE), wo_ref[...], preferred_element_type=F32)
    ms = jnp.mean(r * r, axis=-1, keepdims=True)
    o_ref[...] = x_ref[...] + r * lax.rsqrt(ms + EPS) * nw_ref[...]


def _merge(ya, yb, yc, merge_logits, x2d, w_o, norm_w):
    m, d = x2d.shape
    tm = 128
    row = lambda i: (i, 0)
    return pl.pallas_call(
        _merge_kernel,
        grid=(m // tm,),
        in_specs=[pl.BlockSpec((tm, d), row), pl.BlockSpec((tm, d), row), pl.BlockSpec((tm, d), row),
                  pl.BlockSpec((tm, 3 * d), row), pl.BlockSpec((tm, d), row),
                  pl.BlockSpec((d, d), lambda i: (0, 0)),
                  pl.BlockSpec((1, d), lambda i: (0, 0))],
        out_specs=pl.BlockSpec((tm, d), row),
        out_shape=jax.ShapeDtypeStruct((m, d), F32),
        compiler_params=_cparams(("parallel",)),
        name="merge_out_proj",
    )(ya, yb, yc, merge_logits, x2d, w_o.astype(MXU_DTYPE), norm_w.reshape(1, d))


def _ffn_up_kernel(h_ref, halo_ref, wg_ref, wv_ref, cwg_ref, cwv_ref, cbg_ref, cbv_ref, o_ref,
                   hbuf, ugbuf, uvbuf, *, tm, seq):
    i = pl.program_id(0)
    j = pl.program_id(1)

    @pl.when(j == 0)
    def _():
        hbuf[pl.ds(FFN_HALO, tm), :] = h_ref[...]
        first = (i * tm) % seq == 0
        hbuf[pl.ds(0, FFN_HALO), :] = jnp.where(first, jnp.zeros_like(halo_ref[...]), halo_ref[...])

    lhs = hbuf[...]
    ugbuf[...] = jnp.dot(lhs, wg_ref[...], preferred_element_type=F32)
    uvbuf[...] = jnp.dot(lhs, wv_ref[...], preferred_element_type=F32)

    def conv(buf, cw_ref, cb_ref):
        return (cb_ref[...] + cw_ref[pl.ds(2, 1), :] * buf[pl.ds(FFN_HALO, tm), :]
                + cw_ref[pl.ds(1, 1), :] * buf[pl.ds(FFN_HALO - 1, tm), :]
                + cw_ref[pl.ds(0, 1), :] * buf[pl.ds(FFN_HALO - 2, tm), :])

    cg = conv(ugbuf, cwg_ref, cbg_ref)
    cv = conv(uvbuf, cwv_ref, cbv_ref)
    o_ref[...] = (cg * _sigmoid(cg) * cv).astype(o_ref.dtype)


def _ffn_up(h, w_up, conv_w, conv_b, seq):
    m, d = h.shape
    dff = w_up.shape[1] // 2
    tm = min(1024, seq)
    tn = _pick(dff, (512, 256, 128))
    nj = dff // tn
    per = tm // FFN_HALO
    kern = functools.partial(_ffn_up_kernel, tm=tm, seq=seq)
    cb = conv_b.reshape(1, 2 * dff)
    return pl.pallas_call(
        kern,
        grid=(m // tm, nj),
        in_specs=[pl.BlockSpec((tm, d), lambda i, j: (i, 0)),
                  pl.BlockSpec((FFN_HALO, d), lambda i, j: (jnp.maximum(i * per - 1, 0), 0)),
                  pl.BlockSpec((d, tn), lambda i, j: (0, j)),
                  pl.BlockSpec((d, tn), lambda i, j: (0, j + nj)),
                  pl.BlockSpec((3, tn), lambda i, j: (0, j)),
                  pl.BlockSpec((3, tn), lambda i, j: (0, j + nj)),
                  pl.BlockSpec((1, tn), lambda i, j: (0, j)),
                  pl.BlockSpec((1, tn), lambda i, j: (0, j + nj))],
        out_specs=pl.BlockSpec((tm, tn), lambda i, j: (i, j)),
        out_shape=jax.ShapeDtypeStruct((m, dff), MXU_DTYPE),
        scratch_shapes=[pltpu.VMEM((FFN_HALO + tm, d), MXU_DTYPE),
                        pltpu.VMEM((FFN_HALO + tm, tn), F32),
                        pltpu.VMEM((FFN_HALO + tm, tn), F32)],
        compiler_params=_cparams(("parallel", "arbitrary")),
        name="ffn_up_conv_gate",
    )(h, h, w_up, w_up, conv_w, conv_w, cb, cb)


def _ffn_down_kernel(a_ref, w_ref, x_ref, nw_ref, o_ref, acc):
    k = pl.program_id(1)

    @pl.when(k == 0)
    def _():
        acc[...] = jnp.zeros_like(acc)

    acc[...] += jnp.dot(a_ref[...], w_ref[...], preferred_element_type=F32)

    @pl.when(k == pl.num_programs(1) - 1)
    def _():
        r = acc[...]
        ms = jnp.mean(r * r, axis=-1, keepdims=True)
        o_ref[...] = x_ref[...] + r * lax.rsqrt(ms + EPS) * nw_ref[...]


def _ffn_down(act, w_down, x2d, norm_w):
    m, dff = act.shape
    d = w_down.shape[1]
    tm = min(512, m)
    tk = _pick(dff, (512, 256, 128))
    return pl.pallas_call(
        _ffn_down_kernel,
        grid=(m // tm, dff // tk),
        in_specs=[pl.BlockSpec((tm, tk), lambda i, k: (i, k)),
                  pl.BlockSpec((tk, d), lambda i, k: (k, 0)),
                  pl.BlockSpec((tm, d), lambda i, k: (i, 0)),
                  pl.BlockSpec((1, d), lambda i, k: (0, 0))],
        out_specs=pl.BlockSpec((tm, d), lambda i, k: (i, 0)),
        out_shape=jax.ShapeDtypeStruct((m, d), F32),
        scratch_shapes=[pltpu.VMEM((tm, d), F32)],
        compiler_params=_cparams(("parallel", "arbitrary")),
        name="ffn_down_norm_res",
    )(act, w_down, x2d, norm_w.reshape(1, d))


def _gate_weight(w_gate):
    d = w_gate.shape[0]
    per = HEADS_PER_GROUP * 3
    w = w_gate.reshape(d, N_KV_GROUPS, per)
    w = jnp.pad(w, ((0, 0), (0, 0), (0, GATE_PAD - per)))
    return w.reshape(d, N_KV_GROUPS * GATE_PAD)


def _mixer_layer(x, p, l):
    bsz, seq, d = x.shape
    m = bsz * seq
    x2d = x.reshape(m, d)
    g, hpg, dh = N_KV_GROUPS, HEADS_PER_GROUP, HEAD_DIM
    a_in_w = 2 * CONV_WIDTH
    b_in_w = 2 * SG_WIDTH
    q_w = N_HEADS * dh
    kv_w = 6 * g * dh
    gate_w = 3 * N_HEADS
    o0 = 0
    o1 = o0 + a_in_w
    o2 = o1 + b_in_w
    o3 = o2 + q_w
    o4 = o3 + kv_w
    o5 = o4 + gate_w
    w_in = p["w_in"][l]

    h = _rmsnorm_cast(x2d, p["norm_mix_pre"][l])
    wc = lambda w: w.astype(MXU_DTYPE)
    a_in = _matmul(h, wc(w_in[:, o0:o1]), name="proj_a")
    b_in = _matmul(h, wc(w_in[:, o1:o2]), name="proj_b")
    q = _matmul(h, wc(w_in[:, o2:o3]), name="proj_q")
    kv = _matmul(h, wc(w_in[:, o3:o4]), name="proj_kv")
    gate_logits = _matmul(h, wc(_gate_weight(w_in[:, o4:o5])), name="proj_gate")
    merge_logits = _matmul(h, wc(w_in[:, o5:]), name="proj_merge")

    y_a = _conformer(a_in.reshape(bsz, seq, a_in_w), p["conv_a_w"][l], p["conv_a_b"][l],
                     p["ln_a_g"][l], p["ln_a_b"][l], p["w_a_out"][l]).reshape(m, d)
    y_b = _gmlp(b_in, p["ln_b_g"][l], p["ln_b_b"][l], p["sg_w"][l], p["sg_b"][l], p["w_b_out"][l])

    kvr = kv.reshape(bsz, seq, 6, g, dh).transpose(2, 0, 3, 1, 4)
    n_piece = seq // CMP_STRIDE
    kc = _compress(kvr[0].reshape(bsz, g, n_piece, CMP_STRIDE * dh), p["cmp_pe_k"][l], p["cmp_w1_k"][l],
                   p["cmp_w2_k"][l])
    vc = _compress(kvr[1].reshape(bsz, g, n_piece, CMP_STRIDE * dh), p["cmp_pe_v"][l], p["cmp_w1_v"][l],
                   p["cmp_w2_v"][l])
    o = _nsa_attention(q.reshape(bsz, seq, N_HEADS * dh), kvr, kc, vc, gate_logits.reshape(bsz, seq, g * GATE_PAD))
    y_c = _matmul(o.reshape(m, N_HEADS * dh), wc(p["w_c_out"][l]), name="proj_c_out")

    return _merge(y_a, y_b, y_c, merge_logits, x2d, p["w_o"][l], p["norm_mix_post"][l]).reshape(bsz, seq, d)


def _ffn_layer(x, p, l):
    bsz, seq, d = x.shape
    m = bsz * seq
    x2d = x.reshape(m, d)
    h = _rmsnorm_cast(x2d, p["norm_ffn_pre"][l])
    act = _ffn_up(h, p["w_up"][l].astype(MXU_DTYPE), p["ffn_conv_w"][l], p["ffn_conv_b"][l], seq)
    return _ffn_down(act, p["w_down"][l].astype(MXU_DTYPE), x2d, p["norm_ffn_post"][l]).reshape(bsz, seq, d)


def kernel(x, norm_mix_pre, norm_mix_post, norm_ffn_pre, norm_ffn_post, w_in, conv_a_w, conv_a_b, ln_a_g, ln_a_b,
           w_a_out, ln_b_g, ln_b_b, sg_w, sg_b, w_b_out, cmp_pe_k, cmp_w1_k, cmp_w2_k, cmp_pe_v, cmp_w1_v,
           cmp_w2_v, w_c_out, w_o, w_up, ffn_conv_w, ffn_conv_b, w_down):
    p = dict(norm_mix_pre=norm_mix_pre, norm_mix_post=norm_mix_post, norm_ffn_pre=norm_ffn_pre,
             norm_ffn_post=norm_ffn_post, w_in=w_in, conv_a_w=conv_a_w, conv_a_b=conv_a_b, ln_a_g=ln_a_g,
             ln_a_b=ln_a_b, w_a_out=w_a_out, ln_b_g=ln_b_g, ln_b_b=ln_b_b, sg_w=sg_w, sg_b=sg_b, w_b_out=w_b_out,
             cmp_pe_k=cmp_pe_k, cmp_w1_k=cmp_w1_k, cmp_w2_k=cmp_w2_k, cmp_pe_v=cmp_pe_v, cmp_w1_v=cmp_w1_v,
             cmp_w2_v=cmp_w2_v, w_c_out=w_c_out, w_o=w_o, w_up=w_up, ffn_conv_w=ffn_conv_w,
             ffn_conv_b=ffn_conv_b, w_down=w_down)
    for l in range(w_in.shape[0]):
        x = _mixer_layer(x, p, l)
        x = _ffn_layer(x, p, l)
    return x
```

```python
import functools

import numpy as np
import jax
import jax.numpy as jnp
from jax import lax
from jax.experimental import pallas as pl
from jax.experimental.pallas import tpu as pltpu

F32 = jnp.float32
MXU_DTYPE = jnp.bfloat16

EPS = 1e-6
NEG = -1e30
FORCE = 1e4

CONV_WIDTH = 1024
DW_CONV_SIZE = 31
CONV_HALO = 32
SG_WIDTH = 1024
SG_CHUNK = 128
SG_GROUPS = 8
N_HEADS = 16
N_KV_GROUPS = 4
HEADS_PER_GROUP = 4
HEAD_DIM = 64
CMP_BLOCK = 32
CMP_STRIDE = 16
CMP_HIDDEN = 256
SLC_BLOCK = 64
SLC_TOP_N = 16
WINDOW = 512
D_FF = 5632
FFN_HALO = 16
GATE_PAD = 128
GATE_ROWS = 16

VMEM_LIMIT = 56 * 1024 * 1024


def _cparams(sem):
    return pltpu.CompilerParams(dimension_semantics=sem, vmem_limit_bytes=VMEM_LIMIT)


def _sigmoid(x):
    return jax.nn.sigmoid(x)


def _rmsnorm_kernel(x_ref, w_ref, o_ref):
    x = x_ref[...]
    ms = jnp.mean(x * x, axis=-1, keepdims=True)
    o_ref[...] = (x * lax.rsqrt(ms + EPS) * w_ref[...]).astype(o_ref.dtype)


def _rmsnorm_cast(x2d, w):
    m, d = x2d.shape
    tm = min(512, m)
    return pl.pallas_call(
        _rmsnorm_kernel,
        grid=(m // tm,),
        in_specs=[pl.BlockSpec((tm, d), lambda i: (i, 0)),
                  pl.BlockSpec((1, d), lambda i: (0, 0))],
        out_specs=pl.BlockSpec((tm, d), lambda i: (i, 0)),
        out_shape=jax.ShapeDtypeStruct((m, d), MXU_DTYPE),
        compiler_params=_cparams(("parallel",)),
        name="rmsnorm_cast",
    )(x2d, w.reshape(1, d))


def _mm_kernel(a_ref, w_ref, o_ref):
    o_ref[...] = jnp.dot(a_ref[...], w_ref[...], preferred_element_type=F32).astype(o_ref.dtype)


def _pick(n, cands):
    for c in cands:
        if n % c == 0:
            return c
    raise ValueError(f"no tile for {n}")


def _matmul(a, w, out_dtype=F32, name="matmul"):
    m, k = a.shape
    n = w.shape[1]
    tm = min(1024, m)
    tn = _pick(n, (512, 384, 256, 128))
    return pl.pallas_call(
        _mm_kernel,
        grid=(m // tm, n // tn),
        in_specs=[pl.BlockSpec((tm, k), lambda i, j: (i, 0)),
                  pl.BlockSpec((k, tn), lambda i, j: (0, j))],
        out_specs=pl.BlockSpec((tm, tn), lambda i, j: (i, j)),
        out_shape=jax.ShapeDtypeStruct((m, n), out_dtype),
        compiler_params=_cparams(("parallel", "arbitrary")),
        name=name,
    )(a, w)


def _conformer_kernel(cur_ref, halo_ref, cw_ref, cb_ref, lg_ref, lb_ref, wout_ref, o_ref, hbuf, cbuf, *, ts):
    c = CONV_WIDTH
    i = pl.program_id(1)
    cur = cur_ref[...]
    hbuf[pl.ds(CONV_HALO, ts), :] = cur[:, :c] * _sigmoid(cur[:, c:])
    hal = halo_ref[...]
    hh = hal[:, :c] * _sigmoid(hal[:, c:])
    hbuf[pl.ds(0, CONV_HALO), :] = jnp.where(i == 0, 0.0, hh)

    def col_body(cc, carry):
        off = pl.multiple_of(cc * 128, 128)
        acc = jnp.broadcast_to(cb_ref[:, pl.ds(off, 128)], (ts, 128))
        for k in range(DW_CONV_SIZE):
            acc = acc + cw_ref[pl.ds(k, 1), pl.ds(off, 128)] * hbuf[pl.ds(CONV_HALO - (DW_CONV_SIZE - 1) + k, ts),
                                                                   pl.ds(off, 128)]
        cbuf[:, pl.ds(off, 128)] = acc
        return carry

    lax.fori_loop(0, c // 128, col_body, 0)
    y = cbuf[...]
    mean = jnp.mean(y, axis=-1, keepdims=True)
    yc = y - mean
    var = jnp.mean(yc * yc, axis=-1, keepdims=True)
    z = yc * lax.rsqrt(var + EPS) * lg_ref[...] + lb_ref[...]
    z = z * _sigmoid(z)
    o_ref[...] = jnp.dot(z.astype(MXU_DTYPE), wout_ref[...], preferred_element_type=F32)


def _conformer(a_in, conv_w, conv_b, ln_g, ln_b, w_out):
    bsz, seq, _ = a_in.shape
    c = CONV_WIDTH
    d = w_out.shape[1]
    ts = min(256, seq)
    per = ts // CONV_HALO
    kern = functools.partial(_conformer_kernel, ts=ts)
    return pl.pallas_call(
        kern,
        grid=(bsz, seq // ts),
        in_specs=[pl.BlockSpec((None, ts, 2 * c), lambda b, i: (b, i, 0)),
                  pl.BlockSpec((None, CONV_HALO, 2 * c), lambda b, i: (b, jnp.maximum(i * per - 1, 0), 0)),
                  pl.BlockSpec((DW_CONV_SIZE, c), lambda b, i: (0, 0)),
                  pl.BlockSpec((1, c), lambda b, i: (0, 0)),
                  pl.BlockSpec((1, c), lambda b, i: (0, 0)),
                  pl.BlockSpec((1, c), lambda b, i: (0, 0)),
                  pl.BlockSpec((c, d), lambda b, i: (0, 0))],
        out_specs=pl.BlockSpec((None, ts, d), lambda b, i: (b, i, 0)),
        out_shape=jax.ShapeDtypeStruct((bsz, seq, d), F32),
        scratch_shapes=[pltpu.VMEM((CONV_HALO + ts, c), F32), pltpu.VMEM((ts, c), F32)],
        compiler_params=_cparams(("parallel", "parallel")),
        name="conformer_conv",
    )(a_in, a_in, conv_w, conv_b.reshape(1, c), ln_g.reshape(1, c), ln_b.reshape(1, c), w_out.astype(MXU_DTYPE))


def _gmlp_kernel(x_ref, lg_ref, lb_ref, sgw_ref, sgbt_ref, wout_ref, o_ref, fbuf, *, ts):
    c = SG_WIDTH
    gd = c // SG_GROUPS
    x = x_ref[...]
    z = 0.5 * x * (1.0 + lax.erf(x * np.float32(np.sqrt(0.5))))
    u = z[:, :c]
    v = z[:, c:]
    mean = jnp.mean(v, axis=-1, keepdims=True)
    vc = v - mean
    var = jnp.mean(vc * vc, axis=-1, keepdims=True)
    v = vc * lax.rsqrt(var + EPS) * lg_ref[...] + lb_ref[...]
    row = lax.broadcasted_iota(jnp.int32, (SG_CHUNK, SG_CHUNK), 0)
    col = lax.broadcasted_iota(jnp.int32, (SG_CHUNK, SG_CHUNK), 1)
    causal = col <= row
    for g in range(SG_GROUPS):
        w = jnp.where(causal, sgw_ref[g], 0.0).astype(MXU_DTYPE)
        bias = sgbt_ref[:, g:g + 1]
        for ch in range(ts // SG_CHUNK):
            r0 = ch * SG_CHUNK
            vg = v[r0:r0 + SG_CHUNK, g * gd:(g + 1) * gd].astype(MXU_DTYPE)
            f = jnp.dot(w, vg, preferred_element_type=F32) + bias
            fbuf[r0:r0 + SG_CHUNK, g * gd:(g + 1) * gd] = u[r0:r0 + SG_CHUNK, g * gd:(g + 1) * gd] * f
    o_ref[...] = jnp.dot(fbuf[...].astype(MXU_DTYPE), wout_ref[...], preferred_element_type=F32)


def _gmlp(b_in, ln_g, ln_b, sg_w, sg_b, w_out):
    m, _ = b_in.shape
    c = SG_WIDTH
    d = w_out.shape[1]
    ts = 256
    kern = functools.partial(_gmlp_kernel, ts=ts)
    return pl.pallas_call(
        kern,
        grid=(m // ts,),
        in_specs=[pl.BlockSpec((ts, 2 * c), lambda i: (i, 0)),
                  pl.BlockSpec((1, c), lambda i: (0, 0)),
                  pl.BlockSpec((1, c), lambda i: (0, 0)),
                  pl.BlockSpec((SG_GROUPS, SG_CHUNK, SG_CHUNK), lambda i: (0, 0, 0)),
                  pl.BlockSpec((SG_CHUNK, SG_GROUPS), lambda i: (0, 0)),
                  pl.BlockSpec((c, d), lambda i: (0, 0))],
        out_specs=pl.BlockSpec((ts, d), lambda i: (i, 0)),
        out_shape=jax.ShapeDtypeStruct((m, d), F32),
        scratch_shapes=[pltpu.VMEM((ts, c), F32)],
        compiler_params=_cparams(("parallel",)),
        name="gmlp_gating",
    )(b_in, ln_g.reshape(1, c), ln_b.reshape(1, c), sg_w, sg_b.T, w_out.astype(MXU_DTYPE))


def _compress_kernel(p_ref, pelo_ref, pehi_ref, w1a_ref, w1b_ref, w2_ref, o_ref):
    p = p_ref[...]
    a = jnp.dot((p + pelo_ref[...]).astype(MXU_DTYPE), w1a_ref[...], preferred_element_type=F32)
    b = jnp.dot((p + pehi_ref[...]).astype(MXU_DTYPE), w1b_ref[...], preferred_element_type=F32)
    n = p.shape[0]
    b_next = pltpu.roll(b, n - 1, 0)
    hid = a + b_next
    hid = hid * _sigmoid(hid)
    o_ref[...] = jnp.dot(hid.astype(MXU_DTYPE), w2_ref[...], preferred_element_type=F32)


def _compress(pieces, pe, w1, w2):
    bsz, g, n_piece, width = pieces.shape
    half = CMP_BLOCK // 2
    pe_lo = pe[:half].reshape(1, width)
    pe_hi = pe[half:].reshape(1, width)
    w1a = w1[:half].reshape(width, CMP_HIDDEN).astype(MXU_DTYPE)
    w1b = w1[half:].reshape(width, CMP_HIDDEN).astype(MXU_DTYPE)
    return pl.pallas_call(
        _compress_kernel,
        grid=(bsz, g),
        in_specs=[pl.BlockSpec((None, None, n_piece, width), lambda b, gg: (b, gg, 0, 0)),
                  pl.BlockSpec((1, width), lambda b, gg: (0, 0)),
                  pl.BlockSpec((1, width), lambda b, gg: (0, 0)),
                  pl.BlockSpec((width, CMP_HIDDEN), lambda b, gg: (0, 0)),
                  pl.BlockSpec((width, CMP_HIDDEN), lambda b, gg: (0, 0)),
                  pl.BlockSpec((CMP_HIDDEN, HEAD_DIM), lambda b, gg: (0, 0))],
        out_specs=pl.BlockSpec((None, None, n_piece, HEAD_DIM), lambda b, gg: (b, gg, 0, 0)),
        out_shape=jax.ShapeDtypeStruct((bsz, g, n_piece, HEAD_DIM), F32),
        compiler_params=_cparams(("parallel", "parallel")),
        name="nsa_compress",
    )(pieces, pe_lo, pe_hi, w1a, w1b, w2.astype(MXU_DTYPE))


ROW_BLK = 32
M_INIT = -3e38


def _masked_softmax(s, mask):
    s = jnp.where(mask, s, NEG)
    m = jnp.max(s, axis=-1, keepdims=True)
    e = jnp.where(mask, jnp.exp(s - m), 0.0)
    return e / jnp.maximum(jnp.sum(e, axis=-1, keepdims=True), 1e-30)


def _nt_dot(a, b, precision=None):
    return lax.dot_general(a, b, (((1,), (1,)), ((), ())), preferred_element_type=F32, precision=precision)


def _nsa_kernel(*refs, tq, ck, nt):
    n_in = 12
    shared, scratch = refs[:n_in], refs[n_in:]
    tiles = [_nsa_tile(u, *shared, *[s.at[u] for s in scratch], tq=tq, ck=ck, nt=nt) for u in range(nt)]
    loops = [next(t) for t in tiles]

    def pair_body(i, carry):
        for _, body in loops:
            body(i)
        return carry

    lax.fori_loop(0, loops[0][0], pair_body, 0)
    for t in tiles:
        next(t, None)


def _nsa_tile(u, q_ref, kc_ref, vct_ref, ks_ref, vst_ref, kw_ref, vwt_ref, gate_ref, mapt_ref, bt_ref, bandt_ref,
              o_ref, sc_scr, pc_scr, sw_scr, pw_scr, sa_scr, sb_scr, pa_scr, pb_scr, m_scr, ala_scr, alb_scr, acc_scr,
              mw_scr, al_scr, accw_scr, rank_scr, *, tq, ck, nt):
    hpg = HEADS_PER_GROUP
    dh = HEAD_DIM
    aug = 2 * dh
    cols = hpg * tq
    n_blk = mapt_ref.shape[0]
    q0 = (pl.program_id(2) * nt + u) * tq
    tile_rows = pl.ds(u * tq, tq)
    qp = (q_ref[:, tile_rows, :] * np.float32(dh ** -0.5).astype(q_ref.dtype)).reshape(cols, aug)
    eye4 = ((lax.broadcasted_iota(jnp.int32, (cols, tq), 0) & (tq - 1))
            == lax.broadcasted_iota(jnp.int32, (cols, tq), 1)).astype(MXU_DTYPE)
    t_lane = q0 + lax.broadcasted_iota(jnp.int32, (1, tq), 1)

    def online(s_ref, p_ref, kr, m_ref, acc_ref, v_t):
        for h in range(hpg):
            cb = pl.ds(h * tq, tq)
            s = s_ref[kr, cb]
            m_old = m_ref[:, cb]
            m_new = jnp.maximum(m_old, jnp.max(s, axis=0, keepdims=True))
            p_ref[kr, cb] = jnp.exp(s - m_new).astype(p_ref.dtype)
            al_scr[:, cb] = jnp.exp(m_old - m_new)
            m_ref[:, cb] = m_new
        acc_ref[...] = al_scr[...] * acc_ref[...] + jnp.dot(v_t, p_ref[kr, :], preferred_element_type=F32)

    def finish(acc):
        return acc[:dh] / jnp.maximum(acc[dh:dh + 1], 1e-30)

    n_cmp = kc_ref.shape[0]
    sc_scr[...] = _nt_dot(kc_ref[...].astype(MXU_DTYPE), qp)
    wlen = WINDOW + tq
    w0 = pl.multiple_of(q0, tq)
    pad_flag = (lax.broadcasted_iota(jnp.int32, (1, aug), 1) == dh).astype(MXU_DTYPE)
    lhs_win = jnp.concatenate([qp + pad_flag, eye4], axis=1)
    rhs_win = jnp.concatenate([kw_ref[pl.ds(w0, wlen), :], bandt_ref[...]], axis=1)
    sw_scr[...] = _nt_dot(rhs_win, lhs_win)

    cmask = (lax.broadcasted_iota(jnp.int32, (n_cmp, 1), 0) * CMP_STRIDE + (CMP_BLOCK - 1)) <= t_lane
    psum = jnp.zeros((n_cmp, tq), F32)
    for h in range(hpg):
        cb = pl.ds(h * tq, tq)
        s = jnp.where(cmask, sc_scr[:, cb], NEG)
        e = jnp.where(cmask, jnp.exp(s - jnp.max(s, axis=0, keepdims=True)), 0.0)
        p = e / jnp.maximum(jnp.sum(e, axis=0, keepdims=True), 1e-30)
        pc_scr[:, cb] = p.astype(pc_scr.dtype)
        psum = psum + p
    o_cmp = jnp.dot(vct_ref[...].astype(MXU_DTYPE), pc_scr[...], preferred_element_type=F32)

    mw_scr[...] = jnp.full(mw_scr.shape, M_INIT, F32)
    accw_scr[...] = jnp.zeros(accw_scr.shape, F32)
    for k_lo in range(0, wlen, ck):
        nk = min(ck, wlen - k_lo)
        online(sw_scr, pw_scr, pl.ds(k_lo, nk), mw_scr, accw_scr,
               vwt_ref[:, pl.ds(pl.multiple_of(w0 + k_lo, tq), nk)])
    o_win = finish(accw_scr[...])

    imp_t = jnp.dot(mapt_ref[...], psum, preferred_element_type=F32, precision=lax.Precision.HIGHEST)
    j_t = lax.broadcasted_iota(jnp.int32, (n_blk, tq), 0)
    cur_t = (q0 + lax.broadcasted_iota(jnp.int32, (n_blk, tq), 1)) >> 6
    forced = (j_t == 0) | (j_t == cur_t) | (j_t == cur_t - 1)
    score_t = jnp.where(j_t <= cur_t, jnp.where(forced, FORCE, imp_t), NEG)
    rank_scr[...] = score_t
    sel_rows = []
    for v in range(n_blk // 8):
        sj = score_t[8 * v:8 * v + 8]
        jv = j_t[8 * v:8 * v + 8]
        rank = jnp.zeros((8, tq), jnp.int32)
        for i in range(n_blk):
            ci = rank_scr[pl.ds(i, 1), :]
            if i < 8 * v:
                ahead = ci >= sj
            elif i > 8 * v + 7:
                ahead = ci > sj
            else:
                ahead = (ci > sj) | ((ci == sj) & (jv > i))
            rank = rank + ahead.astype(jnp.int32)
        sel_rows.append(jnp.where(rank < SLC_TOP_N, 0.0, NEG))
    selb_t = jnp.concatenate([jnp.zeros((dh, tq), F32)] + sel_rows, axis=0).astype(MXU_DTYPE)
    selb = _nt_dot(eye4[:tq], selb_t).astype(MXU_DTYPE)
    lhs_slc = jnp.concatenate([(qp.reshape(hpg, tq, aug) + selb[None]).reshape(cols, aug), eye4], axis=1)

    n_full = q0 // ck

    def scores(c, s_ref):
        k0 = pl.multiple_of(jnp.minimum(c, n_full) * ck, ck)
        start = pl.multiple_of(jnp.clip(q0 - c * ck, -tq, ck) + tq, tq)
        rhs = jnp.concatenate([ks_ref[pl.ds(k0, ck), :], bt_ref[:, pl.ds(start, tq)]], axis=1)
        s_ref[...] = _nt_dot(rhs, lhs_slc)

    def softmax(s_ref, p_ref, a_ref):
        for h in range(hpg):
            cb = pl.ds(h * tq, tq)
            s = s_ref[:, cb]
            m_old = m_scr[:, cb]
            m_new = jnp.maximum(m_old, jnp.max(s, axis=0, keepdims=True))
            p_ref[:, cb] = jnp.exp(s - m_new).astype(p_ref.dtype)
            a_ref[:, cb] = jnp.exp(m_old - m_new)
            m_scr[:, cb] = m_new

    def accumulate(c, p_ref, a_ref):
        k0 = pl.multiple_of(jnp.clip(c, 0, n_full) * ck, ck)
        acc_scr[...] = a_ref[...] * acc_scr[...] + jnp.dot(vst_ref[:, pl.ds(k0, ck)], p_ref[...],
                                                           preferred_element_type=F32)

    m_scr[...] = jnp.full(m_scr.shape, M_INIT, F32)
    acc_scr[...] = jnp.zeros(acc_scr.shape, F32)
    pb_scr[...] = jnp.zeros(pb_scr.shape, pb_scr.dtype)
    alb_scr[...] = jnp.ones(alb_scr.shape, F32)
    scores(0, sa_scr)

    def pair_body(i):
        k = 2 * i
        scores(k + 1, sb_scr)
        softmax(sa_scr, pa_scr, ala_scr)
        accumulate(k - 1, pb_scr, alb_scr)
        scores(k + 2, sa_scr)
        softmax(sb_scr, pb_scr, alb_scr)
        accumulate(k, pa_scr, ala_scr)

    n_pairs = (n_full + 2) // 2
    yield n_pairs, pair_body
    accumulate(2 * n_pairs - 1, pb_scr, alb_scr)
    o_slc = finish(acc_scr[...])

    gl = _sigmoid(gate_ref[:, tile_rows])
    outs = []
    for h in range(hpg):
        cb = slice(h * tq, (h + 1) * tq)
        outs.append(gl[3 * h:3 * h + 1] * o_cmp[:, cb] + gl[3 * h + 1:3 * h + 2] * o_slc[:, cb]
                    + gl[3 * h + 2:3 * h + 3] * o_win[:, cb])
    o_t = jnp.concatenate(outs, axis=0).astype(MXU_DTYPE)
    o_ref[tile_rows, :] = _nt_dot(eye4[:tq], o_t).astype(o_ref.dtype)


def _slc_map_t(n_piece, n_slc, n_blk):
    r = CMP_BLOCK // CMP_STRIDE
    a = SLC_BLOCK // CMP_STRIDE
    m = np.zeros((n_blk, n_piece), np.float32)
    for n in range(n_piece - r + 1):
        for i in range(r):
            m[(n + i) // a, n] += 1.0
    return m


def _nsa_attention(q, kvr, kc, vc, gate_logits):
    bsz, seq, _ = q.shape
    g, hpg, dh = N_KV_GROUPS, HEADS_PER_GROUP, HEAD_DIM
    aug = 2 * dh
    n_piece = kc.shape[2]
    n_slc = seq // SLC_BLOCK
    n_blk = dh
    assert n_slc <= n_blk
    tq = 128
    ck = 256
    wlen = WINDOW + tq
    cols = hpg * tq

    q_r = jnp.pad(q.reshape(bsz, seq, g, hpg, dh), ((0, 0),) * 4 + ((0, dh),)).transpose(0, 2, 3, 1, 4)
    q_r = q_r.astype(MXU_DTYPE)
    blk = np.arange(seq) // SLC_BLOCK
    onehot = (blk[:, None] == np.arange(dh)[None, :]).astype(np.float32)
    ones_row = np.zeros((dh, seq), np.float32)
    ones_row[0] = 1.0
    bc = lambda c: jnp.broadcast_to(jnp.asarray(c, MXU_DTYPE), (bsz, g) + c.shape)
    tr = lambda a: a.astype(MXU_DTYPE).transpose(0, 1, 3, 2)
    kc_p = jnp.pad(kc, ((0, 0), (0, 0), (0, 0), (0, dh)))
    vc_t = tr(vc)
    ks = jnp.concatenate([kvr[2].astype(MXU_DTYPE), bc(onehot)], axis=-1)
    vs_t = jnp.concatenate([tr(kvr[3]), bc(ones_row)], axis=2)
    pad_k = np.zeros((WINDOW, aug), np.float32)
    pad_k[:, dh] = NEG
    kw = jnp.concatenate([kvr[4].astype(MXU_DTYPE), jnp.zeros((bsz, g, seq, dh), MXU_DTYPE)], axis=-1)
    kw = jnp.concatenate([bc(pad_k), kw], axis=2)
    vw_t = jnp.concatenate([tr(kvr[5]), bc(ones_row)], axis=2)
    vw_t = jnp.concatenate([jnp.zeros((bsz, g, aug, WINDOW), MXU_DTYPE), vw_t], axis=3)
    gate_t = gate_logits.reshape(bsz, seq, g, GATE_PAD)[..., :GATE_ROWS].transpose(0, 2, 3, 1)

    kk = np.arange(ck)[:, None]
    bt = np.where(kk <= np.arange(-tq, ck + tq)[None, :], 0.0, NEG).astype(np.float32)
    kw_i = np.arange(wlen)[:, None]
    tt = np.arange(tq)[None, :]
    bandt = np.where((kw_i <= tt + WINDOW) & (kw_i > tt), 0.0, NEG).astype(np.float32)

    nt = ck // tq
    kern = functools.partial(_nsa_kernel, tq=tq, ck=ck, nt=nt)
    per_group = lambda r, c: pl.BlockSpec((None, None, r, c), lambda b, gg, i: (b, gg, 0, 0))
    const = lambda shape: pl.BlockSpec(shape, lambda b, gg, i: (0, 0))
    per_tile = lambda shape, dtype: pltpu.VMEM((nt,) + shape, dtype)
    return pl.pallas_call(
        kern,
        grid=(bsz, g, seq // (nt * tq)),
        in_specs=[pl.BlockSpec((None, None, hpg, nt * tq, aug), lambda b, gg, i: (b, gg, 0, i, 0)),
                  per_group(n_piece, aug), per_group(dh, n_piece),
                  per_group(seq, aug), per_group(aug, seq),
                  per_group(WINDOW + seq, aug), per_group(aug, WINDOW + seq),
                  pl.BlockSpec((None, None, GATE_ROWS, nt * tq), lambda b, gg, i: (b, gg, 0, i)),
                  const((n_blk, n_piece)), const((ck, ck + 2 * tq)), const((wlen, tq))],
        out_specs=pl.BlockSpec((None, nt * tq, hpg * dh), lambda b, gg, i: (b, i, gg)),
        out_shape=jax.ShapeDtypeStruct((bsz, seq, g * hpg * dh), MXU_DTYPE),
        scratch_shapes=[per_tile((n_piece, cols), F32),
                        per_tile((n_piece, cols), MXU_DTYPE),
                        per_tile((wlen, cols), F32),
                        per_tile((wlen, cols), MXU_DTYPE),
                        per_tile((ck, cols), F32),
                        per_tile((ck, cols), F32),
                        per_tile((ck, cols), MXU_DTYPE),
                        per_tile((ck, cols), MXU_DTYPE),
                        per_tile((1, cols), F32),
                        per_tile((1, cols), F32),
                        per_tile((1, cols), F32),
                        per_tile((aug, cols), F32),
                        per_tile((1, cols), F32),
                        per_tile((1, cols), F32),
                        per_tile((aug, cols), F32),
                        per_tile((n_blk, tq), F32)],
        compiler_params=_cparams(("parallel", "parallel", "arbitrary")),
        name="nsa_attention",
    )(q_r, kc_p, vc_t, ks, vs_t, kw, vw_t, gate_t, jnp.asarray(_slc_map_t(n_piece, n_slc, n_blk)),
      jnp.asarray(bt, MXU_DTYPE), jnp.asarray(bandt, MXU_DTYPE))


def _merge_kernel(ya_ref, yb_ref, yc_ref, gl_ref, x_ref, wo_ref, nw_ref, o_ref):
    d = ya_ref.shape[1]
    gl = gl_ref[...]
    mixed = (_sigmoid(gl[:, :d]) * ya_ref[...] + _sigmoid(gl[:, d:2 * d]) * yb_ref[...]
             + _sigmoid(gl[:, 2 * d:]) * yc_ref[...])
    r = jnp.dot(mixed.astype(MXU_DTYPE), wo_ref[...], preferred_element_type=F32)
    ms = jnp.mean(r * r, axis=-1, keepdims=True)
    o_ref[...] = x_ref[...] + r * lax.rsqrt(ms + EPS) * nw_ref[...]


def _merge(ya, yb, yc, merge_logits, x2d, w_o, norm_w):
    m, d = x2d.shape
    tm = 256
    row = lambda i: (i, 0)
    return pl.pallas_call(
        _merge_kernel,
        grid=(m // tm,),
        in_specs=[pl.BlockSpec((tm, d), row), pl.BlockSpec((tm, d), row), pl.BlockSpec((tm, d), row),
                  pl.BlockSpec((tm, 3 * d), row), pl.BlockSpec((tm, d), row),
                  pl.BlockSpec((d, d), lambda i: (0, 0), pipeline_mode=pl.Buffered(1)),
                  pl.BlockSpec((1, d), lambda i: (0, 0))],
        out_specs=pl.BlockSpec((tm, d), row),
        out_shape=jax.ShapeDtypeStruct((m, d), F32),
        compiler_params=_cparams(("parallel",)),
        name="merge_out_proj",
    )(ya, yb, yc, merge_logits, x2d, w_o.astype(MXU_DTYPE), norm_w.reshape(1, d))


def _ffn_up_kernel(h_ref, halo_ref, wg_ref, wv_ref, cwg_ref, cwv_ref, cbg_ref, cbv_ref, o_ref,
                   hbuf, ugbuf, uvbuf, *, tm, seq):
    i = pl.program_id(0)
    j = pl.program_id(1)

    @pl.when(j == 0)
    def _():
        hbuf[pl.ds(FFN_HALO, tm), :] = h_ref[...]
        first = (i * tm) % seq == 0
        hbuf[pl.ds(0, FFN_HALO), :] = jnp.where(first, jnp.zeros_like(halo_ref[...]), halo_ref[...])

    lhs = hbuf[...]
    ugbuf[...] = jnp.dot(lhs, wg_ref[...], preferred_element_type=F32)
    uvbuf[...] = jnp.dot(lhs, wv_ref[...], preferred_element_type=F32)

    def conv(buf, cw_ref, cb_ref):
        return (cb_ref[...] + cw_ref[pl.ds(2, 1), :] * buf[pl.ds(FFN_HALO, tm), :]
                + cw_ref[pl.ds(1, 1), :] * buf[pl.ds(FFN_HALO - 1, tm), :]
                + cw_ref[pl.ds(0, 1), :] * buf[pl.ds(FFN_HALO - 2, tm), :])

    cg = conv(ugbuf, cwg_ref, cbg_ref)
    cv = conv(uvbuf, cwv_ref, cbv_ref)
    o_ref[...] = (cg * _sigmoid(cg) * cv).astype(o_ref.dtype)


def _ffn_up(h, w_up, conv_w, conv_b, seq):
    m, d = h.shape
    dff = w_up.shape[1] // 2
    tm = min(1024, seq)
    tn = _pick(dff, (512, 256, 128))
    nj = dff // tn
    per = tm // FFN_HALO
    kern = functools.partial(_ffn_up_kernel, tm=tm, seq=seq)
    cb = conv_b.reshape(1, 2 * dff)
    return pl.pallas_call(
        kern,
        grid=(m // tm, nj),
        in_specs=[pl.BlockSpec((tm, d), lambda i, j: (i, 0)),
                  pl.BlockSpec((FFN_HALO, d), lambda i, j: (jnp.maximum(i * per - 1, 0), 0)),
                  pl.BlockSpec((d, tn), lambda i, j: (0, j)),
                  pl.BlockSpec((d, tn), lambda i, j: (0, j + nj)),
                  pl.BlockSpec((3, tn), lambda i, j: (0, j)),
                  pl.BlockSpec((3, tn), lambda i, j: (0, j + nj)),
                  pl.BlockSpec((1, tn), lambda i, j: (0, j)),
                  pl.BlockSpec((1, tn), lambda i, j: (0, j + nj))],
        out_specs=pl.BlockSpec((tm, tn), lambda i, j: (i, j)),
        out_shape=jax.ShapeDtypeStruct((m, dff), MXU_DTYPE),
        scratch_shapes=[pltpu.VMEM((FFN_HALO + tm, d), MXU_DTYPE),
                        pltpu.VMEM((FFN_HALO + tm, tn), F32),
                        pltpu.VMEM((FFN_HALO + tm, tn), F32)],
        compiler_params=_cparams(("parallel", "arbitrary")),
        name="ffn_up_conv_gate",
    )(h, h, w_up, w_up, conv_w, conv_w, cb, cb)


def _ffn_down_kernel(a_ref, w_ref, x_ref, nw_ref, o_ref):
    r = jnp.dot(a_ref[...], w_ref[...], preferred_element_type=F32)
    ms = jnp.mean(r * r, axis=-1, keepdims=True)
    o_ref[...] = x_ref[...] + r * lax.rsqrt(ms + EPS) * nw_ref[...]


def _ffn_down(act, w_down, x2d, norm_w):
    m, dff = act.shape
    d = w_down.shape[1]
    tm = 256
    return pl.pallas_call(
        _ffn_down_kernel,
        grid=(m // tm,),
        in_specs=[pl.BlockSpec((tm, dff), lambda i: (i, 0)),
                  pl.BlockSpec((dff, d), lambda i: (0, 0), pipeline_mode=pl.Buffered(1)),
                  pl.BlockSpec((tm, d), lambda i: (i, 0)),
                  pl.BlockSpec((1, d), lambda i: (0, 0))],
        out_specs=pl.BlockSpec((tm, d), lambda i: (i, 0)),
        out_shape=jax.ShapeDtypeStruct((m, d), F32),
        compiler_params=_cparams(("parallel",)),
        name="ffn_down_norm_res",
    )(act, w_down, x2d, norm_w.reshape(1, d))


def _gate_weight(w_gate):
    d = w_gate.shape[0]
    per = HEADS_PER_GROUP * 3
    w = w_gate.reshape(d, N_KV_GROUPS, per)
    w = jnp.pad(w, ((0, 0), (0, 0), (0, GATE_PAD - per)))
    return w.reshape(d, N_KV_GROUPS * GATE_PAD)


def _mixer_layer(x, p, l):
    bsz, seq, d = x.shape
    m = bsz * seq
    x2d = x.reshape(m, d)
    g, hpg, dh = N_KV_GROUPS, HEADS_PER_GROUP, HEAD_DIM
    a_in_w = 2 * CONV_WIDTH
    b_in_w = 2 * SG_WIDTH
    q_w = N_HEADS * dh
    kv_w = 6 * g * dh
    gate_w = 3 * N_HEADS
    o0 = 0
    o1 = o0 + a_in_w
    o2 = o1 + b_in_w
    o3 = o2 + q_w
    o4 = o3 + kv_w
    o5 = o4 + gate_w
    w_in = p["w_in"][l]

    h = _rmsnorm_cast(x2d, p["norm_mix_pre"][l])
    wc = lambda w: w.astype(MXU_DTYPE)
    a_in = _matmul(h, wc(w_in[:, o0:o1]), name="proj_a")
    b_in = _matmul(h, wc(w_in[:, o1:o2]), name="proj_b")
    q = _matmul(h, wc(w_in[:, o2:o3]), name="proj_q")
    kv = _matmul(h, wc(w_in[:, o3:o4]), name="proj_kv")
    gate_logits = _matmul(h, wc(_gate_weight(w_in[:, o4:o5])), name="proj_gate")
    merge_logits = _matmul(h, wc(w_in[:, o5:]), name="proj_merge")

    y_a = _conformer(a_in.reshape(bsz, seq, a_in_w), p["conv_a_w"][l], p["conv_a_b"][l],
                     p["ln_a_g"][l], p["ln_a_b"][l], p["w_a_out"][l]).reshape(m, d)
    y_b = _gmlp(b_in, p["ln_b_g"][l], p["ln_b_b"][l], p["sg_w"][l], p["sg_b"][l], p["w_b_out"][l])

    kvr = kv.reshape(bsz, seq, 6, g, dh).transpose(2, 0, 3, 1, 4)
    n_piece = seq // CMP_STRIDE
    kc = _compress(kvr[0].reshape(bsz, g, n_piece, CMP_STRIDE * dh), p["cmp_pe_k"][l], p["cmp_w1_k"][l],
                   p["cmp_w2_k"][l])
    vc = _compress(kvr[1].reshape(bsz, g, n_piece, CMP_STRIDE * dh), p["cmp_pe_v"][l], p["cmp_w1_v"][l],
                   p["cmp_w2_v"][l])
    o = _nsa_attention(q.reshape(bsz, seq, N_HEADS * dh), kvr, kc, vc, gate_logits.reshape(bsz, seq, g * GATE_PAD))
    y_c = _matmul(o.reshape(m, N_HEADS * dh), wc(p["w_c_out"][l]), name="proj_c_out")

    return _merge(y_a, y_b, y_c, merge_logits, x2d, p["w_o"][l], p["norm_mix_post"][l]).reshape(bsz, seq, d)


def _ffn_layer(x, p, l):
    bsz, seq, d = x.shape
    m = bsz * seq
    x2d = x.reshape(m, d)
    h = _rmsnorm_cast(x2d, p["norm_ffn_pre"][l])
    act = _ffn_up(h, p["w_up"][l].astype(MXU_DTYPE), p["ffn_conv_w"][l], p["ffn_conv_b"][l], seq)
    return _ffn_down(act, p["w_down"][l].astype(MXU_DTYPE), x2d, p["norm_ffn_post"][l]).reshape(bsz, seq, d)


def kernel(x, norm_mix_pre, norm_mix_post, norm_ffn_pre, norm_ffn_post, w_in, conv_a_w, conv_a_b, ln_a_g, ln_a_b,
           w_a_out, ln_b_g, ln_b_b, sg_w, sg_b, w_b_out, cmp_pe_k, cmp_w1_k, cmp_w2_k, cmp_pe_v, cmp_w1_v,
           cmp_w2_v, w_c_out, w_o, w_up, ffn_conv_w, ffn_conv_b, w_down):
    p = dict(norm_mix_pre=norm_mix_pre, norm_mix_post=norm_mix_post, norm_ffn_pre=norm_ffn_pre,
             norm_ffn_post=norm_ffn_post, w_in=w_in, conv_a_w=conv_a_w, conv_a_b=conv_a_b, ln_a_g=ln_a_g,
             ln_a_b=ln_a_b, w_a_out=w_a_out, ln_b_g=ln_b_g, ln_b_b=ln_b_b, sg_w=sg_w, sg_b=sg_b, w_b_out=w_b_out,
             cmp_pe_k=cmp_pe_k, cmp_w1_k=cmp_w1_k, cmp_w2_k=cmp_w2_k, cmp_pe_v=cmp_pe_v, cmp_w1_v=cmp_w1_v,
             cmp_w2_v=cmp_w2_v, w_c_out=w_c_out, w_o=w_o, w_up=w_up, ffn_conv_w=ffn_conv_w,
             ffn_conv_b=ffn_conv_b, w_down=w_down)
    for l in range(w_in.shape[0]):
        x = _mixer_layer(x, p, l)
        x = _ffn_layer(x, p, l)
    return x
```

```python
import functools

import numpy as np
import jax
import jax.numpy as jnp
from jax import lax
from jax.experimental import pallas as pl
from jax.experimental.pallas import tpu as pltpu

F32 = jnp.float32
MXU_DTYPE = jnp.bfloat16

EPS = 1e-6
NEG = -1e30
FORCE = 1e4

CONV_WIDTH = 1024
DW_CONV_SIZE = 31
CONV_HALO = 32
CONV_SUB = 64
SG_WIDTH = 1024
SG_CHUNK = 128
SG_GROUPS = 8
N_HEADS = 16
N_KV_GROUPS = 4
HEADS_PER_GROUP = 4
HEAD_DIM = 64
CMP_BLOCK = 32
CMP_STRIDE = 16
CMP_HIDDEN = 256
SLC_BLOCK = 64
SLC_TOP_N = 16
WINDOW = 512
D_FF = 5632
FFN_HALO = 16
GATE_PAD = 128
GATE_ROWS = 16

VMEM_LIMIT = 56 * 1024 * 1024


def _cparams(sem):
    return pltpu.CompilerParams(dimension_semantics=sem, vmem_limit_bytes=VMEM_LIMIT)


def _sigmoid(x):
    return jax.nn.sigmoid(x)


def _rmsnorm_kernel(x_ref, w_ref, o_ref):
    x = x_ref[...]
    ms = jnp.mean(x * x, axis=-1, keepdims=True)
    o_ref[...] = (x * lax.rsqrt(ms + EPS) * w_ref[...]).astype(o_ref.dtype)


def _rmsnorm_cast(x2d, w):
    m, d = x2d.shape
    tm = min(512, m)
    return pl.pallas_call(
        _rmsnorm_kernel,
        grid=(m // tm,),
        in_specs=[pl.BlockSpec((tm, d), lambda i: (i, 0)),
                  pl.BlockSpec((1, d), lambda i: (0, 0))],
        out_specs=pl.BlockSpec((tm, d), lambda i: (i, 0)),
        out_shape=jax.ShapeDtypeStruct((m, d), MXU_DTYPE),
        compiler_params=_cparams(("parallel",)),
        name="rmsnorm_cast",
    )(x2d, w.reshape(1, d))


def _mm_kernel(a_ref, w_ref, o_ref):
    o_ref[...] = jnp.dot(a_ref[...], w_ref[...], preferred_element_type=F32).astype(o_ref.dtype)


def _pick(n, cands):
    for c in cands:
        if n % c == 0:
            return c
    raise ValueError(f"no tile for {n}")


def _matmul(a, w, out_dtype=F32, name="matmul"):
    m, k = a.shape
    n = w.shape[1]
    tm = min(1024, m)
    tn = _pick(n, (512, 384, 256, 128))
    return pl.pallas_call(
        _mm_kernel,
        grid=(m // tm, n // tn),
        in_specs=[pl.BlockSpec((tm, k), lambda i, j: (i, 0)),
                  pl.BlockSpec((k, tn), lambda i, j: (0, j))],
        out_specs=pl.BlockSpec((tm, tn), lambda i, j: (i, j)),
        out_shape=jax.ShapeDtypeStruct((m, n), out_dtype),
        compiler_params=_cparams(("parallel", "arbitrary")),
        name=name,
    )(a, w)


def _conformer_kernel(cur_ref, halo_ref, cw_ref, cb_ref, lg_ref, lb_ref, o_ref, hbuf, cbuf, *, ts):
    c = CONV_WIDTH
    i = pl.program_id(1)
    cur = cur_ref[...]
    hbuf[pl.ds(CONV_HALO, ts), :] = cur[:, :c] * _sigmoid(cur[:, c:])
    hal = halo_ref[...]
    hh = hal[:, :c] * _sigmoid(hal[:, c:])
    hbuf[pl.ds(0, CONV_HALO), :] = jnp.where(i == 0, 0.0, hh)

    first = CONV_HALO - (DW_CONV_SIZE - 1)
    span = CONV_SUB + CONV_HALO

    def blk_body(idx, carry):
        off = pl.multiple_of((idx // (ts // CONV_SUB)) * 128, 128)
        r0 = pl.multiple_of((idx % (ts // CONV_SUB)) * CONV_SUB, CONV_SUB)
        x = hbuf[pl.ds(r0, span), pl.ds(off, 128)]
        acc = jnp.broadcast_to(cb_ref[:, pl.ds(off, 128)], (CONV_SUB, 128))
        for r in range(8):
            xr = x if r == 0 else pltpu.roll(x, span - r, 0)
            for a in range(CONV_HALO // 8 + 1):
                k = 8 * a + r - first
                if 0 <= k < DW_CONV_SIZE:
                    acc = acc + cw_ref[pl.ds(k, 1), pl.ds(off, 128)] * xr[8 * a:8 * a + CONV_SUB]
        cbuf[pl.ds(r0, CONV_SUB), pl.ds(off, 128)] = acc
        return carry

    lax.fori_loop(0, (c // 128) * (ts // CONV_SUB), blk_body, 0)
    y = cbuf[...]
    mean = jnp.mean(y, axis=-1, keepdims=True)
    yc = y - mean
    var = jnp.mean(yc * yc, axis=-1, keepdims=True)
    z = yc * lax.rsqrt(var + EPS) * lg_ref[...] + lb_ref[...]
    o_ref[...] = (z * _sigmoid(z)).astype(o_ref.dtype)


def _conformer(a_in, conv_w, conv_b, ln_g, ln_b):
    bsz, seq, _ = a_in.shape
    c = CONV_WIDTH
    ts = min(256, seq)
    per = ts // CONV_HALO
    kern = functools.partial(_conformer_kernel, ts=ts)
    return pl.pallas_call(
        kern,
        grid=(bsz, seq // ts),
        in_specs=[pl.BlockSpec((None, ts, 2 * c), lambda b, i: (b, i, 0)),
                  pl.BlockSpec((None, CONV_HALO, 2 * c), lambda b, i: (b, jnp.maximum(i * per - 1, 0), 0)),
                  pl.BlockSpec((DW_CONV_SIZE, c), lambda b, i: (0, 0)),
                  pl.BlockSpec((1, c), lambda b, i: (0, 0)),
                  pl.BlockSpec((1, c), lambda b, i: (0, 0)),
                  pl.BlockSpec((1, c), lambda b, i: (0, 0))],
        out_specs=pl.BlockSpec((None, ts, c), lambda b, i: (b, i, 0)),
        out_shape=jax.ShapeDtypeStruct((bsz, seq, c), MXU_DTYPE),
        scratch_shapes=[pltpu.VMEM((CONV_HALO + ts, c), F32), pltpu.VMEM((ts, c), F32)],
        compiler_params=_cparams(("parallel", "parallel")),
        name="conformer_conv",
    )(a_in, a_in, conv_w, conv_b.reshape(1, c), ln_g.reshape(1, c), ln_b.reshape(1, c))


def _gmlp_kernel(x_ref, lg_ref, lb_ref, sgw_ref, sgbt_ref, o_ref, *, ts):
    c = SG_WIDTH
    gd = c // SG_GROUPS
    x = x_ref[...]
    z = 0.5 * x * (1.0 + lax.erf(x * np.float32(np.sqrt(0.5))))
    u = z[:, :c]
    v = z[:, c:]
    mean = jnp.mean(v, axis=-1, keepdims=True)
    vc = v - mean
    var = jnp.mean(vc * vc, axis=-1, keepdims=True)
    v = vc * lax.rsqrt(var + EPS) * lg_ref[...] + lb_ref[...]
    row = lax.broadcasted_iota(jnp.int32, (SG_CHUNK, SG_CHUNK), 0)
    col = lax.broadcasted_iota(jnp.int32, (SG_CHUNK, SG_CHUNK), 1)
    causal = col <= row
    for g in range(SG_GROUPS):
        w = jnp.where(causal, sgw_ref[g], 0.0).astype(MXU_DTYPE)
        bias = sgbt_ref[:, g:g + 1]
        for ch in range(ts // SG_CHUNK):
            r0 = ch * SG_CHUNK
            vg = v[r0:r0 + SG_CHUNK, g * gd:(g + 1) * gd].astype(MXU_DTYPE)
            f = jnp.dot(w, vg, preferred_element_type=F32) + bias
            o_ref[r0:r0 + SG_CHUNK, g * gd:(g + 1) * gd] = (u[r0:r0 + SG_CHUNK, g * gd:(g + 1) * gd]
                                                            * f).astype(o_ref.dtype)


def _gmlp(b_in, ln_g, ln_b, sg_w, sg_b):
    m, _ = b_in.shape
    c = SG_WIDTH
    ts = 256
    kern = functools.partial(_gmlp_kernel, ts=ts)
    return pl.pallas_call(
        kern,
        grid=(m // ts,),
        in_specs=[pl.BlockSpec((ts, 2 * c), lambda i: (i, 0)),
                  pl.BlockSpec((1, c), lambda i: (0, 0)),
                  pl.BlockSpec((1, c), lambda i: (0, 0)),
                  pl.BlockSpec((SG_GROUPS, SG_CHUNK, SG_CHUNK), lambda i: (0, 0, 0)),
                  pl.BlockSpec((SG_CHUNK, SG_GROUPS), lambda i: (0, 0))],
        out_specs=pl.BlockSpec((ts, c), lambda i: (i, 0)),
        out_shape=jax.ShapeDtypeStruct((m, c), MXU_DTYPE),
        compiler_params=_cparams(("parallel",)),
        name="gmlp_gating",
    )(b_in, ln_g.reshape(1, c), ln_b.reshape(1, c), sg_w, sg_b.T)


def _compress_kernel(p_ref, pelo_ref, pehi_ref, w1a_ref, w1b_ref, w2_ref, o_ref):
    p = p_ref[...]
    a = jnp.dot((p + pelo_ref[...]).astype(MXU_DTYPE), w1a_ref[...], preferred_element_type=F32)
    b = jnp.dot((p + pehi_ref[...]).astype(MXU_DTYPE), w1b_ref[...], preferred_element_type=F32)
    n = p.shape[0]
    b_next = pltpu.roll(b, n - 1, 0)
    hid = a + b_next
    hid = hid * _sigmoid(hid)
    o_ref[...] = jnp.dot(hid.astype(MXU_DTYPE), w2_ref[...], preferred_element_type=F32)


def _compress(pieces, pe, w1, w2):
    bsz, g, n_piece, width = pieces.shape
    half = CMP_BLOCK // 2
    pe_lo = pe[:half].reshape(1, width)
    pe_hi = pe[half:].reshape(1, width)
    w1a = w1[:half].reshape(width, CMP_HIDDEN).astype(MXU_DTYPE)
    w1b = w1[half:].reshape(width, CMP_HIDDEN).astype(MXU_DTYPE)
    return pl.pallas_call(
        _compress_kernel,
        grid=(bsz, g),
        in_specs=[pl.BlockSpec((None, None, n_piece, width), lambda b, gg: (b, gg, 0, 0)),
                  pl.BlockSpec((1, width), lambda b, gg: (0, 0)),
                  pl.BlockSpec((1, width), lambda b, gg: (0, 0)),
                  pl.BlockSpec((width, CMP_HIDDEN), lambda b, gg: (0, 0)),
                  pl.BlockSpec((width, CMP_HIDDEN), lambda b, gg: (0, 0)),
                  pl.BlockSpec((CMP_HIDDEN, HEAD_DIM), lambda b, gg: (0, 0))],
        out_specs=pl.BlockSpec((None, None, n_piece, HEAD_DIM), lambda b, gg: (b, gg, 0, 0)),
        out_shape=jax.ShapeDtypeStruct((bsz, g, n_piece, HEAD_DIM), F32),
        compiler_params=_cparams(("parallel", "parallel")),
        name="nsa_compress",
    )(pieces, pe_lo, pe_hi, w1a, w1b, w2.astype(MXU_DTYPE))


ROW_BLK = 32
M_INIT = -3e38
NSA_PROLOGUE_PHASES = 5
NSA_PHASE_LAG = 2


def _masked_softmax(s, mask):
    s = jnp.where(mask, s, NEG)
    m = jnp.max(s, axis=-1, keepdims=True)
    e = jnp.where(mask, jnp.exp(s - m), 0.0)
    return e / jnp.maximum(jnp.sum(e, axis=-1, keepdims=True), 1e-30)


def _nt_dot(a, b, precision=None):
    return lax.dot_general(a, b, (((1,), (1,)), ((), ())), preferred_element_type=F32, precision=precision)


def _nsa_kernel(*refs, tq, ck, nt):
    n_in = 12
    shared, scratch = refs[:n_in], refs[n_in:]
    per_tile = len(scratch) // nt
    tiles = [_nsa_tile(u, *shared, *scratch[u * per_tile:(u + 1) * per_tile], tq=tq, ck=ck, nt=nt)
             for u in range(nt)]
    loops = [None] * nt
    for step in range(NSA_PROLOGUE_PHASES + NSA_PHASE_LAG * (nt - 1)):
        for u in range(nt):
            if 0 <= step - NSA_PHASE_LAG * u < NSA_PROLOGUE_PHASES:
                loops[u] = next(tiles[u])

    def pair_body(i, carry):
        for _, body in loops:
            body(i)
        return carry

    lax.fori_loop(0, loops[0][0], pair_body, 0)
    for t in tiles:
        next(t, None)


def _nsa_tile(u, q_ref, kc_ref, vct_ref, ks_ref, vst_ref, kw_ref, vwt_ref, gate_ref, mapt_ref, bt_ref, bandt_ref,
              o_ref, sc_scr, pc_scr, sw_scr, pw_scr, sa_scr, sb_scr, pa_scr, pb_scr, m_scr, ala_scr, alb_scr, acc_scr,
              mw_scr, al_scr, accw_scr, rank_scr, *, tq, ck, nt):
    hpg = HEADS_PER_GROUP
    dh = HEAD_DIM
    aug = 2 * dh
    cols = hpg * tq
    n_blk = mapt_ref.shape[0]
    q0 = (pl.program_id(2) * nt + u) * tq
    tile_rows = pl.ds(u * tq, tq)
    qp = (q_ref[:, tile_rows, :] * np.float32(dh ** -0.5).astype(q_ref.dtype)).reshape(cols, aug)
    eye4 = ((lax.broadcasted_iota(jnp.int32, (cols, tq), 0) & (tq - 1))
            == lax.broadcasted_iota(jnp.int32, (cols, tq), 1)).astype(MXU_DTYPE)
    t_lane = q0 + lax.broadcasted_iota(jnp.int32, (1, tq), 1)

    def online(s_ref, p_ref, kr, m_ref, acc_ref, v_t):
        for h in range(hpg):
            cb = pl.ds(h * tq, tq)
            s = s_ref[kr, cb]
            m_old = m_ref[:, cb]
            m_new = jnp.maximum(m_old, jnp.max(s, axis=0, keepdims=True))
            p_ref[kr, cb] = jnp.exp(s - m_new).astype(p_ref.dtype)
            al_scr[:, cb] = jnp.exp(m_old - m_new)
            m_ref[:, cb] = m_new
        acc_ref[...] = al_scr[...] * acc_ref[...] + jnp.dot(v_t, p_ref[kr, :], preferred_element_type=F32)

    def finish(acc):
        return acc[:dh] / jnp.maximum(acc[dh:dh + 1], 1e-30)

    n_cmp = kc_ref.shape[0]
    sc_scr[...] = _nt_dot(kc_ref[...].astype(MXU_DTYPE), qp)
    wlen = WINDOW + tq
    w0 = pl.multiple_of(q0, tq)
    pad_flag = (lax.broadcasted_iota(jnp.int32, (1, aug), 1) == dh).astype(MXU_DTYPE)
    lhs_win = jnp.concatenate([qp + pad_flag, eye4], axis=1)
    rhs_win = jnp.concatenate([kw_ref[pl.ds(w0, wlen), :], bandt_ref[...]], axis=1)
    sw_scr[...] = _nt_dot(rhs_win, lhs_win)
    yield None

    cmask = (lax.broadcasted_iota(jnp.int32, (n_cmp, 1), 0) * CMP_STRIDE + (CMP_BLOCK - 1)) <= t_lane
    psum = jnp.zeros((n_cmp, tq), F32)
    for h in range(hpg):
        cb = pl.ds(h * tq, tq)
        s = jnp.where(cmask, sc_scr[:, cb], NEG)
        e = jnp.where(cmask, jnp.exp(s - jnp.max(s, axis=0, keepdims=True)), 0.0)
        p = e / jnp.maximum(jnp.sum(e, axis=0, keepdims=True), 1e-30)
        pc_scr[:, cb] = p.astype(pc_scr.dtype)
        psum = psum + p
    o_cmp = jnp.dot(vct_ref[...].astype(MXU_DTYPE), pc_scr[...], preferred_element_type=F32)
    yield None

    mw_scr[...] = jnp.full(mw_scr.shape, M_INIT, F32)
    accw_scr[...] = jnp.zeros(accw_scr.shape, F32)
    for k_lo in range(0, wlen, ck):
        nk = min(ck, wlen - k_lo)
        online(sw_scr, pw_scr, pl.ds(k_lo, nk), mw_scr, accw_scr,
               vwt_ref[:, pl.ds(pl.multiple_of(w0 + k_lo, tq), nk)])
    o_win = finish(accw_scr[...])
    yield None

    imp_t = jnp.dot(mapt_ref[...], psum, preferred_element_type=F32, precision=lax.Precision.HIGHEST)
    j_t = lax.broadcasted_iota(jnp.int32, (n_blk, tq), 0)
    cur_t = (q0 + lax.broadcasted_iota(jnp.int32, (n_blk, tq), 1)) >> 6
    forced = (j_t == 0) | (j_t == cur_t) | (j_t == cur_t - 1)
    score_t = jnp.where(j_t <= cur_t, jnp.where(forced, FORCE, imp_t), NEG)
    rank_scr[...] = score_t
    sel_rows = []
    for v in range(n_blk // 8):
        sj = score_t[8 * v:8 * v + 8]
        jv = j_t[8 * v:8 * v + 8]
        rank = jnp.zeros((8, tq), jnp.int32)
        for i in range(n_blk):
            ci = rank_scr[pl.ds(i, 1), :]
            if i < 8 * v:
                ahead = ci >= sj
            elif i > 8 * v + 7:
                ahead = ci > sj
            else:
                ahead = (ci > sj) | ((ci == sj) & (jv > i))
            rank = rank + ahead.astype(jnp.int32)
        sel_rows.append(jnp.where(rank < SLC_TOP_N, 0.0, NEG))
    yield None
    selb_t = jnp.concatenate([jnp.zeros((dh, tq), F32)] + sel_rows, axis=0).astype(MXU_DTYPE)
    selb = _nt_dot(eye4[:tq], selb_t).astype(MXU_DTYPE)
    lhs_slc = jnp.concatenate([(qp.reshape(hpg, tq, aug) + selb[None]).reshape(cols, aug), eye4], axis=1)

    n_full = q0 // ck

    def scores(c, s_ref):
        k0 = pl.multiple_of(jnp.minimum(c, n_full) * ck, ck)
        start = pl.multiple_of(jnp.clip(q0 - c * ck, -tq, ck) + tq, tq)
        rhs = jnp.concatenate([ks_ref[pl.ds(k0, ck), :], bt_ref[:, pl.ds(start, tq)]], axis=1)
        s_ref[...] = _nt_dot(rhs, lhs_slc)

    def softmax(s_ref, p_ref, a_ref):
        for h in range(hpg):
            cb = pl.ds(h * tq, tq)
            s = s_ref[:, cb]
            m_old = m_scr[:, cb]
            m_new = jnp.maximum(m_old, jnp.max(s, axis=0, keepdims=True))
            p_ref[:, cb] = jnp.exp(s - m_new).astype(p_ref.dtype)
            a_ref[:, cb] = jnp.exp(m_old - m_new)
            m_scr[:, cb] = m_new

    def accumulate(c, p_ref, a_ref):
        k0 = pl.multiple_of(jnp.clip(c, 0, n_full) * ck, ck)
        acc_scr[...] = a_ref[...] * acc_scr[...] + jnp.dot(vst_ref[:, pl.ds(k0, ck)], p_ref[...],
                                                           preferred_element_type=F32)

    m_scr[...] = jnp.full(m_scr.shape, M_INIT, F32)
    acc_scr[...] = jnp.zeros(acc_scr.shape, F32)
    pb_scr[...] = jnp.zeros(pb_scr.shape, pb_scr.dtype)
    alb_scr[...] = jnp.ones(alb_scr.shape, F32)
    scores(0, sa_scr)

    def pair_body(i):
        k = 2 * i
        scores(k + 1, sb_scr)
        softmax(sa_scr, pa_scr, ala_scr)
        accumulate(k - 1, pb_scr, alb_scr)
        scores(k + 2, sa_scr)
        softmax(sb_scr, pb_scr, alb_scr)
        accumulate(k, pa_scr, ala_scr)

    n_pairs = (n_full + 2) // 2
    yield n_pairs, pair_body
    accumulate(2 * n_pairs - 1, pb_scr, alb_scr)
    o_slc = finish(acc_scr[...])

    gl = _sigmoid(gate_ref[:, tile_rows])
    outs = []
    for h in range(hpg):
        cb = slice(h * tq, (h + 1) * tq)
        outs.append(gl[3 * h:3 * h + 1] * o_cmp[:, cb] + gl[3 * h + 1:3 * h + 2] * o_slc[:, cb]
                    + gl[3 * h + 2:3 * h + 3] * o_win[:, cb])
    o_t = jnp.concatenate(outs, axis=0).astype(MXU_DTYPE)
    o_ref[tile_rows, :] = _nt_dot(eye4[:tq], o_t).astype(o_ref.dtype)


def _slc_map_t(n_piece, n_slc, n_blk):
    r = CMP_BLOCK // CMP_STRIDE
    a = SLC_BLOCK // CMP_STRIDE
    m = np.zeros((n_blk, n_piece), np.float32)
    for n in range(n_piece - r + 1):
        for i in range(r):
            m[(n + i) // a, n] += 1.0
    return m


def _nsa_attention(q, kvr, kc, vc, gate_logits):
    bsz, seq, _ = q.shape
    g, hpg, dh = N_KV_GROUPS, HEADS_PER_GROUP, HEAD_DIM
    aug = 2 * dh
    n_piece = kc.shape[2]
    n_slc = seq // SLC_BLOCK
    n_blk = dh
    assert n_slc <= n_blk
    tq = 128
    ck = 256
    wlen = WINDOW + tq
    cols = hpg * tq

    q_r = jnp.pad(q.reshape(bsz, seq, g, hpg, dh), ((0, 0),) * 4 + ((0, dh),)).transpose(0, 2, 3, 1, 4)
    q_r = q_r.astype(MXU_DTYPE)
    blk = np.arange(seq) // SLC_BLOCK
    onehot = (blk[:, None] == np.arange(dh)[None, :]).astype(np.float32)
    ones_row = np.zeros((dh, seq), np.float32)
    ones_row[0] = 1.0
    bc = lambda c: jnp.broadcast_to(jnp.asarray(c, MXU_DTYPE), (bsz, g) + c.shape)
    tr = lambda a: a.astype(MXU_DTYPE).transpose(0, 1, 3, 2)
    kc_p = jnp.pad(kc, ((0, 0), (0, 0), (0, 0), (0, dh)))
    vc_t = tr(vc)
    ks = jnp.concatenate([kvr[2].astype(MXU_DTYPE), bc(onehot)], axis=-1)
    vs_t = jnp.concatenate([tr(kvr[3]), bc(ones_row)], axis=2)
    pad_k = np.zeros((WINDOW, aug), np.float32)
    pad_k[:, dh] = NEG
    kw = jnp.concatenate([kvr[4].astype(MXU_DTYPE), jnp.zeros((bsz, g, seq, dh), MXU_DTYPE)], axis=-1)
    kw = jnp.concatenate([bc(pad_k), kw], axis=2)
    vw_t = jnp.concatenate([tr(kvr[5]), bc(ones_row)], axis=2)
    vw_t = jnp.concatenate([jnp.zeros((bsz, g, aug, WINDOW), MXU_DTYPE), vw_t], axis=3)
    gate_t = gate_logits.reshape(bsz, seq, g, GATE_PAD)[..., :GATE_ROWS].transpose(0, 2, 3, 1)

    kk = np.arange(ck)[:, None]
    bt = np.where(kk <= np.arange(-tq, ck + tq)[None, :], 0.0, NEG).astype(np.float32)
    kw_i = np.arange(wlen)[:, None]
    tt = np.arange(tq)[None, :]
    bandt = np.where((kw_i <= tt + WINDOW) & (kw_i > tt), 0.0, NEG).astype(np.float32)

    nt = ck // tq
    kern = functools.partial(_nsa_kernel, tq=tq, ck=ck, nt=nt)
    per_group = lambda r, c: pl.BlockSpec((None, None, r, c), lambda b, gg, i: (b, gg, 0, 0))
    const = lambda shape: pl.BlockSpec(shape, lambda b, gg, i: (0, 0))
    per_tile = pltpu.VMEM
    return pl.pallas_call(
        kern,
        grid=(bsz, g, seq // (nt * tq)),
        in_specs=[pl.BlockSpec((None, None, hpg, nt * tq, aug), lambda b, gg, i: (b, gg, 0, i, 0)),
                  per_group(n_piece, aug), per_group(dh, n_piece),
                  per_group(seq, aug), per_group(aug, seq),
                  per_group(WINDOW + seq, aug), per_group(aug, WINDOW + seq),
                  pl.BlockSpec((None, None, GATE_ROWS, nt * tq), lambda b, gg, i: (b, gg, 0, i)),
                  const((n_blk, n_piece)), const((ck, ck + 2 * tq)), const((wlen, tq))],
        out_specs=pl.BlockSpec((None, nt * tq, hpg * dh), lambda b, gg, i: (b, i, gg)),
        out_shape=jax.ShapeDtypeStruct((bsz, seq, g * hpg * dh), MXU_DTYPE),
        scratch_shapes=[per_tile((n_piece, cols), F32),
                        per_tile((n_piece, cols), MXU_DTYPE),
                        per_tile((wlen, cols), F32),
                        per_tile((wlen, cols), MXU_DTYPE),
                        per_tile((ck, cols), F32),
                        per_tile((ck, cols), F32),
                        per_tile((ck, cols), MXU_DTYPE),
                        per_tile((ck, cols), MXU_DTYPE),
                        per_tile((1, cols), F32),
                        per_tile((1, cols), F32),
                        per_tile((1, cols), F32),
                        per_tile((aug, cols), F32),
                        per_tile((1, cols), F32),
                        per_tile((1, cols), F32),
                        per_tile((aug, cols), F32),
                        per_tile((n_blk, tq), F32)] * nt,
        compiler_params=_cparams(("parallel", "parallel", "arbitrary")),
        name="nsa_attention",
    )(q_r, kc_p, vc_t, ks, vs_t, kw, vw_t, gate_t, jnp.asarray(_slc_map_t(n_piece, n_slc, n_blk)),
      jnp.asarray(bt, MXU_DTYPE), jnp.asarray(bandt, MXU_DTYPE))


def _merge_kernel(ha_ref, hb_ref, hc_ref, gl_ref, x_ref, wa_ref, wb_ref, wc_ref, wo_ref, nw_ref, o_ref, mix):
    d = x_ref.shape[1]
    branches = ((ha_ref, wa_ref), (hb_ref, wb_ref), (hc_ref, wc_ref))
    for n, (h_ref, w_ref) in enumerate(branches):
        y = _sigmoid(gl_ref[:, n * d:(n + 1) * d]) * jnp.dot(h_ref[...], w_ref[...], preferred_element_type=F32)
        if n == 0:
            mix[...] = y
        else:
            mix[...] += y
    r = jnp.dot(mix[...].astype(MXU_DTYPE), wo_ref[...], preferred_element_type=F32)
    ms = jnp.mean(r * r, axis=-1, keepdims=True)
    o_ref[...] = x_ref[...] + r * lax.rsqrt(ms + EPS) * nw_ref[...]


def _merge(ha, hb, hc, merge_logits, x2d, w_a, w_b, w_c, w_o, norm_w):
    m, d = x2d.shape
    c = ha.shape[1]
    tm = 256
    row = lambda i: (i, 0)
    resident = lambda shape: pl.BlockSpec(shape, lambda i: (0, 0), pipeline_mode=pl.Buffered(1))
    wc = lambda w: w.astype(MXU_DTYPE)
    return pl.pallas_call(
        _merge_kernel,
        grid=(m // tm,),
        in_specs=[pl.BlockSpec((tm, c), row), pl.BlockSpec((tm, c), row), pl.BlockSpec((tm, c), row),
                  pl.BlockSpec((tm, 3 * d), row), pl.BlockSpec((tm, d), row),
                  resident((c, d)), resident((c, d)), resident((c, d)), resident((d, d)),
                  pl.BlockSpec((1, d), lambda i: (0, 0))],
        out_specs=pl.BlockSpec((tm, d), row),
        out_shape=jax.ShapeDtypeStruct((m, d), F32),
        scratch_shapes=[pltpu.VMEM((tm, d), F32)],
        compiler_params=_cparams(("parallel",)),
        name="merge_out_proj",
    )(ha, hb, hc, merge_logits, x2d, wc(w_a), wc(w_b), wc(w_c), wc(w_o), norm_w.reshape(1, d))


def _ffn_up_kernel(h_ref, halo_ref, wg_ref, wv_ref, cwg_ref, cwv_ref, cbg_ref, cbv_ref, o_ref,
                   hbuf, ugbuf, uvbuf, *, tm, seq):
    i = pl.program_id(0)
    j = pl.program_id(1)

    @pl.when(j == 0)
    def _():
        hbuf[pl.ds(FFN_HALO, tm), :] = h_ref[...]
        first = (i * tm) % seq == 0
        hbuf[pl.ds(0, FFN_HALO), :] = jnp.where(first, jnp.zeros_like(halo_ref[...]), halo_ref[...])

    lhs = hbuf[...]
    ugbuf[...] = jnp.dot(lhs, wg_ref[...], preferred_element_type=F32)
    uvbuf[...] = jnp.dot(lhs, wv_ref[...], preferred_element_type=F32)

    def conv(buf, cw_ref, cb_ref):
        return (cb_ref[...] + cw_ref[pl.ds(2, 1), :] * buf[pl.ds(FFN_HALO, tm), :]
                + cw_ref[pl.ds(1, 1), :] * buf[pl.ds(FFN_HALO - 1, tm), :]
                + cw_ref[pl.ds(0, 1), :] * buf[pl.ds(FFN_HALO - 2, tm), :])

    cg = conv(ugbuf, cwg_ref, cbg_ref)
    cv = conv(uvbuf, cwv_ref, cbv_ref)
    o_ref[...] = (cg * _sigmoid(cg) * cv).astype(o_ref.dtype)


def _ffn_up(h, w_up, conv_w, conv_b, seq):
    m, d = h.shape
    dff = w_up.shape[1] // 2
    tm = min(1024, seq)
    tn = _pick(dff, (512, 256, 128))
    nj = dff // tn
    per = tm // FFN_HALO
    kern = functools.partial(_ffn_up_kernel, tm=tm, seq=seq)
    cb = conv_b.reshape(1, 2 * dff)
    return pl.pallas_call(
        kern,
        grid=(m // tm, nj),
        in_specs=[pl.BlockSpec((tm, d), lambda i, j: (i, 0)),
                  pl.BlockSpec((FFN_HALO, d), lambda i, j: (jnp.maximum(i * per - 1, 0), 0)),
                  pl.BlockSpec((d, tn), lambda i, j: (0, j)),
                  pl.BlockSpec((d, tn), lambda i, j: (0, j + nj)),
                  pl.BlockSpec((3, tn), lambda i, j: (0, j)),
                  pl.BlockSpec((3, tn), lambda i, j: (0, j + nj)),
                  pl.BlockSpec((1, tn), lambda i, j: (0, j)),
                  pl.BlockSpec((1, tn), lambda i, j: (0, j + nj))],
        out_specs=pl.BlockSpec((tm, tn), lambda i, j: (i, j)),
        out_shape=jax.ShapeDtypeStruct((m, dff), MXU_DTYPE),
        scratch_shapes=[pltpu.VMEM((FFN_HALO + tm, d), MXU_DTYPE),
                        pltpu.VMEM((FFN_HALO + tm, tn), F32),
                        pltpu.VMEM((FFN_HALO + tm, tn), F32)],
        compiler_params=_cparams(("parallel", "arbitrary")),
        name="ffn_up_conv_gate",
    )(h, h, w_up, w_up, conv_w, conv_w, cb, cb)


def _ffn_down_kernel(a_ref, w_ref, x_ref, nw_ref, o_ref):
    r = jnp.dot(a_ref[...], w_ref[...], preferred_element_type=F32)
    ms = jnp.mean(r * r, axis=-1, keepdims=True)
    o_ref[...] = x_ref[...] + r * lax.rsqrt(ms + EPS) * nw_ref[...]


def _ffn_down(act, w_down, x2d, norm_w):
    m, dff = act.shape
    d = w_down.shape[1]
    tm = 256
    return pl.pallas_call(
        _ffn_down_kernel,
        grid=(m // tm,),
        in_specs=[pl.BlockSpec((tm, dff), lambda i: (i, 0)),
                  pl.BlockSpec((dff, d), lambda i: (0, 0), pipeline_mode=pl.Buffered(1)),
                  pl.BlockSpec((tm, d), lambda i: (i, 0)),
                  pl.BlockSpec((1, d), lambda i: (0, 0))],
        out_specs=pl.BlockSpec((tm, d), lambda i: (i, 0)),
        out_shape=jax.ShapeDtypeStruct((m, d), F32),
        compiler_params=_cparams(("parallel",)),
        name="ffn_down_norm_res",
    )(act, w_down, x2d, norm_w.reshape(1, d))


def _gate_weight(w_gate):
    d = w_gate.shape[0]
    per = HEADS_PER_GROUP * 3
    w = w_gate.reshape(d, N_KV_GROUPS, per)
    w = jnp.pad(w, ((0, 0), (0, 0), (0, GATE_PAD - per)))
    return w.reshape(d, N_KV_GROUPS * GATE_PAD)


def _mixer_layer(x, p, l):
    bsz, seq, d = x.shape
    m = bsz * seq
    x2d = x.reshape(m, d)
    g, hpg, dh = N_KV_GROUPS, HEADS_PER_GROUP, HEAD_DIM
    a_in_w = 2 * CONV_WIDTH
    b_in_w = 2 * SG_WIDTH
    q_w = N_HEADS * dh
    kv_w = 6 * g * dh
    gate_w = 3 * N_HEADS
    o0 = 0
    o1 = o0 + a_in_w
    o2 = o1 + b_in_w
    o3 = o2 + q_w
    o4 = o3 + kv_w
    o5 = o4 + gate_w
    w_in = p["w_in"][l]

    h = _rmsnorm_cast(x2d, p["norm_mix_pre"][l])
    wc = lambda w: w.astype(MXU_DTYPE)
    a_in = _matmul(h, wc(w_in[:, o0:o1]), name="proj_a")
    b_in = _matmul(h, wc(w_in[:, o1:o2]), name="proj_b")
    q = _matmul(h, wc(w_in[:, o2:o3]), name="proj_q")
    kv = _matmul(h, wc(w_in[:, o3:o4]), name="proj_kv")
    gate_logits = _matmul(h, wc(_gate_weight(w_in[:, o4:o5])), name="proj_gate")
    merge_logits = _matmul(h, wc(w_in[:, o5:]), name="proj_merge")

    h_a = _conformer(a_in.reshape(bsz, seq, a_in_w), p["conv_a_w"][l], p["conv_a_b"][l],
                     p["ln_a_g"][l], p["ln_a_b"][l]).reshape(m, CONV_WIDTH)
    h_b = _gmlp(b_in, p["ln_b_g"][l], p["ln_b_b"][l], p["sg_w"][l], p["sg_b"][l])

    kvr = kv.reshape(bsz, seq, 6, g, dh).transpose(2, 0, 3, 1, 4)
    n_piece = seq // CMP_STRIDE
    kc = _compress(kvr[0].reshape(bsz, g, n_piece, CMP_STRIDE * dh), p["cmp_pe_k"][l], p["cmp_w1_k"][l],
                   p["cmp_w2_k"][l])
    vc = _compress(kvr[1].reshape(bsz, g, n_piece, CMP_STRIDE * dh), p["cmp_pe_v"][l], p["cmp_w1_v"][l],
                   p["cmp_w2_v"][l])
    o = _nsa_attention(q.reshape(bsz, seq, N_HEADS * dh), kvr, kc, vc, gate_logits.reshape(bsz, seq, g * GATE_PAD))
    h_c = o.reshape(m, N_HEADS * dh)

    return _merge(h_a, h_b, h_c, merge_logits, x2d, p["w_a_out"][l], p["w_b_out"][l], p["w_c_out"][l], p["w_o"][l],
                  p["norm_mix_post"][l]).reshape(bsz, seq, d)


def _ffn_layer(x, p, l):
    bsz, seq, d = x.shape
    m = bsz * seq
    x2d = x.reshape(m, d)
    h = _rmsnorm_cast(x2d, p["norm_ffn_pre"][l])
    act = _ffn_up(h, p["w_up"][l].astype(MXU_DTYPE), p["ffn_conv_w"][l], p["ffn_conv_b"][l], seq)
    return _ffn_down(act, p["w_down"][l].astype(MXU_DTYPE), x2d, p["norm_ffn_post"][l]).reshape(bsz, seq, d)


def kernel(x, norm_mix_pre, norm_mix_post, norm_ffn_pre, norm_ffn_post, w_in, conv_a_w, conv_a_b, ln_a_g, ln_a_b,
           w_a_out, ln_b_g, ln_b_b, sg_w, sg_b, w_b_out, cmp_pe_k, cmp_w1_k, cmp_w2_k, cmp_pe_v, cmp_w1_v,
           cmp_w2_v, w_c_out, w_o, w_up, ffn_conv_w, ffn_conv_b, w_down):
    p = dict(norm_mix_pre=norm_mix_pre, norm_mix_post=norm_mix_post, norm_ffn_pre=norm_ffn_pre,
             norm_ffn_post=norm_ffn_post, w_in=w_in, conv_a_w=conv_a_w, conv_a_b=conv_a_b, ln_a_g=ln_a_g,
             ln_a_b=ln_a_b, w_a_out=w_a_out, ln_b_g=ln_b_g, ln_b_b=ln_b_b, sg_w=sg_w, sg_b=sg_b, w_b_out=w_b_out,
             cmp_pe_k=cmp_pe_k, cmp_w1_k=cmp_w1_k, cmp_w2_k=cmp_w2_k, cmp_pe_v=cmp_pe_v, cmp_w1_v=cmp_w1_v,
             cmp_w2_v=cmp_w2_v, w_c_out=w_c_out, w_o=w_o, w_up=w_up, ffn_conv_w=ffn_conv_w,
             ffn_conv_b=ffn_conv_b, w_down=w_down)
    for l in range(w_in.shape[0]):
        x = _mixer_layer(x, p, l)
        x = _ffn_layer(x, p, l)
    return x
```

```python
import functools

import numpy as np
import jax
import jax.numpy as jnp
from jax import lax
from jax.experimental import pallas as pl
from jax.experimental.pallas import tpu as pltpu

F32 = jnp.float32
MXU_DTYPE = jnp.bfloat16

EPS = 1e-6
NEG = -1e30
FORCE = 1e4

CONV_WIDTH = 1024
DW_CONV_SIZE = 31
CONV_HALO = 32
CONV_SUB = 64
SG_WIDTH = 1024
SG_CHUNK = 128
SG_GROUPS = 8
N_HEADS = 16
N_KV_GROUPS = 4
HEADS_PER_GROUP = 4
HEAD_DIM = 64
CMP_BLOCK = 32
CMP_STRIDE = 16
CMP_HIDDEN = 256
SLC_BLOCK = 64
SLC_TOP_N = 16
WINDOW = 512
D_FF = 5632
FFN_HALO = 16
GATE_PAD = 128
GATE_ROWS = 16

VMEM_LIMIT = 56 * 1024 * 1024


def _cparams(sem):
    return pltpu.CompilerParams(dimension_semantics=sem, vmem_limit_bytes=VMEM_LIMIT)


def _sigmoid(x):
    return jax.nn.sigmoid(x)


def _rmsnorm_kernel(x_ref, w_ref, o_ref):
    x = x_ref[...]
    ms = jnp.mean(x * x, axis=-1, keepdims=True)
    o_ref[...] = (x * lax.rsqrt(ms + EPS) * w_ref[...]).astype(o_ref.dtype)


def _rmsnorm_cast(x2d, w):
    m, d = x2d.shape
    tm = min(512, m)
    return pl.pallas_call(
        _rmsnorm_kernel,
        grid=(m // tm,),
        in_specs=[pl.BlockSpec((tm, d), lambda i: (i, 0)),
                  pl.BlockSpec((1, d), lambda i: (0, 0))],
        out_specs=pl.BlockSpec((tm, d), lambda i: (i, 0)),
        out_shape=jax.ShapeDtypeStruct((m, d), MXU_DTYPE),
        compiler_params=_cparams(("parallel",)),
        name="rmsnorm_cast",
    )(x2d, w.reshape(1, d))


def _mm_kernel(a_ref, w_ref, o_ref):
    o_ref[...] = jnp.dot(a_ref[...], w_ref[...], preferred_element_type=F32).astype(o_ref.dtype)


def _pick(n, cands):
    for c in cands:
        if n % c == 0:
            return c
    raise ValueError(f"no tile for {n}")


def _matmul(a, w, out_dtype=F32, name="matmul"):
    m, k = a.shape
    n = w.shape[1]
    tm = min(1024, m)
    tn = _pick(n, (512, 384, 256, 128))
    return pl.pallas_call(
        _mm_kernel,
        grid=(m // tm, n // tn),
        in_specs=[pl.BlockSpec((tm, k), lambda i, j: (i, 0)),
                  pl.BlockSpec((k, tn), lambda i, j: (0, j))],
        out_specs=pl.BlockSpec((tm, tn), lambda i, j: (i, j)),
        out_shape=jax.ShapeDtypeStruct((m, n), out_dtype),
        compiler_params=_cparams(("parallel", "arbitrary")),
        name=name,
    )(a, w)


def _proj_planes_kernel(h_ref, w_ref, add_ref, o_ref, *, add_lo, add_hi, width):
    j = pl.program_id(2)
    res = jnp.dot(h_ref[...], w_ref[...], preferred_element_type=F32)
    if add_hi > add_lo:
        add = jnp.where((j >= add_lo) & (j < add_hi), add_ref[...], 0.0)
    for p in range(o_ref.shape[0]):
        piece = res[:, p * width:(p + 1) * width]
        o_ref[p] = (piece + add if add_hi > add_lo else piece).astype(o_ref.dtype)


def _proj_planes(h3, w, add, out_dtype, width, add_range, name):
    bsz, seq, d = h3.shape
    n = w.shape[1]
    slab = 256
    per = slab // width
    tm = min(512, seq)
    kern = functools.partial(_proj_planes_kernel, add_lo=add_range[0], add_hi=add_range[1], width=width)
    return pl.pallas_call(
        kern,
        grid=(bsz, seq // tm, n // slab),
        in_specs=[pl.BlockSpec((None, tm, d), lambda b, i, j: (b, i, 0)),
                  pl.BlockSpec((d, slab), lambda b, i, j: (0, j)),
                  pl.BlockSpec((tm, width), lambda b, i, j: (i, 0))],
        out_specs=pl.BlockSpec((None, per, tm, width), lambda b, i, j: (b, j, i, 0)),
        out_shape=jax.ShapeDtypeStruct((bsz, n // width, seq, width), out_dtype),
        compiler_params=_cparams(("parallel", "parallel", "arbitrary")),
        name=name,
    )(h3, w, add)


def _proj_cols_kernel(h_ref, wt_ref, add_ref, o_ref):
    res = lax.dot_general(wt_ref[...], h_ref[...], (((1,), (1,)), ((), ())), preferred_element_type=F32)
    o_ref[...] = (res + add_ref[...]).astype(o_ref.dtype).reshape(o_ref.shape)


def _proj_cols(h3, w_t, add, planes):
    bsz, seq, d = h3.shape
    n = w_t.shape[0]
    tm = min(512, seq)
    return pl.pallas_call(
        _proj_cols_kernel,
        grid=(bsz, seq // tm),
        in_specs=[pl.BlockSpec((None, tm, d), lambda b, i: (b, i, 0)),
                  pl.BlockSpec((n, d), lambda b, i: (0, 0), pipeline_mode=pl.Buffered(1)),
                  pl.BlockSpec((n, tm), lambda b, i: (0, 0), pipeline_mode=pl.Buffered(1))],
        out_specs=pl.BlockSpec((None, planes, n // planes, tm), lambda b, i: (b, 0, 0, i)),
        out_shape=jax.ShapeDtypeStruct((bsz, planes, n // planes, seq), MXU_DTYPE),
        compiler_params=_cparams(("parallel", "parallel")),
        name="proj_attn_cols",
    )(h3, w_t, add)


def _conformer_kernel(cur_ref, halo_ref, cw_ref, cb_ref, lg_ref, lb_ref, o_ref, hbuf, cbuf, *, ts):
    c = CONV_WIDTH
    i = pl.program_id(1)
    cur = cur_ref[...]
    hbuf[pl.ds(CONV_HALO, ts), :] = cur[:, :c] * _sigmoid(cur[:, c:])
    hal = halo_ref[...]
    hh = hal[:, :c] * _sigmoid(hal[:, c:])
    hbuf[pl.ds(0, CONV_HALO), :] = jnp.where(i == 0, 0.0, hh)

    first = CONV_HALO - (DW_CONV_SIZE - 1)
    span = CONV_SUB + CONV_HALO

    def blk_body(idx, carry):
        off = pl.multiple_of((idx // (ts // CONV_SUB)) * 128, 128)
        r0 = pl.multiple_of((idx % (ts // CONV_SUB)) * CONV_SUB, CONV_SUB)
        x = hbuf[pl.ds(r0, span), pl.ds(off, 128)]
        acc = jnp.broadcast_to(cb_ref[:, pl.ds(off, 128)], (CONV_SUB, 128))
        for r in range(8):
            xr = x if r == 0 else pltpu.roll(x, span - r, 0)
            for a in range(CONV_HALO // 8 + 1):
                k = 8 * a + r - first
                if 0 <= k < DW_CONV_SIZE:
                    acc = acc + cw_ref[pl.ds(k, 1), pl.ds(off, 128)] * xr[8 * a:8 * a + CONV_SUB]
        cbuf[pl.ds(r0, CONV_SUB), pl.ds(off, 128)] = acc
        return carry

    lax.fori_loop(0, (c // 128) * (ts // CONV_SUB), blk_body, 0)
    y = cbuf[...]
    mean = jnp.mean(y, axis=-1, keepdims=True)
    yc = y - mean
    var = jnp.mean(yc * yc, axis=-1, keepdims=True)
    z = yc * lax.rsqrt(var + EPS) * lg_ref[...] + lb_ref[...]
    o_ref[...] = (z * _sigmoid(z)).astype(o_ref.dtype)


def _conformer(a_in, conv_w, conv_b, ln_g, ln_b):
    bsz, seq, _ = a_in.shape
    c = CONV_WIDTH
    ts = min(256, seq)
    per = ts // CONV_HALO
    kern = functools.partial(_conformer_kernel, ts=ts)
    return pl.pallas_call(
        kern,
        grid=(bsz, seq // ts),
        in_specs=[pl.BlockSpec((None, ts, 2 * c), lambda b, i: (b, i, 0)),
                  pl.BlockSpec((None, CONV_HALO, 2 * c), lambda b, i: (b, jnp.maximum(i * per - 1, 0), 0)),
                  pl.BlockSpec((DW_CONV_SIZE, c), lambda b, i: (0, 0)),
                  pl.BlockSpec((1, c), lambda b, i: (0, 0)),
                  pl.BlockSpec((1, c), lambda b, i: (0, 0)),
                  pl.BlockSpec((1, c), lambda b, i: (0, 0))],
        out_specs=pl.BlockSpec((None, ts, c), lambda b, i: (b, i, 0)),
        out_shape=jax.ShapeDtypeStruct((bsz, seq, c), MXU_DTYPE),
        scratch_shapes=[pltpu.VMEM((CONV_HALO + ts, c), F32), pltpu.VMEM((ts, c), F32)],
        compiler_params=_cparams(("parallel", "parallel")),
        name="conformer_conv",
    )(a_in, a_in, conv_w, conv_b.reshape(1, c), ln_g.reshape(1, c), ln_b.reshape(1, c))


def _gmlp_kernel(x_ref, lg_ref, lb_ref, sgw_ref, sgbt_ref, o_ref, *, ts):
    c = SG_WIDTH
    gd = c // SG_GROUPS
    x = x_ref[...]
    z = 0.5 * x * (1.0 + lax.erf(x * np.float32(np.sqrt(0.5))))
    u = z[:, :c]
    v = z[:, c:]
    mean = jnp.mean(v, axis=-1, keepdims=True)
    vc = v - mean
    var = jnp.mean(vc * vc, axis=-1, keepdims=True)
    v = vc * lax.rsqrt(var + EPS) * lg_ref[...] + lb_ref[...]
    row = lax.broadcasted_iota(jnp.int32, (SG_CHUNK, SG_CHUNK), 0)
    col = lax.broadcasted_iota(jnp.int32, (SG_CHUNK, SG_CHUNK), 1)
    causal = col <= row
    for g in range(SG_GROUPS):
        w = jnp.where(causal, sgw_ref[g], 0.0).astype(MXU_DTYPE)
        bias = sgbt_ref[:, g:g + 1]
        for ch in range(ts // SG_CHUNK):
            r0 = ch * SG_CHUNK
            vg = v[r0:r0 + SG_CHUNK, g * gd:(g + 1) * gd].astype(MXU_DTYPE)
            f = jnp.dot(w, vg, preferred_element_type=F32) + bias
            o_ref[r0:r0 + SG_CHUNK, g * gd:(g + 1) * gd] = (u[r0:r0 + SG_CHUNK, g * gd:(g + 1) * gd]
                                                            * f).astype(o_ref.dtype)


def _gmlp(b_in, ln_g, ln_b, sg_w, sg_b):
    m, _ = b_in.shape
    c = SG_WIDTH
    ts = 256
    kern = functools.partial(_gmlp_kernel, ts=ts)
    return pl.pallas_call(
        kern,
        grid=(m // ts,),
        in_specs=[pl.BlockSpec((ts, 2 * c), lambda i: (i, 0)),
                  pl.BlockSpec((1, c), lambda i: (0, 0)),
                  pl.BlockSpec((1, c), lambda i: (0, 0)),
                  pl.BlockSpec((SG_GROUPS, SG_CHUNK, SG_CHUNK), lambda i: (0, 0, 0)),
                  pl.BlockSpec((SG_CHUNK, SG_GROUPS), lambda i: (0, 0))],
        out_specs=pl.BlockSpec((ts, c), lambda i: (i, 0)),
        out_shape=jax.ShapeDtypeStruct((m, c), MXU_DTYPE),
        compiler_params=_cparams(("parallel",)),
        name="gmlp_gating",
    )(b_in, ln_g.reshape(1, c), ln_b.reshape(1, c), sg_w, sg_b.T)


def _compress_kernel(p_ref, pelo_ref, pehi_ref, w1a_ref, w1b_ref, w2_ref, o_ref):
    p = p_ref[...]
    a = jnp.dot((p + pelo_ref[...]).astype(MXU_DTYPE), w1a_ref[...], preferred_element_type=F32)
    b = jnp.dot((p + pehi_ref[...]).astype(MXU_DTYPE), w1b_ref[...], preferred_element_type=F32)
    n = p.shape[0]
    b_next = pltpu.roll(b, n - 1, 0)
    hid = a + b_next
    hid = hid * _sigmoid(hid)
    o_ref[...] = jnp.dot(hid.astype(MXU_DTYPE), w2_ref[...], preferred_element_type=F32)


def _compress(pieces, pe, w1, w2):
    bsz, g, n_piece, width = pieces.shape
    half = CMP_BLOCK // 2
    pe_lo = pe[:half].reshape(1, width)
    pe_hi = pe[half:].reshape(1, width)
    w1a = w1[:half].reshape(width, CMP_HIDDEN).astype(MXU_DTYPE)
    w1b = w1[half:].reshape(width, CMP_HIDDEN).astype(MXU_DTYPE)
    return pl.pallas_call(
        _compress_kernel,
        grid=(bsz, g),
        in_specs=[pl.BlockSpec((None, None, n_piece, width), lambda b, gg: (b, gg, 0, 0)),
                  pl.BlockSpec((1, width), lambda b, gg: (0, 0)),
                  pl.BlockSpec((1, width), lambda b, gg: (0, 0)),
                  pl.BlockSpec((width, CMP_HIDDEN), lambda b, gg: (0, 0)),
                  pl.BlockSpec((width, CMP_HIDDEN), lambda b, gg: (0, 0)),
                  pl.BlockSpec((CMP_HIDDEN, HEAD_DIM), lambda b, gg: (0, 0))],
        out_specs=pl.BlockSpec((None, None, n_piece, HEAD_DIM), lambda b, gg: (b, gg, 0, 0)),
        out_shape=jax.ShapeDtypeStruct((bsz, g, n_piece, HEAD_DIM), F32),
        compiler_params=_cparams(("parallel", "parallel")),
        name="nsa_compress",
    )(pieces, pe_lo, pe_hi, w1a, w1b, w2.astype(MXU_DTYPE))


ROW_BLK = 32
M_INIT = -3e38
NSA_PROLOGUE_PHASES = 5
NSA_PHASE_LAG = 2


def _masked_softmax(s, mask):
    s = jnp.where(mask, s, NEG)
    m = jnp.max(s, axis=-1, keepdims=True)
    e = jnp.where(mask, jnp.exp(s - m), 0.0)
    return e / jnp.maximum(jnp.sum(e, axis=-1, keepdims=True), 1e-30)


def _nt_dot(a, b, precision=None):
    return lax.dot_general(a, b, (((1,), (1,)), ((), ())), preferred_element_type=F32, precision=precision)


def _nsa_kernel(*refs, tq, ck, nt):
    n_in = 12
    shared, scratch = refs[:n_in], refs[n_in:]
    per_tile = len(scratch) // nt
    tiles = [_nsa_tile(u, *shared, *scratch[u * per_tile:(u + 1) * per_tile], tq=tq, ck=ck, nt=nt)
             for u in range(nt)]
    loops = [None] * nt
    for step in range(NSA_PROLOGUE_PHASES + NSA_PHASE_LAG * (nt - 1)):
        for u in range(nt):
            if 0 <= step - NSA_PHASE_LAG * u < NSA_PROLOGUE_PHASES:
                loops[u] = next(tiles[u])

    def pair_body(i, carry):
        for _, body in loops:
            body(i)
        return carry

    lax.fori_loop(0, loops[0][0], pair_body, 0)
    for t in tiles:
        next(t, None)


def _nsa_tile(u, q_ref, kc_ref, vct_ref, ks_ref, vst_ref, kw_ref, vwt_ref, gate_ref, mapt_ref, bt_ref, bandt_ref,
              o_ref, sc_scr, pc_scr, sw_scr, pw_scr, sa_scr, sb_scr, pa_scr, pb_scr, m_scr, ala_scr, alb_scr, acc_scr,
              mw_scr, al_scr, accw_scr, rank_scr, *, tq, ck, nt):
    hpg = HEADS_PER_GROUP
    dh = HEAD_DIM
    aug = 2 * dh
    cols = hpg * tq
    n_blk = mapt_ref.shape[0]
    q0 = (pl.program_id(2) * nt + u) * tq
    tile_rows = pl.ds(u * tq, tq)
    qp = (q_ref[:, tile_rows, :] * np.float32(dh ** -0.5).astype(q_ref.dtype)).reshape(cols, aug)
    eye4 = ((lax.broadcasted_iota(jnp.int32, (cols, tq), 0) & (tq - 1))
            == lax.broadcasted_iota(jnp.int32, (cols, tq), 1)).astype(MXU_DTYPE)
    t_lane = q0 + lax.broadcasted_iota(jnp.int32, (1, tq), 1)

    def online(s_ref, p_ref, kr, m_ref, acc_ref, v_t):
        for h in range(hpg):
            cb = pl.ds(h * tq, tq)
            s = s_ref[kr, cb]
            m_old = m_ref[:, cb]
            m_new = jnp.maximum(m_old, jnp.max(s, axis=0, keepdims=True))
            p_ref[kr, cb] = jnp.exp(s - m_new).astype(p_ref.dtype)
            al_scr[:, cb] = jnp.exp(m_old - m_new)
            m_ref[:, cb] = m_new
        acc_ref[...] = al_scr[...] * acc_ref[...] + jnp.dot(v_t, p_ref[kr, :], preferred_element_type=F32)

    def finish(acc):
        return acc[:dh] / jnp.maximum(acc[dh:dh + 1], 1e-30)

    n_cmp = kc_ref.shape[0]
    sc_scr[...] = _nt_dot(kc_ref[...].astype(MXU_DTYPE), qp)
    wlen = WINDOW + tq
    w0 = pl.multiple_of(jnp.maximum(q0 - WINDOW, 0), tq)
    lhs_win = jnp.concatenate([qp, eye4], axis=1)
    rhs_win = jnp.concatenate([kw_ref[pl.ds(w0, wlen), :], bandt_ref[:, pl.ds(pl.multiple_of(q0 - w0, tq), tq)]],
                              axis=1)
    sw_scr[...] = _nt_dot(rhs_win, lhs_win)
    yield None

    cmask = (lax.broadcasted_iota(jnp.int32, (n_cmp, 1), 0) * CMP_STRIDE + (CMP_BLOCK - 1)) <= t_lane
    psum = jnp.zeros((n_cmp, tq), F32)
    for h in range(hpg):
        cb = pl.ds(h * tq, tq)
        s = jnp.where(cmask, sc_scr[:, cb], NEG)
        e = jnp.where(cmask, jnp.exp(s - jnp.max(s, axis=0, keepdims=True)), 0.0)
        p = e / jnp.maximum(jnp.sum(e, axis=0, keepdims=True), 1e-30)
        pc_scr[:, cb] = p.astype(pc_scr.dtype)
        psum = psum + p
    o_cmp = jnp.dot(vct_ref[...].astype(MXU_DTYPE), pc_scr[...], preferred_element_type=F32)
    yield None

    mw_scr[...] = jnp.full(mw_scr.shape, M_INIT, F32)
    accw_scr[...] = jnp.zeros(accw_scr.shape, F32)
    for k_lo in range(0, wlen, ck):
        nk = min(ck, wlen - k_lo)
        online(sw_scr, pw_scr, pl.ds(k_lo, nk), mw_scr, accw_scr,
               vwt_ref[:, pl.ds(pl.multiple_of(w0 + k_lo, tq), nk)])
    o_win = finish(accw_scr[...])
    yield None

    imp_t = jnp.dot(mapt_ref[...], psum, preferred_element_type=F32, precision=lax.Precision.HIGHEST)
    j_t = lax.broadcasted_iota(jnp.int32, (n_blk, tq), 0)
    cur_t = (q0 + lax.broadcasted_iota(jnp.int32, (n_blk, tq), 1)) >> 6
    forced = (j_t == 0) | (j_t == cur_t) | (j_t == cur_t - 1)
    score_t = jnp.where(j_t <= cur_t, jnp.where(forced, FORCE, imp_t), NEG)
    rank_scr[...] = score_t
    sel_rows = []
    for v in range(n_blk // 8):
        sj = score_t[8 * v:8 * v + 8]
        jv = j_t[8 * v:8 * v + 8]
        rank = jnp.zeros((8, tq), jnp.int32)
        for i in range(n_blk):
            ci = rank_scr[pl.ds(i, 1), :]
            if i < 8 * v:
                ahead = ci >= sj
            elif i > 8 * v + 7:
                ahead = ci > sj
            else:
                ahead = (ci > sj) | ((ci == sj) & (jv > i))
            rank = rank + ahead.astype(jnp.int32)
        sel_rows.append(jnp.where(rank < SLC_TOP_N, 0.0, NEG))
    yield None
    selb_t = jnp.concatenate([jnp.zeros((dh, tq), F32)] + sel_rows, axis=0).astype(MXU_DTYPE)
    selb = _nt_dot(eye4[:tq], selb_t).astype(MXU_DTYPE)
    lhs_slc = jnp.concatenate([(qp.reshape(hpg, tq, aug) + selb[None]).reshape(cols, aug), eye4], axis=1)

    n_full = q0 // ck

    def scores(c, s_ref):
        k0 = pl.multiple_of(jnp.minimum(c, n_full) * ck, ck)
        start = pl.multiple_of(jnp.clip(q0 - c * ck, -tq, ck) + tq, tq)
        rhs = jnp.concatenate([ks_ref[pl.ds(k0, ck), :], bt_ref[:, pl.ds(start, tq)]], axis=1)
        s_ref[...] = _nt_dot(rhs, lhs_slc)

    def softmax(s_ref, p_ref, a_ref):
        for h in range(hpg):
            cb = pl.ds(h * tq, tq)
            s = s_ref[:, cb]
            m_old = m_scr[:, cb]
            m_new = jnp.maximum(m_old, jnp.max(s, axis=0, keepdims=True))
            p_ref[:, cb] = jnp.exp(s - m_new).astype(p_ref.dtype)
            a_ref[:, cb] = jnp.exp(m_old - m_new)
            m_scr[:, cb] = m_new

    def accumulate(c, p_ref, a_ref):
        k0 = pl.multiple_of(jnp.clip(c, 0, n_full) * ck, ck)
        acc_scr[...] = a_ref[...] * acc_scr[...] + jnp.dot(vst_ref[:, pl.ds(k0, ck)], p_ref[...],
                                                           preferred_element_type=F32)

    m_scr[...] = jnp.full(m_scr.shape, M_INIT, F32)
    acc_scr[...] = jnp.zeros(acc_scr.shape, F32)
    pb_scr[...] = jnp.zeros(pb_scr.shape, pb_scr.dtype)
    alb_scr[...] = jnp.ones(alb_scr.shape, F32)
    scores(0, sa_scr)

    def pair_body(i):
        k = 2 * i
        scores(k + 1, sb_scr)
        softmax(sa_scr, pa_scr, ala_scr)
        accumulate(k - 1, pb_scr, alb_scr)
        scores(k + 2, sa_scr)
        softmax(sb_scr, pb_scr, alb_scr)
        accumulate(k, pa_scr, ala_scr)

    n_pairs = (n_full + 2) // 2
    yield n_pairs, pair_body
    accumulate(2 * n_pairs - 1, pb_scr, alb_scr)
    o_slc = finish(acc_scr[...])

    gl = _sigmoid(gate_ref[:, tile_rows])
    outs = []
    for h in range(hpg):
        cb = slice(h * tq, (h + 1) * tq)
        outs.append(gl[3 * h:3 * h + 1] * o_cmp[:, cb] + gl[3 * h + 1:3 * h + 2] * o_slc[:, cb]
                    + gl[3 * h + 2:3 * h + 3] * o_win[:, cb])
    o_t = jnp.concatenate(outs, axis=0).astype(MXU_DTYPE)
    o_ref[tile_rows, :] = _nt_dot(eye4[:tq], o_t).astype(o_ref.dtype)


def _slc_map_t(n_piece, n_slc, n_blk):
    r = CMP_BLOCK // CMP_STRIDE
    a = SLC_BLOCK // CMP_STRIDE
    m = np.zeros((n_blk, n_piece), np.float32)
    for n in range(n_piece - r + 1):
        for i in range(r):
            m[(n + i) // a, n] += 1.0
    return m


def _nsa_attention(rows_op, cols_op, kc, vc, gate_logits):
    bsz, _, seq, aug = rows_op.shape
    g, hpg, dh = N_KV_GROUPS, HEADS_PER_GROUP, HEAD_DIM
    n_piece = kc.shape[2]
    n_slc = seq // SLC_BLOCK
    n_blk = dh
    assert n_slc <= n_blk
    tq = 128
    ck = 256
    wlen = WINDOW + tq
    cols = hpg * tq

    kc_p = jnp.pad(kc, ((0, 0), (0, 0), (0, 0), (0, dh)))
    vc_t = vc.astype(MXU_DTYPE).transpose(0, 1, 3, 2)
    gate_t = gate_logits.reshape(bsz, seq, g, GATE_PAD)[..., :GATE_ROWS].transpose(0, 2, 3, 1)

    kk = np.arange(ck)[:, None]
    bt = np.where(kk <= np.arange(-tq, ck + tq)[None, :], 0.0, NEG).astype(np.float32)
    kw_i = np.arange(wlen)[:, None]
    tau = np.arange(wlen)[None, :]
    bandt = np.where((kw_i <= tau) & (kw_i > tau - WINDOW), 0.0, NEG).astype(np.float32)

    nt = ck // tq
    kern = functools.partial(_nsa_kernel, tq=tq, ck=ck, nt=nt)
    per_group = lambda r, c, first=0: pl.BlockSpec((None, None, r, c), lambda b, gg, i: (b, first + gg, 0, 0))
    const = lambda shape: pl.BlockSpec(shape, lambda b, gg, i: (0, 0))
    per_tile = pltpu.VMEM
    return pl.pallas_call(
        kern,
        grid=(bsz, g, seq // (nt * tq)),
        in_specs=[pl.BlockSpec((None, hpg, nt * tq, aug), lambda b, gg, i: (b, gg, i, 0)),
                  per_group(n_piece, aug), per_group(dh, n_piece),
                  per_group(seq, aug, g * hpg), per_group(aug, seq),
                  per_group(seq, aug, g * hpg + g), per_group(aug, seq, g),
                  pl.BlockSpec((None, None, GATE_ROWS, nt * tq), lambda b, gg, i: (b, gg, 0, i)),
                  const((n_blk, n_piece)), const((ck, ck + 2 * tq)), const((wlen, wlen))],
        out_specs=pl.BlockSpec((None, nt * tq, hpg * dh), lambda b, gg, i: (b, i, gg)),
        out_shape=jax.ShapeDtypeStruct((bsz, seq, g * hpg * dh), MXU_DTYPE),
        scratch_shapes=[per_tile((n_piece, cols), F32),
                        per_tile((n_piece, cols), MXU_DTYPE),
                        per_tile((wlen, cols), F32),
                        per_tile((wlen, cols), MXU_DTYPE),
                        per_tile((ck, cols), F32),
                        per_tile((ck, cols), F32),
                        per_tile((ck, cols), MXU_DTYPE),
                        per_tile((ck, cols), MXU_DTYPE),
                        per_tile((1, cols), F32),
                        per_tile((1, cols), F32),
                        per_tile((1, cols), F32),
                        per_tile((aug, cols), F32),
                        per_tile((1, cols), F32),
                        per_tile((1, cols), F32),
                        per_tile((aug, cols), F32),
                        per_tile((n_blk, tq), F32)] * nt,
        compiler_params=_cparams(("parallel", "parallel", "arbitrary")),
        name="nsa_attention",
    )(rows_op, kc_p, vc_t, rows_op, cols_op, rows_op, cols_op, gate_t,
      jnp.asarray(_slc_map_t(n_piece, n_slc, n_blk)), jnp.asarray(bt, MXU_DTYPE), jnp.asarray(bandt, MXU_DTYPE))


def _merge_kernel(ha_ref, hb_ref, hc_ref, gl_ref, x_ref, wa_ref, wb_ref, wc_ref, wo_ref, nw_ref, o_ref, mix):
    d = x_ref.shape[1]
    branches = ((ha_ref, wa_ref), (hb_ref, wb_ref), (hc_ref, wc_ref))
    for n, (h_ref, w_ref) in enumerate(branches):
        y = _sigmoid(gl_ref[:, n * d:(n + 1) * d]) * jnp.dot(h_ref[...], w_ref[...], preferred_element_type=F32)
        if n == 0:
            mix[...] = y
        else:
            mix[...] += y
    r = jnp.dot(mix[...].astype(MXU_DTYPE), wo_ref[...], preferred_element_type=F32)
    ms = jnp.mean(r * r, axis=-1, keepdims=True)
    o_ref[...] = x_ref[...] + r * lax.rsqrt(ms + EPS) * nw_ref[...]


def _merge(ha, hb, hc, merge_logits, x2d, w_a, w_b, w_c, w_o, norm_w):
    m, d = x2d.shape
    c = ha.shape[1]
    tm = 256
    row = lambda i: (i, 0)
    resident = lambda shape: pl.BlockSpec(shape, lambda i: (0, 0), pipeline_mode=pl.Buffered(1))
    wc = lambda w: w.astype(MXU_DTYPE)
    return pl.pallas_call(
        _merge_kernel,
        grid=(m // tm,),
        in_specs=[pl.BlockSpec((tm, c), row), pl.BlockSpec((tm, c), row), pl.BlockSpec((tm, c), row),
                  pl.BlockSpec((tm, 3 * d), row), pl.BlockSpec((tm, d), row),
                  resident((c, d)), resident((c, d)), resident((c, d)), resident((d, d)),
                  pl.BlockSpec((1, d), lambda i: (0, 0))],
        out_specs=pl.BlockSpec((tm, d), row),
        out_shape=jax.ShapeDtypeStruct((m, d), F32),
        scratch_shapes=[pltpu.VMEM((tm, d), F32)],
        compiler_params=_cparams(("parallel",)),
        name="merge_out_proj",
    )(ha, hb, hc, merge_logits, x2d, wc(w_a), wc(w_b), wc(w_c), wc(w_o), norm_w.reshape(1, d))


def _ffn_up_kernel(h_ref, halo_ref, wg_ref, wv_ref, cwg_ref, cwv_ref, cbg_ref, cbv_ref, o_ref,
                   hbuf, ugbuf, uvbuf, *, tm, seq):
    i = pl.program_id(0)
    j = pl.program_id(1)

    @pl.when(j == 0)
    def _():
        hbuf[pl.ds(FFN_HALO, tm), :] = h_ref[...]
        first = (i * tm) % seq == 0
        hbuf[pl.ds(0, FFN_HALO), :] = jnp.where(first, jnp.zeros_like(halo_ref[...]), halo_ref[...])

    lhs = hbuf[...]
    ugbuf[...] = jnp.dot(lhs, wg_ref[...], preferred_element_type=F32)
    uvbuf[...] = jnp.dot(lhs, wv_ref[...], preferred_element_type=F32)

    def conv(buf, cw_ref, cb_ref):
        return (cb_ref[...] + cw_ref[pl.ds(2, 1), :] * buf[pl.ds(FFN_HALO, tm), :]
                + cw_ref[pl.ds(1, 1), :] * buf[pl.ds(FFN_HALO - 1, tm), :]
                + cw_ref[pl.ds(0, 1), :] * buf[pl.ds(FFN_HALO - 2, tm), :])

    cg = conv(ugbuf, cwg_ref, cbg_ref)
    cv = conv(uvbuf, cwv_ref, cbv_ref)
    o_ref[...] = (cg * _sigmoid(cg) * cv).astype(o_ref.dtype)


def _ffn_up(h, w_up, conv_w, conv_b, seq):
    m, d = h.shape
    dff = w_up.shape[1] // 2
    tm = min(1024, seq)
    tn = _pick(dff, (512, 256, 128))
    nj = dff // tn
    per = tm // FFN_HALO
    kern = functools.partial(_ffn_up_kernel, tm=tm, seq=seq)
    cb = conv_b.reshape(1, 2 * dff)
    return pl.pallas_call(
        kern,
        grid=(m // tm, nj),
        in_specs=[pl.BlockSpec((tm, d), lambda i, j: (i, 0)),
                  pl.BlockSpec((FFN_HALO, d), lambda i, j: (jnp.maximum(i * per - 1, 0), 0)),
                  pl.BlockSpec((d, tn), lambda i, j: (0, j)),
                  pl.BlockSpec((d, tn), lambda i, j: (0, j + nj)),
                  pl.BlockSpec((3, tn), lambda i, j: (0, j)),
                  pl.BlockSpec((3, tn), lambda i, j: (0, j + nj)),
                  pl.BlockSpec((1, tn), lambda i, j: (0, j)),
                  pl.BlockSpec((1, tn), lambda i, j: (0, j + nj))],
        out_specs=pl.BlockSpec((tm, tn), lambda i, j: (i, j)),
        out_shape=jax.ShapeDtypeStruct((m, dff), MXU_DTYPE),
        scratch_shapes=[pltpu.VMEM((FFN_HALO + tm, d), MXU_DTYPE),
                        pltpu.VMEM((FFN_HALO + tm, tn), F32),
                        pltpu.VMEM((FFN_HALO + tm, tn), F32)],
        compiler_params=_cparams(("parallel", "arbitrary")),
        name="ffn_up_conv_gate",
    )(h, h, w_up, w_up, conv_w, conv_w, cb, cb)


def _ffn_down_kernel(a_ref, w_ref, x_ref, nw_ref, o_ref):
    r = jnp.dot(a_ref[...], w_ref[...], preferred_element_type=F32)
    ms = jnp.mean(r * r, axis=-1, keepdims=True)
    o_ref[...] = x_ref[...] + r * lax.rsqrt(ms + EPS) * nw_ref[...]


def _ffn_down(act, w_down, x2d, norm_w):
    m, dff = act.shape
    d = w_down.shape[1]
    tm = 256
    return pl.pallas_call(
        _ffn_down_kernel,
        grid=(m // tm,),
        in_specs=[pl.BlockSpec((tm, dff), lambda i: (i, 0)),
                  pl.BlockSpec((dff, d), lambda i: (0, 0), pipeline_mode=pl.Buffered(1)),
                  pl.BlockSpec((tm, d), lambda i: (i, 0)),
                  pl.BlockSpec((1, d), lambda i: (0, 0))],
        out_specs=pl.BlockSpec((tm, d), lambda i: (i, 0)),
        out_shape=jax.ShapeDtypeStruct((m, d), F32),
        compiler_params=_cparams(("parallel",)),
        name="ffn_down_norm_res",
    )(act, w_down, x2d, norm_w.reshape(1, d))


def _gate_weight(w_gate):
    d = w_gate.shape[0]
    per = HEADS_PER_GROUP * 3
    w = w_gate.reshape(d, N_KV_GROUPS, per)
    w = jnp.pad(w, ((0, 0), (0, 0), (0, GATE_PAD - per)))
    return w.reshape(d, N_KV_GROUPS * GATE_PAD)


def _mixer_layer(x, p, l):
    bsz, seq, d = x.shape
    m = bsz * seq
    x2d = x.reshape(m, d)
    g, hpg, dh = N_KV_GROUPS, HEADS_PER_GROUP, HEAD_DIM
    a_in_w = 2 * CONV_WIDTH
    b_in_w = 2 * SG_WIDTH
    q_w = N_HEADS * dh
    kv_w = 6 * g * dh
    gate_w = 3 * N_HEADS
    o0 = 0
    o1 = o0 + a_in_w
    o2 = o1 + b_in_w
    o3 = o2 + q_w
    o4 = o3 + kv_w
    o5 = o4 + gate_w
    w_in = p["w_in"][l]

    h = _rmsnorm_cast(x2d, p["norm_mix_pre"][l])
    wc = lambda w: w.astype(MXU_DTYPE)
    a_in = _matmul(h, wc(w_in[:, o0:o1]), name="proj_a")
    b_in = _matmul(h, wc(w_in[:, o1:o2]), name="proj_b")
    h3 = h.reshape(bsz, seq, d)
    aug = 2 * dh
    w_q = w_in[:, o2:o3].reshape(d, N_HEADS, dh)
    w_kv = w_in[:, o3:o4].reshape(d, 6, g, dh)
    pad_lanes = lambda w: jnp.pad(w, ((0, 0), (0, 0), (0, aug - dh))).reshape(d, -1)
    w_rows = wc(jnp.concatenate([pad_lanes(w_q), pad_lanes(w_kv[:, 2]), pad_lanes(w_kv[:, 4])], axis=1))
    onehot = np.zeros((seq, aug), np.float32)
    onehot[np.arange(seq), dh + np.arange(seq) // SLC_BLOCK] = 1.0
    slc_slabs = (N_HEADS * aug // 256, (N_HEADS + g) * aug // 256)
    rows_op = _proj_planes(h3, w_rows, jnp.asarray(onehot), MXU_DTYPE, aug, slc_slabs, "proj_attn_rows")
    w_cmp = wc(jnp.concatenate([w_kv[:, 0].reshape(d, g * dh), w_kv[:, 1].reshape(d, g * dh)], axis=1))
    cmp_op = _proj_planes(h3, w_cmp, jnp.zeros((seq, dh), F32), F32, dh, (0, 0), "proj_attn_cmp")
    pad_rows = lambda w: jnp.pad(w.transpose(1, 2, 0), ((0, 0), (0, aug - dh), (0, 0))).reshape(-1, d)
    w_cols = wc(jnp.concatenate([pad_rows(w_kv[:, 3]), pad_rows(w_kv[:, 5])], axis=0))
    ones_rows = np.zeros((2 * g, aug, min(512, seq)), np.float32)
    ones_rows[:, dh] = 1.0
    cols_op = _proj_cols(h3, w_cols, jnp.asarray(ones_rows.reshape(2 * g * aug, -1)), 2 * g)
    gate_logits = _matmul(h, wc(_gate_weight(w_in[:, o4:o5])), name="proj_gate")
    merge_logits = _matmul(h, wc(w_in[:, o5:]), name="proj_merge")

    h_a = _conformer(a_in.reshape(bsz, seq, a_in_w), p["conv_a_w"][l], p["conv_a_b"][l],
                     p["ln_a_g"][l], p["ln_a_b"][l]).reshape(m, CONV_WIDTH)
    h_b = _gmlp(b_in, p["ln_b_g"][l], p["ln_b_b"][l], p["sg_w"][l], p["sg_b"][l])

    n_piece = seq // CMP_STRIDE
    kc = _compress(cmp_op[:, :g].reshape(bsz, g, n_piece, CMP_STRIDE * dh), p["cmp_pe_k"][l], p["cmp_w1_k"][l],
                   p["cmp_w2_k"][l])
    vc = _compress(cmp_op[:, g:].reshape(bsz, g, n_piece, CMP_STRIDE * dh), p["cmp_pe_v"][l], p["cmp_w1_v"][l],
                   p["cmp_w2_v"][l])
    o = _nsa_attention(rows_op, cols_op, kc, vc, gate_logits.reshape(bsz, seq, g * GATE_PAD))
    h_c = o.reshape(m, N_HEADS * dh)

    return _merge(h_a, h_b, h_c, merge_logits, x2d, p["w_a_out"][l], p["w_b_out"][l], p["w_c_out"][l], p["w_o"][l],
                  p["norm_mix_post"][l]).reshape(bsz, seq, d)


def _ffn_layer(x, p, l):
    bsz, seq, d = x.shape
    m = bsz * seq
    x2d = x.reshape(m, d)
    h = _rmsnorm_cast(x2d, p["norm_ffn_pre"][l])
    act = _ffn_up(h, p["w_up"][l].astype(MXU_DTYPE), p["ffn_conv_w"][l], p["ffn_conv_b"][l], seq)
    return _ffn_down(act, p["w_down"][l].astype(MXU_DTYPE), x2d, p["norm_ffn_post"][l]).reshape(bsz, seq, d)


def kernel(x, norm_mix_pre, norm_mix_post, norm_ffn_pre, norm_ffn_post, w_in, conv_a_w, conv_a_b, ln_a_g, ln_a_b,
           w_a_out, ln_b_g, ln_b_b, sg_w, sg_b, w_b_out, cmp_pe_k, cmp_w1_k, cmp_w2_k, cmp_pe_v, cmp_w1_v,
           cmp_w2_v, w_c_out, w_o, w_up, ffn_conv_w, ffn_conv_b, w_down):
    p = dict(norm_mix_pre=norm_mix_pre, norm_mix_post=norm_mix_post, norm_ffn_pre=norm_ffn_pre,
             norm_ffn_post=norm_ffn_post, w_in=w_in, conv_a_w=conv_a_w, conv_a_b=conv_a_b, ln_a_g=ln_a_g,
             ln_a_b=ln_a_b, w_a_out=w_a_out, ln_b_g=ln_b_g, ln_b_b=ln_b_b, sg_w=sg_w, sg_b=sg_b, w_b_out=w_b_out,
             cmp_pe_k=cmp_pe_k, cmp_w1_k=cmp_w1_k, cmp_w2_k=cmp_w2_k, cmp_pe_v=cmp_pe_v, cmp_w1_v=cmp_w1_v,
             cmp_w2_v=cmp_w2_v, w_c_out=w_c_out, w_o=w_o, w_up=w_up, ffn_conv_w=ffn_conv_w,
             ffn_conv_b=ffn_conv_b, w_down=w_down)
    for l in range(w_in.shape[0]):
        x = _mixer_layer(x, p, l)
        x = _ffn_layer(x, p, l)
    return x
```

```python
import functools

import numpy as np
import jax
import jax.numpy as jnp
from jax import lax
from jax.experimental import pallas as pl
from jax.experimental.pallas import tpu as pltpu

F32 = jnp.float32
MXU_DTYPE = jnp.bfloat16

EPS = 1e-6
NEG = -1e30
FORCE = 1e4

CONV_WIDTH = 1024
DW_CONV_SIZE = 31
CONV_HALO = 32
CONV_SUB = 64
SG_WIDTH = 1024
SG_CHUNK = 128
SG_GROUPS = 8
N_HEADS = 16
N_KV_GROUPS = 4
HEADS_PER_GROUP = 4
HEAD_DIM = 64
CMP_BLOCK = 32
CMP_STRIDE = 16
CMP_HIDDEN = 256
SLC_BLOCK = 64
SLC_TOP_N = 16
WINDOW = 512
D_FF = 5632
FFN_HALO = 16
GATE_PAD = 128
PROJ_SLAB = 512
GATE_ROWS = 16

VMEM_LIMIT = 56 * 1024 * 1024


def _cparams(sem):
    return pltpu.CompilerParams(dimension_semantics=sem, vmem_limit_bytes=VMEM_LIMIT)


def _sigmoid(x):
    return jax.nn.sigmoid(x)


def _rmsnorm_kernel(x_ref, w_ref, o_ref):
    x = x_ref[...]
    ms = jnp.mean(x * x, axis=-1, keepdims=True)
    o_ref[...] = (x * lax.rsqrt(ms + EPS) * w_ref[...]).astype(o_ref.dtype)


def _rmsnorm_cast(x2d, w):
    m, d = x2d.shape
    tm = min(512, m)
    return pl.pallas_call(
        _rmsnorm_kernel,
        grid=(m // tm,),
        in_specs=[pl.BlockSpec((tm, d), lambda i: (i, 0)),
                  pl.BlockSpec((1, d), lambda i: (0, 0))],
        out_specs=pl.BlockSpec((tm, d), lambda i: (i, 0)),
        out_shape=jax.ShapeDtypeStruct((m, d), MXU_DTYPE),
        compiler_params=_cparams(("parallel",)),
        name="rmsnorm_cast",
    )(x2d, w.reshape(1, d))


def _mm_kernel(a_ref, w_ref, o_ref):
    o_ref[...] = jnp.dot(a_ref[...], w_ref[...], preferred_element_type=F32).astype(o_ref.dtype)


def _pick(n, cands):
    for c in cands:
        if n % c == 0:
            return c
    raise ValueError(f"no tile for {n}")


def _matmul(a, w, out_dtype=F32, name="matmul"):
    m, k = a.shape
    n = w.shape[1]
    tm = min(1024, m)
    tn = _pick(n, (512, 384, 256, 128))
    return pl.pallas_call(
        _mm_kernel,
        grid=(m // tm, n // tn),
        in_specs=[pl.BlockSpec((tm, k), lambda i, j: (i, 0)),
                  pl.BlockSpec((k, tn), lambda i, j: (0, j))],
        out_specs=pl.BlockSpec((tm, tn), lambda i, j: (i, j)),
        out_shape=jax.ShapeDtypeStruct((m, n), out_dtype),
        compiler_params=_cparams(("parallel", "arbitrary")),
        name=name,
    )(a, w)


def _proj_planes_kernel(h_ref, w_ref, add_ref, o_ref, *, add_lo, add_hi, width):
    j = pl.program_id(2)
    res = jnp.dot(h_ref[...], w_ref[...], preferred_element_type=F32)
    if add_hi > add_lo:
        add = jnp.where((j >= add_lo) & (j < add_hi), add_ref[...], 0.0)
    for p in range(o_ref.shape[0]):
        piece = res[:, p * width:(p + 1) * width]
        o_ref[p] = (piece + add if add_hi > add_lo else piece).astype(o_ref.dtype)


def _proj_planes(h3, w, add, out_dtype, width, add_range, name):
    bsz, seq, d = h3.shape
    n = w.shape[1]
    slab = PROJ_SLAB
    per = slab // width
    tm = min(1024, seq)
    kern = functools.partial(_proj_planes_kernel, add_lo=add_range[0], add_hi=add_range[1], width=width)
    return pl.pallas_call(
        kern,
        grid=(bsz, seq // tm, n // slab),
        in_specs=[pl.BlockSpec((None, tm, d), lambda b, i, j: (b, i, 0)),
                  pl.BlockSpec((d, slab), lambda b, i, j: (0, j)),
                  pl.BlockSpec((tm, width), lambda b, i, j: (i, 0))],
        out_specs=pl.BlockSpec((None, per, tm, width), lambda b, i, j: (b, j, i, 0)),
        out_shape=jax.ShapeDtypeStruct((bsz, n // width, seq, width), out_dtype),
        compiler_params=_cparams(("parallel", "parallel", "arbitrary")),
        name=name,
    )(h3, w, add)


def _proj_cols_kernel(h_ref, wt_ref, add_ref, o_ref):
    res = lax.dot_general(wt_ref[...], h_ref[...], (((1,), (1,)), ((), ())), preferred_element_type=F32)
    o_ref[...] = (res + add_ref[...]).astype(o_ref.dtype).reshape(o_ref.shape)


def _proj_cols(h3, w_t, add, planes):
    bsz, seq, d = h3.shape
    n = w_t.shape[0]
    tm = min(512, seq)
    return pl.pallas_call(
        _proj_cols_kernel,
        grid=(bsz, seq // tm),
        in_specs=[pl.BlockSpec((None, tm, d), lambda b, i: (b, i, 0)),
                  pl.BlockSpec((n, d), lambda b, i: (0, 0), pipeline_mode=pl.Buffered(1)),
                  pl.BlockSpec((n, tm), lambda b, i: (0, 0), pipeline_mode=pl.Buffered(1))],
        out_specs=pl.BlockSpec((None, planes, n // planes, tm), lambda b, i: (b, 0, 0, i)),
        out_shape=jax.ShapeDtypeStruct((bsz, planes, n // planes, seq), MXU_DTYPE),
        compiler_params=_cparams(("parallel", "parallel")),
        name="proj_attn_cols",
    )(h3, w_t, add)


def _conformer_kernel(cur_ref, halo_ref, cw_ref, cb_ref, lg_ref, lb_ref, o_ref, hbuf, cbuf, *, ts):
    c = CONV_WIDTH
    i = pl.program_id(1)
    cur = cur_ref[...]
    hbuf[pl.ds(CONV_HALO, ts), :] = cur[:, :c] * _sigmoid(cur[:, c:])
    hal = halo_ref[...]
    hh = hal[:, :c] * _sigmoid(hal[:, c:])
    hbuf[pl.ds(0, CONV_HALO), :] = jnp.where(i == 0, 0.0, hh)

    first = CONV_HALO - (DW_CONV_SIZE - 1)
    span = CONV_SUB + CONV_HALO

    def blk_body(idx, carry):
        off = pl.multiple_of((idx // (ts // CONV_SUB)) * 128, 128)
        r0 = pl.multiple_of((idx % (ts // CONV_SUB)) * CONV_SUB, CONV_SUB)
        x = hbuf[pl.ds(r0, span), pl.ds(off, 128)]
        acc = jnp.broadcast_to(cb_ref[:, pl.ds(off, 128)], (CONV_SUB, 128))
        for r in range(8):
            xr = x if r == 0 else pltpu.roll(x, span - r, 0)
            for a in range(CONV_HALO // 8 + 1):
                k = 8 * a + r - first
                if 0 <= k < DW_CONV_SIZE:
                    acc = acc + cw_ref[pl.ds(k, 1), pl.ds(off, 128)] * xr[8 * a:8 * a + CONV_SUB]
        cbuf[pl.ds(r0, CONV_SUB), pl.ds(off, 128)] = acc
        return carry

    lax.fori_loop(0, (c // 128) * (ts // CONV_SUB), blk_body, 0)
    y = cbuf[...]
    mean = jnp.mean(y, axis=-1, keepdims=True)
    yc = y - mean
    var = jnp.mean(yc * yc, axis=-1, keepdims=True)
    z = yc * lax.rsqrt(var + EPS) * lg_ref[...] + lb_ref[...]
    o_ref[...] = (z * _sigmoid(z)).astype(o_ref.dtype)


def _conformer(a_in, conv_w, conv_b, ln_g, ln_b):
    bsz, seq, _ = a_in.shape
    c = CONV_WIDTH
    ts = min(256, seq)
    per = ts // CONV_HALO
    kern = functools.partial(_conformer_kernel, ts=ts)
    return pl.pallas_call(
        kern,
        grid=(bsz, seq // ts),
        in_specs=[pl.BlockSpec((None, ts, 2 * c), lambda b, i: (b, i, 0)),
                  pl.BlockSpec((None, CONV_HALO, 2 * c), lambda b, i: (b, jnp.maximum(i * per - 1, 0), 0)),
                  pl.BlockSpec((DW_CONV_SIZE, c), lambda b, i: (0, 0)),
                  pl.BlockSpec((1, c), lambda b, i: (0, 0)),
                  pl.BlockSpec((1, c), lambda b, i: (0, 0)),
                  pl.BlockSpec((1, c), lambda b, i: (0, 0))],
        out_specs=pl.BlockSpec((None, ts, c), lambda b, i: (b, i, 0)),
        out_shape=jax.ShapeDtypeStruct((bsz, seq, c), MXU_DTYPE),
        scratch_shapes=[pltpu.VMEM((CONV_HALO + ts, c), F32), pltpu.VMEM((ts, c), F32)],
        compiler_params=_cparams(("parallel", "parallel")),
        name="conformer_conv",
    )(a_in, a_in, conv_w, conv_b.reshape(1, c), ln_g.reshape(1, c), ln_b.reshape(1, c))


def _gmlp_kernel(x_ref, lg_ref, lb_ref, sgw_ref, sgbt_ref, o_ref, *, ts):
    c = SG_WIDTH
    gd = c // SG_GROUPS
    x = x_ref[...]
    z = 0.5 * x * (1.0 + lax.erf(x * np.float32(np.sqrt(0.5))))
    u = z[:, :c]
    v = z[:, c:]
    mean = jnp.mean(v, axis=-1, keepdims=True)
    vc = v - mean
    var = jnp.mean(vc * vc, axis=-1, keepdims=True)
    v = vc * lax.rsqrt(var + EPS) * lg_ref[...] + lb_ref[...]
    row = lax.broadcasted_iota(jnp.int32, (SG_CHUNK, SG_CHUNK), 0)
    col = lax.broadcasted_iota(jnp.int32, (SG_CHUNK, SG_CHUNK), 1)
    causal = col <= row
    for g in range(SG_GROUPS):
        w = jnp.where(causal, sgw_ref[g], 0.0).astype(MXU_DTYPE)
        bias = sgbt_ref[:, g:g + 1]
        for ch in range(ts // SG_CHUNK):
            r0 = ch * SG_CHUNK
            vg = v[r0:r0 + SG_CHUNK, g * gd:(g + 1) * gd].astype(MXU_DTYPE)
            f = jnp.dot(w, vg, preferred_element_type=F32) + bias
            o_ref[r0:r0 + SG_CHUNK, g * gd:(g + 1) * gd] = (u[r0:r0 + SG_CHUNK, g * gd:(g + 1) * gd]
                                                            * f).astype(o_ref.dtype)


def _gmlp(b_in, ln_g, ln_b, sg_w, sg_b):
    m, _ = b_in.shape
    c = SG_WIDTH
    ts = 256
    kern = functools.partial(_gmlp_kernel, ts=ts)
    return pl.pallas_call(
        kern,
        grid=(m // ts,),
        in_specs=[pl.BlockSpec((ts, 2 * c), lambda i: (i, 0)),
                  pl.BlockSpec((1, c), lambda i: (0, 0)),
                  pl.BlockSpec((1, c), lambda i: (0, 0)),
                  pl.BlockSpec((SG_GROUPS, SG_CHUNK, SG_CHUNK), lambda i: (0, 0, 0)),
                  pl.BlockSpec((SG_CHUNK, SG_GROUPS), lambda i: (0, 0))],
        out_specs=pl.BlockSpec((ts, c), lambda i: (i, 0)),
        out_shape=jax.ShapeDtypeStruct((m, c), MXU_DTYPE),
        compiler_params=_cparams(("parallel",)),
        name="gmlp_gating",
    )(b_in, ln_g.reshape(1, c), ln_b.reshape(1, c), sg_w, sg_b.T)


def _compress_kernel(p_ref, pelo_ref, pehi_ref, w1a_ref, w1b_ref, w2_ref, o_ref):
    p = p_ref[...]
    a = jnp.dot((p + pelo_ref[...]).astype(MXU_DTYPE), w1a_ref[...], preferred_element_type=F32)
    b = jnp.dot((p + pehi_ref[...]).astype(MXU_DTYPE), w1b_ref[...], preferred_element_type=F32)
    n = p.shape[0]
    b_next = pltpu.roll(b, n - 1, 0)
    hid = a + b_next
    hid = hid * _sigmoid(hid)
    o_ref[...] = jnp.dot(hid.astype(MXU_DTYPE), w2_ref[...], preferred_element_type=F32)


def _compress(pieces, pe, w1, w2):
    bsz, g, n_piece, width = pieces.shape
    half = CMP_BLOCK // 2
    pe_lo = pe[:half].reshape(1, width)
    pe_hi = pe[half:].reshape(1, width)
    w1a = w1[:half].reshape(width, CMP_HIDDEN).astype(MXU_DTYPE)
    w1b = w1[half:].reshape(width, CMP_HIDDEN).astype(MXU_DTYPE)
    return pl.pallas_call(
        _compress_kernel,
        grid=(bsz, g),
        in_specs=[pl.BlockSpec((None, None, n_piece, width), lambda b, gg: (b, gg, 0, 0)),
                  pl.BlockSpec((1, width), lambda b, gg: (0, 0)),
                  pl.BlockSpec((1, width), lambda b, gg: (0, 0)),
                  pl.BlockSpec((width, CMP_HIDDEN), lambda b, gg: (0, 0)),
                  pl.BlockSpec((width, CMP_HIDDEN), lambda b, gg: (0, 0)),
                  pl.BlockSpec((CMP_HIDDEN, HEAD_DIM), lambda b, gg: (0, 0))],
        out_specs=pl.BlockSpec((None, None, n_piece, HEAD_DIM), lambda b, gg: (b, gg, 0, 0)),
        out_shape=jax.ShapeDtypeStruct((bsz, g, n_piece, HEAD_DIM), F32),
        compiler_params=_cparams(("parallel", "parallel")),
        name="nsa_compress",
    )(pieces, pe_lo, pe_hi, w1a, w1b, w2.astype(MXU_DTYPE))


ROW_BLK = 32
M_INIT = -3e38
NSA_PROLOGUE_PHASES = 5
NSA_PHASE_LAG = 2


def _masked_softmax(s, mask):
    s = jnp.where(mask, s, NEG)
    m = jnp.max(s, axis=-1, keepdims=True)
    e = jnp.where(mask, jnp.exp(s - m), 0.0)
    return e / jnp.maximum(jnp.sum(e, axis=-1, keepdims=True), 1e-30)


def _nt_dot(a, b, precision=None):
    return lax.dot_general(a, b, (((1,), (1,)), ((), ())), preferred_element_type=F32, precision=precision)


def _nsa_kernel(*refs, tq, ck, nt):
    n_in = 12
    shared, scratch = refs[:n_in], refs[n_in:]
    per_tile = len(scratch) // nt
    tiles = [_nsa_tile(u, *shared, *scratch[u * per_tile:(u + 1) * per_tile], tq=tq, ck=ck, nt=nt)
             for u in range(nt)]
    loops = [None] * nt
    for step in range(NSA_PROLOGUE_PHASES + NSA_PHASE_LAG * (nt - 1)):
        for u in range(nt):
            if 0 <= step - NSA_PHASE_LAG * u < NSA_PROLOGUE_PHASES:
                loops[u] = next(tiles[u])

    def pair_body(i, carry):
        for stage in zip(*[stages(i) for _, stages in loops]):
            for run in stage:
                run()
        return carry

    lax.fori_loop(0, loops[0][0], pair_body, 0)
    for t in tiles:
        next(t, None)


def _nsa_tile(u, q_ref, kc_ref, vct_ref, ks_ref, vst_ref, kw_ref, vwt_ref, gate_ref, mapt_ref, bt_ref, bandt_ref,
              o_ref, sc_scr, pc_scr, sw_scr, pw_scr, sa_scr, sb_scr, pa_scr, pb_scr, m_scr, ala_scr, alb_scr, acc_scr,
              mw_scr, al_scr, accw_scr, rank_scr, *, tq, ck, nt):
    hpg = HEADS_PER_GROUP
    dh = HEAD_DIM
    aug = 2 * dh
    cols = hpg * tq
    n_blk = mapt_ref.shape[0]
    q0 = (pl.program_id(2) * nt + u) * tq
    tile_rows = pl.ds(u * tq, tq)
    qp = (q_ref[:, tile_rows, :] * np.float32(dh ** -0.5).astype(q_ref.dtype)).reshape(cols, aug)
    eye4 = ((lax.broadcasted_iota(jnp.int32, (cols, tq), 0) & (tq - 1))
            == lax.broadcasted_iota(jnp.int32, (cols, tq), 1)).astype(MXU_DTYPE)
    t_lane = q0 + lax.broadcasted_iota(jnp.int32, (1, tq), 1)

    def online(s_ref, p_ref, kr, m_ref, acc_ref, v_t):
        for h in range(hpg):
            cb = pl.ds(h * tq, tq)
            s = s_ref[kr, cb]
            m_old = m_ref[:, cb]
            m_new = jnp.maximum(m_old, jnp.max(s, axis=0, keepdims=True))
            p_ref[kr, cb] = jnp.exp(s - m_new).astype(p_ref.dtype)
            al_scr[:, cb] = jnp.exp(m_old - m_new)
            m_ref[:, cb] = m_new
        acc_ref[...] = al_scr[...] * acc_ref[...] + jnp.dot(v_t, p_ref[kr, :], preferred_element_type=F32)

    def finish(acc):
        return acc[:dh] / jnp.maximum(acc[dh:dh + 1], 1e-30)

    n_cmp = kc_ref.shape[0]
    sc_scr[...] = _nt_dot(kc_ref[...].astype(MXU_DTYPE), qp)
    wlen = WINDOW + tq
    w0 = pl.multiple_of(jnp.maximum(q0 - WINDOW, 0), tq)
    lhs_win = jnp.concatenate([qp, eye4], axis=1)
    rhs_win = jnp.concatenate([kw_ref[pl.ds(w0, wlen), :], bandt_ref[:, pl.ds(pl.multiple_of(q0 - w0, tq), tq)]],
                              axis=1)
    sw_scr[...] = _nt_dot(rhs_win, lhs_win)
    yield None

    cmask = (lax.broadcasted_iota(jnp.int32, (n_cmp, 1), 0) * CMP_STRIDE + (CMP_BLOCK - 1)) <= t_lane
    psum = jnp.zeros((n_cmp, tq), F32)
    for h in range(hpg):
        cb = pl.ds(h * tq, tq)
        s = jnp.where(cmask, sc_scr[:, cb], NEG)
        e = jnp.where(cmask, jnp.exp(s - jnp.max(s, axis=0, keepdims=True)), 0.0)
        p = e / jnp.maximum(jnp.sum(e, axis=0, keepdims=True), 1e-30)
        pc_scr[:, cb] = p.astype(pc_scr.dtype)
        psum = psum + p
    o_cmp = jnp.dot(vct_ref[...].astype(MXU_DTYPE), pc_scr[...], preferred_element_type=F32)
    yield None

    mw_scr[...] = jnp.full(mw_scr.shape, M_INIT, F32)
    accw_scr[...] = jnp.zeros(accw_scr.shape, F32)
    for k_lo in range(0, wlen, ck):
        nk = min(ck, wlen - k_lo)
        online(sw_scr, pw_scr, pl.ds(k_lo, nk), mw_scr, accw_scr,
               vwt_ref[:, pl.ds(pl.multiple_of(w0 + k_lo, tq), nk)])
    o_win = finish(accw_scr[...])
    yield None

    imp_t = jnp.dot(mapt_ref[...], psum, preferred_element_type=F32, precision=lax.Precision.HIGHEST)
    j_t = lax.broadcasted_iota(jnp.int32, (n_blk, tq), 0)
    cur_t = (q0 + lax.broadcasted_iota(jnp.int32, (n_blk, tq), 1)) >> 6
    forced = (j_t == 0) | (j_t == cur_t) | (j_t == cur_t - 1)
    score_t = jnp.where(j_t <= cur_t, jnp.where(forced, FORCE, imp_t), NEG)
    rank_scr[...] = score_t
    sel_rows = []
    for v in range(n_blk // 8):
        sj = score_t[8 * v:8 * v + 8]
        jv = j_t[8 * v:8 * v + 8]
        rank = jnp.zeros((8, tq), jnp.int32)
        for i in range(n_blk):
            ci = rank_scr[pl.ds(i, 1), :]
            if i < 8 * v:
                ahead = ci >= sj
            elif i > 8 * v + 7:
                ahead = ci > sj
            else:
                ahead = (ci > sj) | ((ci == sj) & (jv > i))
            rank = rank + ahead.astype(jnp.int32)
        sel_rows.append(jnp.where(rank < SLC_TOP_N, 0.0, NEG))
    yield None
    selb_t = jnp.concatenate([jnp.zeros((dh, tq), F32)] + sel_rows, axis=0).astype(MXU_DTYPE)
    selb = _nt_dot(eye4[:tq], selb_t).astype(MXU_DTYPE)
    lhs_slc = jnp.concatenate([(qp.reshape(hpg, tq, aug) + selb[None]).reshape(cols, aug), eye4], axis=1)

    n_full = q0 // ck

    def scores(c, s_ref):
        k0 = pl.multiple_of(jnp.minimum(c, n_full) * ck, ck)
        start = pl.multiple_of(jnp.clip(q0 - c * ck, -tq, ck) + tq, tq)
        rhs = jnp.concatenate([ks_ref[pl.ds(k0, ck), :], bt_ref[:, pl.ds(start, tq)]], axis=1)
        s_ref[...] = _nt_dot(rhs, lhs_slc)

    def softmax(s_ref, p_ref, a_ref):
        for h in range(hpg):
            cb = pl.ds(h * tq, tq)
            s = s_ref[:, cb]
            m_old = m_scr[:, cb]
            m_new = jnp.maximum(m_old, jnp.max(s, axis=0, keepdims=True))
            p_ref[:, cb] = jnp.exp(s - m_new).astype(p_ref.dtype)
            a_ref[:, cb] = jnp.exp(m_old - m_new)
            m_scr[:, cb] = m_new

    def accumulate(c, p_ref, a_ref):
        k0 = pl.multiple_of(jnp.clip(c, 0, n_full) * ck, ck)
        acc_scr[...] = a_ref[...] * acc_scr[...] + jnp.dot(vst_ref[:, pl.ds(k0, ck)], p_ref[...],
                                                           preferred_element_type=F32)

    m_scr[...] = jnp.full(m_scr.shape, M_INIT, F32)
    acc_scr[...] = jnp.zeros(acc_scr.shape, F32)
    pb_scr[...] = jnp.zeros(pb_scr.shape, pb_scr.dtype)
    alb_scr[...] = jnp.ones(alb_scr.shape, F32)
    scores(0, sa_scr)

    def pair_stages(i):
        k = 2 * i
        return (lambda: softmax(sa_scr, pa_scr, ala_scr),
                lambda: accumulate(k - 1, pb_scr, alb_scr),
                lambda: scores(k + 1, sb_scr),
                lambda: softmax(sb_scr, pb_scr, alb_scr),
                lambda: accumulate(k, pa_scr, ala_scr),
                lambda: scores(k + 2, sa_scr))

    n_pairs = (n_full + 2) // 2
    yield n_pairs, pair_stages
    accumulate(2 * n_pairs - 1, pb_scr, alb_scr)
    o_slc = finish(acc_scr[...])

    gl = _sigmoid(gate_ref[:, tile_rows])
    outs = []
    for h in range(hpg):
        cb = slice(h * tq, (h + 1) * tq)
        outs.append(gl[3 * h:3 * h + 1] * o_cmp[:, cb] + gl[3 * h + 1:3 * h + 2] * o_slc[:, cb]
                    + gl[3 * h + 2:3 * h + 3] * o_win[:, cb])
    o_t = jnp.concatenate(outs, axis=0).astype(MXU_DTYPE)
    o_ref[tile_rows, :] = _nt_dot(eye4[:tq], o_t).astype(o_ref.dtype)


def _slc_map_t(n_piece, n_slc, n_blk):
    r = CMP_BLOCK // CMP_STRIDE
    a = SLC_BLOCK // CMP_STRIDE
    m = np.zeros((n_blk, n_piece), np.float32)
    for n in range(n_piece - r + 1):
        for i in range(r):
            m[(n + i) // a, n] += 1.0
    return m


def _nsa_attention(rows_op, cols_op, kc, vc, gate_logits):
    bsz, _, seq, aug = rows_op.shape
    g, hpg, dh = N_KV_GROUPS, HEADS_PER_GROUP, HEAD_DIM
    n_piece = kc.shape[2]
    n_slc = seq // SLC_BLOCK
    n_blk = dh
    assert n_slc <= n_blk
    tq = 128
    ck = 256
    wlen = WINDOW + tq
    cols = hpg * tq

    kc_p = jnp.pad(kc, ((0, 0), (0, 0), (0, 0), (0, dh)))
    vc_t = vc.astype(MXU_DTYPE).transpose(0, 1, 3, 2)
    gate_t = gate_logits.reshape(bsz, seq, g, GATE_PAD)[..., :GATE_ROWS].transpose(0, 2, 3, 1)

    kk = np.arange(ck)[:, None]
    bt = np.where(kk <= np.arange(-tq, ck + tq)[None, :], 0.0, NEG).astype(np.float32)
    kw_i = np.arange(wlen)[:, None]
    tau = np.arange(wlen)[None, :]
    bandt = np.where((kw_i <= tau) & (kw_i > tau - WINDOW), 0.0, NEG).astype(np.float32)

    nt = ck // tq
    kern = functools.partial(_nsa_kernel, tq=tq, ck=ck, nt=nt)
    per_group = lambda r, c, first=0: pl.BlockSpec((None, None, r, c), lambda b, gg, i: (b, first + gg, 0, 0))
    const = lambda shape: pl.BlockSpec(shape, lambda b, gg, i: (0, 0))
    per_tile = pltpu.VMEM
    return pl.pallas_call(
        kern,
        grid=(bsz, g, seq // (nt * tq)),
        in_specs=[pl.BlockSpec((None, hpg, nt * tq, aug), lambda b, gg, i: (b, gg, i, 0)),
                  per_group(n_piece, aug), per_group(dh, n_piece),
                  per_group(seq, aug, g * hpg), per_group(aug, seq),
                  per_group(seq, aug, g * hpg + g), per_group(aug, seq, g),
                  pl.BlockSpec((None, None, GATE_ROWS, nt * tq), lambda b, gg, i: (b, gg, 0, i)),
                  const((n_blk, n_piece)), const((ck, ck + 2 * tq)), const((wlen, wlen))],
        out_specs=pl.BlockSpec((None, nt * tq, hpg * dh), lambda b, gg, i: (b, i, gg)),
        out_shape=jax.ShapeDtypeStruct((bsz, seq, g * hpg * dh), MXU_DTYPE),
        scratch_shapes=[per_tile((n_piece, cols), F32),
                        per_tile((n_piece, cols), MXU_DTYPE),
                        per_tile((wlen, cols), F32),
                        per_tile((wlen, cols), MXU_DTYPE),
                        per_tile((ck, cols), F32),
                        per_tile((ck, cols), F32),
                        per_tile((ck, cols), MXU_DTYPE),
                        per_tile((ck, cols), MXU_DTYPE),
                        per_tile((1, cols), F32),
                        per_tile((1, cols), F32),
                        per_tile((1, cols), F32),
                        per_tile((aug, cols), F32),
                        per_tile((1, cols), F32),
                        per_tile((1, cols), F32),
                        per_tile((aug, cols), F32),
                        per_tile((n_blk, tq), F32)] * nt,
        compiler_params=_cparams(("parallel", "parallel", "arbitrary")),
        name="nsa_attention",
    )(rows_op, kc_p, vc_t, rows_op, cols_op, rows_op, cols_op, gate_t,
      jnp.asarray(_slc_map_t(n_piece, n_slc, n_blk)), jnp.asarray(bt, MXU_DTYPE), jnp.asarray(bandt, MXU_DTYPE))


def _merge_kernel(ha_ref, hb_ref, hc_ref, gl_ref, x_ref, wa_ref, wb_ref, wc_ref, wo_ref, nw_ref, o_ref, mix):
    d = x_ref.shape[1]
    branches = ((ha_ref, wa_ref), (hb_ref, wb_ref), (hc_ref, wc_ref))
    for n, (h_ref, w_ref) in enumerate(branches):
        y = _sigmoid(gl_ref[:, n * d:(n + 1) * d]) * jnp.dot(h_ref[...], w_ref[...], preferred_element_type=F32)
        if n == 0:
            mix[...] = y
        else:
            mix[...] += y
    r = jnp.dot(mix[...].astype(MXU_DTYPE), wo_ref[...], preferred_element_type=F32)
    ms = jnp.mean(r * r, axis=-1, keepdims=True)
    o_ref[...] = x_ref[...] + r * lax.rsqrt(ms + EPS) * nw_ref[...]


def _merge(ha, hb, hc, merge_logits, x2d, w_a, w_b, w_c, w_o, norm_w):
    m, d = x2d.shape
    c = ha.shape[1]
    tm = 256
    row = lambda i: (i, 0)
    resident = lambda shape: pl.BlockSpec(shape, lambda i: (0, 0), pipeline_mode=pl.Buffered(1))
    wc = lambda w: w.astype(MXU_DTYPE)
    return pl.pallas_call(
        _merge_kernel,
        grid=(m // tm,),
        in_specs=[pl.BlockSpec((tm, c), row), pl.BlockSpec((tm, c), row), pl.BlockSpec((tm, c), row),
                  pl.BlockSpec((tm, 3 * d), row), pl.BlockSpec((tm, d), row),
                  resident((c, d)), resident((c, d)), resident((c, d)), resident((d, d)),
                  pl.BlockSpec((1, d), lambda i: (0, 0))],
        out_specs=pl.BlockSpec((tm, d), row),
        out_shape=jax.ShapeDtypeStruct((m, d), F32),
        scratch_shapes=[pltpu.VMEM((tm, d), F32)],
        compiler_params=_cparams(("parallel",)),
        name="merge_out_proj",
    )(ha, hb, hc, merge_logits, x2d, wc(w_a), wc(w_b), wc(w_c), wc(w_o), norm_w.reshape(1, d))


def _ffn_up_kernel(h_ref, halo_ref, wg_ref, wv_ref, cwg_ref, cwv_ref, cbg_ref, cbv_ref, o_ref,
                   hbuf, ugbuf, uvbuf, *, tm, seq):
    i = pl.program_id(0)
    j = pl.program_id(1)

    @pl.when(j == 0)
    def _():
        hbuf[pl.ds(FFN_HALO, tm), :] = h_ref[...]
        first = (i * tm) % seq == 0
        hbuf[pl.ds(0, FFN_HALO), :] = jnp.where(first, jnp.zeros_like(halo_ref[...]), halo_ref[...])

    lhs = hbuf[...]
    ugbuf[...] = jnp.dot(lhs, wg_ref[...], preferred_element_type=F32)
    uvbuf[...] = jnp.dot(lhs, wv_ref[...], preferred_element_type=F32)

    def conv(buf, cw_ref, cb_ref):
        return (cb_ref[...] + cw_ref[pl.ds(2, 1), :] * buf[pl.ds(FFN_HALO, tm), :]
                + cw_ref[pl.ds(1, 1), :] * buf[pl.ds(FFN_HALO - 1, tm), :]
                + cw_ref[pl.ds(0, 1), :] * buf[pl.ds(FFN_HALO - 2, tm), :])

    cg = conv(ugbuf, cwg_ref, cbg_ref)
    cv = conv(uvbuf, cwv_ref, cbv_ref)
    o_ref[...] = (cg * _sigmoid(cg) * cv).astype(o_ref.dtype)


def _ffn_up(h, w_up, conv_w, conv_b, seq):
    m, d = h.shape
    dff = w_up.shape[1] // 2
    tm = min(1024, seq)
    tn = _pick(dff, (512, 256, 128))
    nj = dff // tn
    per = tm // FFN_HALO
    kern = functools.partial(_ffn_up_kernel, tm=tm, seq=seq)
    cb = conv_b.reshape(1, 2 * dff)
    return pl.pallas_call(
        kern,
        grid=(m // tm, nj),
        in_specs=[pl.BlockSpec((tm, d), lambda i, j: (i, 0)),
                  pl.BlockSpec((FFN_HALO, d), lambda i, j: (jnp.maximum(i * per - 1, 0), 0)),
                  pl.BlockSpec((d, tn), lambda i, j: (0, j)),
                  pl.BlockSpec((d, tn), lambda i, j: (0, j + nj)),
                  pl.BlockSpec((3, tn), lambda i, j: (0, j)),
                  pl.BlockSpec((3, tn), lambda i, j: (0, j + nj)),
                  pl.BlockSpec((1, tn), lambda i, j: (0, j)),
                  pl.BlockSpec((1, tn), lambda i, j: (0, j + nj))],
        out_specs=pl.BlockSpec((tm, tn), lambda i, j: (i, j)),
        out_shape=jax.ShapeDtypeStruct((m, dff), MXU_DTYPE),
        scratch_shapes=[pltpu.VMEM((FFN_HALO + tm, d), MXU_DTYPE),
                        pltpu.VMEM((FFN_HALO + tm, tn), F32),
                        pltpu.VMEM((FFN_HALO + tm, tn), F32)],
        compiler_params=_cparams(("parallel", "arbitrary")),
        name="ffn_up_conv_gate",
    )(h, h, w_up, w_up, conv_w, conv_w, cb, cb)


def _ffn_down_kernel(a_ref, w_ref, x_ref, nw_ref, o_ref):
    r = jnp.dot(a_ref[...], w_ref[...], preferred_element_type=F32)
    ms = jnp.mean(r * r, axis=-1, keepdims=True)
    o_ref[...] = x_ref[...] + r * lax.rsqrt(ms + EPS) * nw_ref[...]


def _ffn_down(act, w_down, x2d, norm_w):
    m, dff = act.shape
    d = w_down.shape[1]
    tm = 256
    return pl.pallas_call(
        _ffn_down_kernel,
        grid=(m // tm,),
        in_specs=[pl.BlockSpec((tm, dff), lambda i: (i, 0)),
                  pl.BlockSpec((dff, d), lambda i: (0, 0), pipeline_mode=pl.Buffered(1)),
                  pl.BlockSpec((tm, d), lambda i: (i, 0)),
                  pl.BlockSpec((1, d), lambda i: (0, 0))],
        out_specs=pl.BlockSpec((tm, d), lambda i: (i, 0)),
        out_shape=jax.ShapeDtypeStruct((m, d), F32),
        compiler_params=_cparams(("parallel",)),
        name="ffn_down_norm_res",
    )(act, w_down, x2d, norm_w.reshape(1, d))


def _gate_weight(w_gate):
    d = w_gate.shape[0]
    per = HEADS_PER_GROUP * 3
    w = w_gate.reshape(d, N_KV_GROUPS, per)
    w = jnp.pad(w, ((0, 0), (0, 0), (0, GATE_PAD - per)))
    return w.reshape(d, N_KV_GROUPS * GATE_PAD)


def _mixer_layer(x, p, l):
    bsz, seq, d = x.shape
    m = bsz * seq
    x2d = x.reshape(m, d)
    g, hpg, dh = N_KV_GROUPS, HEADS_PER_GROUP, HEAD_DIM
    a_in_w = 2 * CONV_WIDTH
    b_in_w = 2 * SG_WIDTH
    q_w = N_HEADS * dh
    kv_w = 6 * g * dh
    gate_w = 3 * N_HEADS
    o0 = 0
    o1 = o0 + a_in_w
    o2 = o1 + b_in_w
    o3 = o2 + q_w
    o4 = o3 + kv_w
    o5 = o4 + gate_w
    w_in = p["w_in"][l]

    h = _rmsnorm_cast(x2d, p["norm_mix_pre"][l])
    wc = lambda w: w.astype(MXU_DTYPE)
    a_in = _matmul(h, wc(w_in[:, o0:o1]), name="proj_a")
    b_in = _matmul(h, wc(w_in[:, o1:o2]), name="proj_b")
    h3 = h.reshape(bsz, seq, d)
    aug = 2 * dh
    w_q = w_in[:, o2:o3].reshape(d, N_HEADS, dh)
    w_kv = w_in[:, o3:o4].reshape(d, 6, g, dh)
    pad_lanes = lambda w: jnp.pad(w, ((0, 0), (0, 0), (0, aug - dh))).reshape(d, -1)
    w_rows = wc(jnp.concatenate([pad_lanes(w_q), pad_lanes(w_kv[:, 2]), pad_lanes(w_kv[:, 4])], axis=1))
    onehot = np.zeros((seq, aug), np.float32)
    onehot[np.arange(seq), dh + np.arange(seq) // SLC_BLOCK] = 1.0
    slc_slabs = (N_HEADS * aug // PROJ_SLAB, (N_HEADS + g) * aug // PROJ_SLAB)
    rows_op = _proj_planes(h3, w_rows, jnp.asarray(onehot), MXU_DTYPE, aug, slc_slabs, "proj_attn_rows")
    w_cmp = wc(jnp.concatenate([w_kv[:, 0].reshape(d, g * dh), w_kv[:, 1].reshape(d, g * dh)], axis=1))
    cmp_op = _proj_planes(h3, w_cmp, jnp.zeros((seq, dh), F32), F32, dh, (0, 0), "proj_attn_cmp")
    pad_rows = lambda w: jnp.pad(w.transpose(1, 2, 0), ((0, 0), (0, aug - dh), (0, 0))).reshape(-1, d)
    w_cols = wc(jnp.concatenate([pad_rows(w_kv[:, 3]), pad_rows(w_kv[:, 5])], axis=0))
    ones_rows = np.zeros((2 * g, aug, min(512, seq)), np.float32)
    ones_rows[:, dh] = 1.0
    cols_op = _proj_cols(h3, w_cols, jnp.asarray(ones_rows.reshape(2 * g * aug, -1)), 2 * g)
    gate_logits = _matmul(h, wc(_gate_weight(w_in[:, o4:o5])), name="proj_gate")
    merge_logits = _matmul(h, wc(w_in[:, o5:]), name="proj_merge")

    h_a = _conformer(a_in.reshape(bsz, seq, a_in_w), p["conv_a_w"][l], p["conv_a_b"][l],
                     p["ln_a_g"][l], p["ln_a_b"][l]).reshape(m, CONV_WIDTH)
    h_b = _gmlp(b_in, p["ln_b_g"][l], p["ln_b_b"][l], p["sg_w"][l], p["sg_b"][l])

    n_piece = seq // CMP_STRIDE
    kc = _compress(cmp_op[:, :g].reshape(bsz, g, n_piece, CMP_STRIDE * dh), p["cmp_pe_k"][l], p["cmp_w1_k"][l],
                   p["cmp_w2_k"][l])
    vc = _compress(cmp_op[:, g:].reshape(bsz, g, n_piece, CMP_STRIDE * dh), p["cmp_pe_v"][l], p["cmp_w1_v"][l],
                   p["cmp_w2_v"][l])
    o = _nsa_attention(rows_op, cols_op, kc, vc, gate_logits.reshape(bsz, seq, g * GATE_PAD))
    h_c = o.reshape(m, N_HEADS * dh)

    return _merge(h_a, h_b, h_c, merge_logits, x2d, p["w_a_out"][l], p["w_b_out"][l], p["w_c_out"][l], p["w_o"][l],
                  p["norm_mix_post"][l]).reshape(bsz, seq, d)


def _ffn_layer(x, p, l):
    bsz, seq, d = x.shape
    m = bsz * seq
    x2d = x.reshape(m, d)
    h = _rmsnorm_cast(x2d, p["norm_ffn_pre"][l])
    act = _ffn_up(h, p["w_up"][l].astype(MXU_DTYPE), p["ffn_conv_w"][l], p["ffn_conv_b"][l], seq)
    return _ffn_down(act, p["w_down"][l].astype(MXU_DTYPE), x2d, p["norm_ffn_post"][l]).reshape(bsz, seq, d)


def kernel(x, norm_mix_pre, norm_mix_post, norm_ffn_pre, norm_ffn_post, w_in, conv_a_w, conv_a_b, ln_a_g, ln_a_b,
           w_a_out, ln_b_g, ln_b_b, sg_w, sg_b, w_b_out, cmp_pe_k, cmp_w1_k, cmp_w2_k, cmp_pe_v, cmp_w1_v,
           cmp_w2_v, w_c_out, w_o, w_up, ffn_conv_w, ffn_conv_b, w_down):
    p = dict(norm_mix_pre=norm_mix_pre, norm_mix_post=norm_mix_post, norm_ffn_pre=norm_ffn_pre,
             norm_ffn_post=norm_ffn_post, w_in=w_in, conv_a_w=conv_a_w, conv_a_b=conv_a_b, ln_a_g=ln_a_g,
             ln_a_b=ln_a_b, w_a_out=w_a_out, ln_b_g=ln_b_g, ln_b_b=ln_b_b, sg_w=sg_w, sg_b=sg_b, w_b_out=w_b_out,
             cmp_pe_k=cmp_pe_k, cmp_w1_k=cmp_w1_k, cmp_w2_k=cmp_w2_k, cmp_pe_v=cmp_pe_v, cmp_w1_v=cmp_w1_v,
             cmp_w2_v=cmp_w2_v, w_c_out=w_c_out, w_o=w_o, w_up=w_up, ffn_conv_w=ffn_conv_w,
             ffn_conv_b=ffn_conv_b, w_down=w_down)
    for l in range(w_in.shape[0]):
        x = _mixer_layer(x, p, l)
        x = _ffn_layer(x, p, l)
    return x
```

```python
import functools

import numpy as np
import jax
import jax.numpy as jnp
from jax import lax
from jax.experimental import pallas as pl
from jax.experimental.pallas import tpu as pltpu

F32 = jnp.float32
MXU_DTYPE = jnp.bfloat16

EPS = 1e-6
NEG = -1e30
FORCE = 1e4

CONV_WIDTH = 1024
DW_CONV_SIZE = 31
CONV_HALO = 32
CONV_SUB = 64
SG_WIDTH = 1024
SG_CHUNK = 128
SG_GROUPS = 8
N_HEADS = 16
N_KV_GROUPS = 4
HEADS_PER_GROUP = 4
HEAD_DIM = 64
CMP_BLOCK = 32
CMP_STRIDE = 16
CMP_HIDDEN = 256
SLC_BLOCK = 64
SLC_TOP_N = 16
WINDOW = 512
D_FF = 5632
FFN_HALO = 16
GATE_PAD = 128
PROJ_SLAB = 512
GATE_ROWS = 16

VMEM_LIMIT = 56 * 1024 * 1024


def _cparams(sem):
    return pltpu.CompilerParams(dimension_semantics=sem, vmem_limit_bytes=VMEM_LIMIT)


def _sigmoid(x):
    return jax.nn.sigmoid(x)


def _rmsnorm_kernel(x_ref, w_ref, o_ref):
    x = x_ref[...]
    ms = jnp.mean(x * x, axis=-1, keepdims=True)
    o_ref[...] = (x * lax.rsqrt(ms + EPS) * w_ref[...]).astype(o_ref.dtype)


def _rmsnorm_cast(x2d, w):
    m, d = x2d.shape
    tm = min(512, m)
    return pl.pallas_call(
        _rmsnorm_kernel,
        grid=(m // tm,),
        in_specs=[pl.BlockSpec((tm, d), lambda i: (i, 0)),
                  pl.BlockSpec((1, d), lambda i: (0, 0))],
        out_specs=pl.BlockSpec((tm, d), lambda i: (i, 0)),
        out_shape=jax.ShapeDtypeStruct((m, d), MXU_DTYPE),
        compiler_params=_cparams(("parallel",)),
        name="rmsnorm_cast",
    )(x2d, w.reshape(1, d))


def _mm_kernel(a_ref, w_ref, o_ref):
    o_ref[...] = jnp.dot(a_ref[...], w_ref[...], preferred_element_type=F32).astype(o_ref.dtype)


def _pick(n, cands):
    for c in cands:
        if n % c == 0:
            return c
    raise ValueError(f"no tile for {n}")


def _matmul(a, w, out_dtype=F32, name="matmul"):
    m, k = a.shape
    n = w.shape[1]
    tm = min(1024, m)
    tn = _pick(n, (512, 384, 256, 128))
    return pl.pallas_call(
        _mm_kernel,
        grid=(m // tm, n // tn),
        in_specs=[pl.BlockSpec((tm, k), lambda i, j: (i, 0)),
                  pl.BlockSpec((k, tn), lambda i, j: (0, j))],
        out_specs=pl.BlockSpec((tm, tn), lambda i, j: (i, j)),
        out_shape=jax.ShapeDtypeStruct((m, n), out_dtype),
        compiler_params=_cparams(("parallel", "arbitrary")),
        name=name,
    )(a, w)


def _proj_planes_kernel(h_ref, w_ref, add_ref, o_ref, *, add_lo, add_hi, width):
    j = pl.program_id(2)
    res = jnp.dot(h_ref[...], w_ref[...], preferred_element_type=F32)
    if add_hi > add_lo:
        add = jnp.where((j >= add_lo) & (j < add_hi), add_ref[...], 0.0)
    for p in range(o_ref.shape[0]):
        piece = res[:, p * width:(p + 1) * width]
        o_ref[p] = (piece + add if add_hi > add_lo else piece).astype(o_ref.dtype)


def _proj_planes(h3, w, add, out_dtype, width, add_range, name):
    bsz, seq, d = h3.shape
    n = w.shape[1]
    slab = PROJ_SLAB
    per = slab // width
    tm = min(1024, seq)
    kern = functools.partial(_proj_planes_kernel, add_lo=add_range[0], add_hi=add_range[1], width=width)
    return pl.pallas_call(
        kern,
        grid=(bsz, seq // tm, n // slab),
        in_specs=[pl.BlockSpec((None, tm, d), lambda b, i, j: (b, i, 0)),
                  pl.BlockSpec((d, slab), lambda b, i, j: (0, j)),
                  pl.BlockSpec((tm, width), lambda b, i, j: (i, 0))],
        out_specs=pl.BlockSpec((None, per, tm, width), lambda b, i, j: (b, j, i, 0)),
        out_shape=jax.ShapeDtypeStruct((bsz, n // width, seq, width), out_dtype),
        compiler_params=_cparams(("parallel", "parallel", "arbitrary")),
        name=name,
    )(h3, w, add)


def _proj_cols_kernel(h_ref, wt_ref, add_ref, o_ref):
    res = lax.dot_general(wt_ref[...], h_ref[...], (((1,), (1,)), ((), ())), preferred_element_type=F32)
    o_ref[...] = (res + add_ref[...]).astype(o_ref.dtype).reshape(o_ref.shape)


def _proj_cols(h3, w_t, add, planes):
    bsz, seq, d = h3.shape
    n = w_t.shape[0]
    tm = min(512, seq)
    return pl.pallas_call(
        _proj_cols_kernel,
        grid=(bsz, seq // tm),
        in_specs=[pl.BlockSpec((None, tm, d), lambda b, i: (b, i, 0)),
                  pl.BlockSpec((n, d), lambda b, i: (0, 0), pipeline_mode=pl.Buffered(1)),
                  pl.BlockSpec((n, tm), lambda b, i: (0, 0), pipeline_mode=pl.Buffered(1))],
        out_specs=pl.BlockSpec((None, planes, n // planes, tm), lambda b, i: (b, 0, 0, i)),
        out_shape=jax.ShapeDtypeStruct((bsz, planes, n // planes, seq), MXU_DTYPE),
        compiler_params=_cparams(("parallel", "parallel")),
        name="proj_attn_cols",
    )(h3, w_t, add)


def _conformer_kernel(cur_ref, halo_ref, cw_ref, cb_ref, lg_ref, lb_ref, o_ref, hbuf, cbuf, *, ts):
    c = CONV_WIDTH
    i = pl.program_id(1)
    cur = cur_ref[...]
    hbuf[pl.ds(CONV_HALO, ts), :] = cur[:, :c] * _sigmoid(cur[:, c:])
    hal = halo_ref[...]
    hh = hal[:, :c] * _sigmoid(hal[:, c:])
    hbuf[pl.ds(0, CONV_HALO), :] = jnp.where(i == 0, 0.0, hh)

    first = CONV_HALO - (DW_CONV_SIZE - 1)
    span = CONV_SUB + CONV_HALO

    def blk_body(idx, carry):
        off = pl.multiple_of((idx // (ts // CONV_SUB)) * 128, 128)
        r0 = pl.multiple_of((idx % (ts // CONV_SUB)) * CONV_SUB, CONV_SUB)
        x = hbuf[pl.ds(r0, span), pl.ds(off, 128)]
        acc = jnp.broadcast_to(cb_ref[:, pl.ds(off, 128)], (CONV_SUB, 128))
        for r in range(8):
            xr = x if r == 0 else pltpu.roll(x, span - r, 0)
            for a in range(CONV_HALO // 8 + 1):
                k = 8 * a + r - first
                if 0 <= k < DW_CONV_SIZE:
                    acc = acc + cw_ref[pl.ds(k, 1), pl.ds(off, 128)] * xr[8 * a:8 * a + CONV_SUB]
        cbuf[pl.ds(r0, CONV_SUB), pl.ds(off, 128)] = acc
        return carry

    lax.fori_loop(0, (c // 128) * (ts // CONV_SUB), blk_body, 0)
    y = cbuf[...]
    mean = jnp.mean(y, axis=-1, keepdims=True)
    yc = y - mean
    var = jnp.mean(yc * yc, axis=-1, keepdims=True)
    z = yc * lax.rsqrt(var + EPS) * lg_ref[...] + lb_ref[...]
    o_ref[...] = (z * _sigmoid(z)).astype(o_ref.dtype)


def _conformer(a_in, conv_w, conv_b, ln_g, ln_b):
    bsz, seq, _ = a_in.shape
    c = CONV_WIDTH
    ts = min(256, seq)
    per = ts // CONV_HALO
    kern = functools.partial(_conformer_kernel, ts=ts)
    return pl.pallas_call(
        kern,
        grid=(bsz, seq // ts),
        in_specs=[pl.BlockSpec((None, ts, 2 * c), lambda b, i: (b, i, 0)),
                  pl.BlockSpec((None, CONV_HALO, 2 * c), lambda b, i: (b, jnp.maximum(i * per - 1, 0), 0)),
                  pl.BlockSpec((DW_CONV_SIZE, c), lambda b, i: (0, 0)),
                  pl.BlockSpec((1, c), lambda b, i: (0, 0)),
                  pl.BlockSpec((1, c), lambda b, i: (0, 0)),
                  pl.BlockSpec((1, c), lambda b, i: (0, 0))],
        out_specs=pl.BlockSpec((None, ts, c), lambda b, i: (b, i, 0)),
        out_shape=jax.ShapeDtypeStruct((bsz, seq, c), MXU_DTYPE),
        scratch_shapes=[pltpu.VMEM((CONV_HALO + ts, c), F32), pltpu.VMEM((ts, c), F32)],
        compiler_params=_cparams(("parallel", "parallel")),
        name="conformer_conv",
    )(a_in, a_in, conv_w, conv_b.reshape(1, c), ln_g.reshape(1, c), ln_b.reshape(1, c))


def _gmlp_kernel(x_ref, lg_ref, lb_ref, sgw_ref, sgbt_ref, o_ref, *, ts):
    c = SG_WIDTH
    gd = c // SG_GROUPS
    x = x_ref[...]
    z = 0.5 * x * (1.0 + lax.erf(x * np.float32(np.sqrt(0.5))))
    u = z[:, :c]
    v = z[:, c:]
    mean = jnp.mean(v, axis=-1, keepdims=True)
    vc = v - mean
    var = jnp.mean(vc * vc, axis=-1, keepdims=True)
    v = vc * lax.rsqrt(var + EPS) * lg_ref[...] + lb_ref[...]
    row = lax.broadcasted_iota(jnp.int32, (SG_CHUNK, SG_CHUNK), 0)
    col = lax.broadcasted_iota(jnp.int32, (SG_CHUNK, SG_CHUNK), 1)
    causal = col <= row
    for g in range(SG_GROUPS):
        w = jnp.where(causal, sgw_ref[g], 0.0).astype(MXU_DTYPE)
        bias = sgbt_ref[:, g:g + 1]
        for ch in range(ts // SG_CHUNK):
            r0 = ch * SG_CHUNK
            vg = v[r0:r0 + SG_CHUNK, g * gd:(g + 1) * gd].astype(MXU_DTYPE)
            f = jnp.dot(w, vg, preferred_element_type=F32) + bias
            o_ref[r0:r0 + SG_CHUNK, g * gd:(g + 1) * gd] = (u[r0:r0 + SG_CHUNK, g * gd:(g + 1) * gd]
                                                            * f).astype(o_ref.dtype)


def _gmlp(b_in, ln_g, ln_b, sg_w, sg_b):
    m, _ = b_in.shape
    c = SG_WIDTH
    ts = 256
    kern = functools.partial(_gmlp_kernel, ts=ts)
    return pl.pallas_call(
        kern,
        grid=(m // ts,),
        in_specs=[pl.BlockSpec((ts, 2 * c), lambda i: (i, 0)),
                  pl.BlockSpec((1, c), lambda i: (0, 0)),
                  pl.BlockSpec((1, c), lambda i: (0, 0)),
                  pl.BlockSpec((SG_GROUPS, SG_CHUNK, SG_CHUNK), lambda i: (0, 0, 0)),
                  pl.BlockSpec((SG_CHUNK, SG_GROUPS), lambda i: (0, 0))],
        out_specs=pl.BlockSpec((ts, c), lambda i: (i, 0)),
        out_shape=jax.ShapeDtypeStruct((m, c), MXU_DTYPE),
        compiler_params=_cparams(("parallel",)),
        name="gmlp_gating",
    )(b_in, ln_g.reshape(1, c), ln_b.reshape(1, c), sg_w, sg_b.T)


def _compress_kernel(p_ref, pelo_ref, pehi_ref, w1a_ref, w1b_ref, w2_ref, o_ref):
    p = p_ref[...]
    a = jnp.dot((p + pelo_ref[...]).astype(MXU_DTYPE), w1a_ref[...], preferred_element_type=F32)
    b = jnp.dot((p + pehi_ref[...]).astype(MXU_DTYPE), w1b_ref[...], preferred_element_type=F32)
    n = p.shape[0]
    b_next = pltpu.roll(b, n - 1, 0)
    hid = a + b_next
    hid = hid * _sigmoid(hid)
    o_ref[...] = jnp.dot(hid.astype(MXU_DTYPE), w2_ref[...], preferred_element_type=F32)


def _compress(pieces, pe, w1, w2):
    bsz, g, n_piece, width = pieces.shape
    half = CMP_BLOCK // 2
    pe_lo = pe[:half].reshape(1, width)
    pe_hi = pe[half:].reshape(1, width)
    w1a = w1[:half].reshape(width, CMP_HIDDEN).astype(MXU_DTYPE)
    w1b = w1[half:].reshape(width, CMP_HIDDEN).astype(MXU_DTYPE)
    return pl.pallas_call(
        _compress_kernel,
        grid=(bsz, g),
        in_specs=[pl.BlockSpec((None, None, n_piece, width), lambda b, gg: (b, gg, 0, 0)),
                  pl.BlockSpec((1, width), lambda b, gg: (0, 0)),
                  pl.BlockSpec((1, width), lambda b, gg: (0, 0)),
                  pl.BlockSpec((width, CMP_HIDDEN), lambda b, gg: (0, 0)),
                  pl.BlockSpec((width, CMP_HIDDEN), lambda b, gg: (0, 0)),
                  pl.BlockSpec((CMP_HIDDEN, HEAD_DIM), lambda b, gg: (0, 0))],
        out_specs=pl.BlockSpec((None, None, n_piece, HEAD_DIM), lambda b, gg: (b, gg, 0, 0)),
        out_shape=jax.ShapeDtypeStruct((bsz, g, n_piece, HEAD_DIM), F32),
        compiler_params=_cparams(("parallel", "parallel")),
        name="nsa_compress",
    )(pieces, pe_lo, pe_hi, w1a, w1b, w2.astype(MXU_DTYPE))


ROW_BLK = 32
M_INIT = -3e38
NSA_PROLOGUE_PHASES = 5
NSA_PHASE_LAG = 2


def _masked_softmax(s, mask):
    s = jnp.where(mask, s, NEG)
    m = jnp.max(s, axis=-1, keepdims=True)
    e = jnp.where(mask, jnp.exp(s - m), 0.0)
    return e / jnp.maximum(jnp.sum(e, axis=-1, keepdims=True), 1e-30)


def _nt_dot(a, b, precision=None):
    return lax.dot_general(a, b, (((1,), (1,)), ((), ())), preferred_element_type=F32, precision=precision)


def _nsa_kernel(*refs, tq, ck, nt):
    n_in = 12
    shared, scratch = refs[:n_in], refs[n_in:]
    per_tile = len(scratch) // nt
    tiles = [_nsa_tile(u, *shared, *scratch[u * per_tile:(u + 1) * per_tile], tq=tq, ck=ck, nt=nt)
             for u in range(nt)]
    loops = [None] * nt
    for step in range(NSA_PROLOGUE_PHASES + NSA_PHASE_LAG * (nt - 1)):
        for u in range(nt):
            if 0 <= step - NSA_PHASE_LAG * u < NSA_PROLOGUE_PHASES:
                loops[u] = next(tiles[u])

    def pair_body(i, carry):
        for stage in zip(*[stages(i) for _, stages in loops]):
            for run in stage:
                run()
        return carry

    lax.fori_loop(0, loops[0][0], pair_body, 0)
    for t in tiles:
        next(t, None)


def _nsa_tile(u, q_ref, kc_ref, vct_ref, ks_ref, vst_ref, kw_ref, vwt_ref, gate_ref, mapt_ref, bt_ref, bandt_ref,
              o_ref, sc_scr, pc_scr, sw_scr, pw_scr, sa_scr, sb_scr, pa_scr, pb_scr, m_scr, ala_scr, alb_scr, acc_scr,
              mw_scr, al_scr, accw_scr, rank_scr, *, tq, ck, nt):
    hpg = HEADS_PER_GROUP
    dh = HEAD_DIM
    aug = 2 * dh
    cols = hpg * tq
    n_blk = mapt_ref.shape[0]
    q0 = (pl.program_id(2) * nt + u) * tq
    tile_rows = pl.ds(u * tq, tq)
    qp = (q_ref[:, tile_rows, :] * np.float32(dh ** -0.5).astype(q_ref.dtype)).reshape(cols, aug)
    eye4 = ((lax.broadcasted_iota(jnp.int32, (cols, tq), 0) & (tq - 1))
            == lax.broadcasted_iota(jnp.int32, (cols, tq), 1)).astype(MXU_DTYPE)
    t_lane = q0 + lax.broadcasted_iota(jnp.int32, (1, tq), 1)

    def online(s_ref, p_ref, kr, m_ref, acc_ref, v_t):
        for h in range(hpg):
            cb = pl.ds(h * tq, tq)
            s = s_ref[kr, cb]
            m_old = m_ref[:, cb]
            m_new = jnp.maximum(m_old, jnp.max(s, axis=0, keepdims=True))
            p_ref[kr, cb] = jnp.exp(s - m_new).astype(p_ref.dtype)
            al_scr[:, cb] = jnp.exp(m_old - m_new)
            m_ref[:, cb] = m_new
        acc_ref[...] = al_scr[...] * acc_ref[...] + jnp.dot(v_t, p_ref[kr, :], preferred_element_type=F32)

    def finish(acc):
        return acc[:dh] / jnp.maximum(acc[dh:dh + 1], 1e-30)

    n_cmp = kc_ref.shape[0]
    sc_scr[...] = _nt_dot(kc_ref[...].astype(MXU_DTYPE), qp)
    wlen = WINDOW + tq
    w0 = pl.multiple_of(jnp.maximum(q0 - WINDOW, 0), tq)
    lhs_win = jnp.concatenate([qp, eye4], axis=1)
    rhs_win = jnp.concatenate([kw_ref[pl.ds(w0, wlen), :], bandt_ref[:, pl.ds(pl.multiple_of(q0 - w0, tq), tq)]],
                              axis=1)
    sw_scr[...] = _nt_dot(rhs_win, lhs_win)
    yield None

    cmask = (lax.broadcasted_iota(jnp.int32, (n_cmp, 1), 0) * CMP_STRIDE + (CMP_BLOCK - 1)) <= t_lane
    psum = jnp.zeros((n_cmp, tq), F32)
    for h in range(hpg):
        cb = pl.ds(h * tq, tq)
        s = jnp.where(cmask, sc_scr[:, cb], NEG)
        e = jnp.where(cmask, jnp.exp(s - jnp.max(s, axis=0, keepdims=True)), 0.0)
        p = e / jnp.maximum(jnp.sum(e, axis=0, keepdims=True), 1e-30)
        pc_scr[:, cb] = p.astype(pc_scr.dtype)
        psum = psum + p
    o_cmp = jnp.dot(vct_ref[...].astype(MXU_DTYPE), pc_scr[...], preferred_element_type=F32)
    yield None

    mw_scr[...] = jnp.full(mw_scr.shape, M_INIT, F32)
    accw_scr[...] = jnp.zeros(accw_scr.shape, F32)
    for k_lo in range(0, wlen, ck):
        nk = min(ck, wlen - k_lo)
        online(sw_scr, pw_scr, pl.ds(k_lo, nk), mw_scr, accw_scr,
               vwt_ref[:, pl.ds(pl.multiple_of(w0 + k_lo, tq), nk)])
    o_win = finish(accw_scr[...])
    yield None

    imp_t = jnp.dot(mapt_ref[...], psum, preferred_element_type=F32, precision=lax.Precision.HIGHEST)
    j_t = lax.broadcasted_iota(jnp.int32, (n_blk, tq), 0)
    cur_t = (q0 + lax.broadcasted_iota(jnp.int32, (n_blk, tq), 1)) >> 6
    forced = (j_t == 0) | (j_t == cur_t) | (j_t == cur_t - 1)
    score_t = jnp.where(j_t <= cur_t, jnp.where(forced, FORCE, imp_t), NEG)
    rank_scr[...] = score_t
    sel_rows = []
    for v in range(n_blk // 8):
        sj = score_t[8 * v:8 * v + 8]
        jv = j_t[8 * v:8 * v + 8]
        rank = jnp.zeros((8, tq), jnp.int32)
        for i in range(n_blk):
            ci = rank_scr[pl.ds(i, 1), :]
            if i < 8 * v:
                ahead = ci >= sj
            elif i > 8 * v + 7:
                ahead = ci > sj
            else:
                ahead = (ci > sj) | ((ci == sj) & (jv > i))
            rank = rank + ahead.astype(jnp.int32)
        sel_rows.append(jnp.where(rank < SLC_TOP_N, 0.0, NEG))
    yield None
    selb_t = jnp.concatenate([jnp.zeros((dh, tq), F32)] + sel_rows, axis=0).astype(MXU_DTYPE)
    selb = _nt_dot(eye4[:tq], selb_t).astype(MXU_DTYPE)
    lhs_slc = jnp.concatenate([(qp.reshape(hpg, tq, aug) + selb[None]).reshape(cols, aug), eye4], axis=1)

    n_full = q0 // ck

    def scores(c, s_ref):
        k0 = pl.multiple_of(jnp.minimum(c, n_full) * ck, ck)
        start = pl.multiple_of(jnp.clip(q0 - c * ck, -tq, ck) + tq, tq)
        rhs = jnp.concatenate([ks_ref[pl.ds(k0, ck), :], bt_ref[:, pl.ds(start, tq)]], axis=1)
        s_ref[...] = _nt_dot(rhs, lhs_slc)

    def softmax(s_ref, p_ref, a_ref):
        for h in range(hpg):
            cb = pl.ds(h * tq, tq)
            s = s_ref[:, cb]
            m_old = m_scr[:, cb]
            m_new = jnp.maximum(m_old, jnp.max(s, axis=0, keepdims=True))
            p_ref[:, cb] = jnp.exp(s - m_new).astype(p_ref.dtype)
            a_ref[:, cb] = jnp.exp(m_old - m_new)
            m_scr[:, cb] = m_new

    def accumulate(c, p_ref, a_ref):
        k0 = pl.multiple_of(jnp.clip(c, 0, n_full) * ck, ck)
        acc_scr[...] = a_ref[...] * acc_scr[...] + jnp.dot(vst_ref[:, pl.ds(k0, ck)], p_ref[...],
                                                           preferred_element_type=F32)

    m_scr[...] = jnp.full(m_scr.shape, M_INIT, F32)
    acc_scr[...] = jnp.zeros(acc_scr.shape, F32)
    pb_scr[...] = jnp.zeros(pb_scr.shape, pb_scr.dtype)
    alb_scr[...] = jnp.ones(alb_scr.shape, F32)
    scores(0, sa_scr)

    def pair_stages(i):
        k = 2 * i
        return (lambda: softmax(sa_scr, pa_scr, ala_scr),
                lambda: accumulate(k - 1, pb_scr, alb_scr),
                lambda: scores(k + 1, sb_scr),
                lambda: softmax(sb_scr, pb_scr, alb_scr),
                lambda: accumulate(k, pa_scr, ala_scr),
                lambda: scores(k + 2, sa_scr))

    n_pairs = (n_full + 2) // 2
    yield n_pairs, pair_stages
    accumulate(2 * n_pairs - 1, pb_scr, alb_scr)
    o_slc = finish(acc_scr[...])

    gl = _sigmoid(gate_ref[:, tile_rows])
    outs = []
    for h in range(hpg):
        cb = slice(h * tq, (h + 1) * tq)
        outs.append(gl[3 * h:3 * h + 1] * o_cmp[:, cb] + gl[3 * h + 1:3 * h + 2] * o_slc[:, cb]
                    + gl[3 * h + 2:3 * h + 3] * o_win[:, cb])
    o_t = jnp.concatenate(outs, axis=0).astype(MXU_DTYPE)
    o_ref[tile_rows, :] = _nt_dot(eye4[:tq], o_t).astype(o_ref.dtype)


def _slc_map_t(n_piece, n_slc, n_blk):
    r = CMP_BLOCK // CMP_STRIDE
    a = SLC_BLOCK // CMP_STRIDE
    m = np.zeros((n_blk, n_piece), np.float32)
    for n in range(n_piece - r + 1):
        for i in range(r):
            m[(n + i) // a, n] += 1.0
    return m


def _nsa_attention(rows_op, cols_op, kc, vc, gate_logits):
    bsz, _, seq, aug = rows_op.shape
    g, hpg, dh = N_KV_GROUPS, HEADS_PER_GROUP, HEAD_DIM
    n_piece = kc.shape[2]
    n_slc = seq // SLC_BLOCK
    n_blk = dh
    assert n_slc <= n_blk
    tq = 128
    ck = 256
    wlen = WINDOW + tq
    cols = hpg * tq

    kc_p = jnp.pad(kc, ((0, 0), (0, 0), (0, 0), (0, dh)))
    vc_t = vc.astype(MXU_DTYPE).transpose(0, 1, 3, 2)
    gate_t = gate_logits.reshape(bsz, seq, g, GATE_PAD)[..., :GATE_ROWS].transpose(0, 2, 3, 1)

    kk = np.arange(ck)[:, None]
    bt = np.where(kk <= np.arange(-tq, ck + tq)[None, :], 0.0, NEG).astype(np.float32)
    kw_i = np.arange(wlen)[:, None]
    tau = np.arange(wlen)[None, :]
    bandt = np.where((kw_i <= tau) & (kw_i > tau - WINDOW), 0.0, NEG).astype(np.float32)

    nt = ck // tq
    kern = functools.partial(_nsa_kernel, tq=tq, ck=ck, nt=nt)
    per_group = lambda r, c, first=0: pl.BlockSpec((None, None, r, c), lambda b, gg, i: (b, first + gg, 0, 0))
    const = lambda shape: pl.BlockSpec(shape, lambda b, gg, i: (0, 0))
    per_tile = pltpu.VMEM
    return pl.pallas_call(
        kern,
        grid=(bsz, g, seq // (nt * tq)),
        in_specs=[pl.BlockSpec((None, hpg, nt * tq, aug), lambda b, gg, i: (b, gg, i, 0)),
                  per_group(n_piece, aug), per_group(dh, n_piece),
                  per_group(seq, aug, g * hpg), per_group(aug, seq),
                  per_group(seq, aug, g * hpg + g), per_group(aug, seq, g),
                  pl.BlockSpec((None, None, GATE_ROWS, nt * tq), lambda b, gg, i: (b, gg, 0, i)),
                  const((n_blk, n_piece)), const((ck, ck + 2 * tq)), const((wlen, wlen))],
        out_specs=pl.BlockSpec((None, nt * tq, hpg * dh), lambda b, gg, i: (b, i, gg)),
        out_shape=jax.ShapeDtypeStruct((bsz, seq, g * hpg * dh), MXU_DTYPE),
        scratch_shapes=[per_tile((n_piece, cols), F32),
                        per_tile((n_piece, cols), MXU_DTYPE),
                        per_tile((wlen, cols), F32),
                        per_tile((wlen, cols), MXU_DTYPE),
                        per_tile((ck, cols), F32),
                        per_tile((ck, cols), F32),
                        per_tile((ck, cols), MXU_DTYPE),
                        per_tile((ck, cols), MXU_DTYPE),
                        per_tile((1, cols), F32),
                        per_tile((1, cols), F32),
                        per_tile((1, cols), F32),
                        per_tile((aug, cols), F32),
                        per_tile((1, cols), F32),
                        per_tile((1, cols), F32),
                        per_tile((aug, cols), F32),
                        per_tile((n_blk, tq), F32)] * nt,
        compiler_params=_cparams(("parallel", "parallel", "arbitrary")),
        name="nsa_attention",
    )(rows_op, kc_p, vc_t, rows_op, cols_op, rows_op, cols_op, gate_t,
      jnp.asarray(_slc_map_t(n_piece, n_slc, n_blk)), jnp.asarray(bt, MXU_DTYPE), jnp.asarray(bandt, MXU_DTYPE))


def _residual_and_next_norm(x_ref, r, nw_ref, next_w_ref, o_ref, hn_ref):
    ms = jnp.mean(r * r, axis=-1, keepdims=True)
    o = x_ref[...] + r * lax.rsqrt(ms + EPS) * nw_ref[...]
    o_ref[...] = o
    ms_o = jnp.mean(o * o, axis=-1, keepdims=True)
    hn_ref[...] = (o * lax.rsqrt(ms_o + EPS) * next_w_ref[...]).astype(hn_ref.dtype)


def _merge_kernel(ha_ref, hb_ref, hc_ref, gl_ref, x_ref, wa_ref, wb_ref, wc_ref, wo_ref, nw_ref, next_w_ref,
                  o_ref, hn_ref, mix):
    d = x_ref.shape[1]
    branches = ((ha_ref, wa_ref), (hb_ref, wb_ref), (hc_ref, wc_ref))
    for n, (h_ref, w_ref) in enumerate(branches):
        y = _sigmoid(gl_ref[:, n * d:(n + 1) * d]) * jnp.dot(h_ref[...], w_ref[...], preferred_element_type=F32)
        if n == 0:
            mix[...] = y
        else:
            mix[...] += y
    r = jnp.dot(mix[...].astype(MXU_DTYPE), wo_ref[...], preferred_element_type=F32)
    _residual_and_next_norm(x_ref, r, nw_ref, next_w_ref, o_ref, hn_ref)


def _merge(ha, hb, hc, merge_logits, x2d, w_a, w_b, w_c, w_o, norm_w, next_norm_w):
    m, d = x2d.shape
    c = ha.shape[1]
    tm = 256
    row = lambda i: (i, 0)
    resident = lambda shape: pl.BlockSpec(shape, lambda i: (0, 0), pipeline_mode=pl.Buffered(1))
    wc = lambda w: w.astype(MXU_DTYPE)
    return pl.pallas_call(
        _merge_kernel,
        grid=(m // tm,),
        in_specs=[pl.BlockSpec((tm, c), row), pl.BlockSpec((tm, c), row), pl.BlockSpec((tm, c), row),
                  pl.BlockSpec((tm, 3 * d), row), pl.BlockSpec((tm, d), row),
                  resident((c, d)), resident((c, d)), resident((c, d)), resident((d, d)),
                  pl.BlockSpec((1, d), lambda i: (0, 0)), pl.BlockSpec((1, d), lambda i: (0, 0))],
        out_specs=[pl.BlockSpec((tm, d), row), pl.BlockSpec((tm, d), row)],
        out_shape=[jax.ShapeDtypeStruct((m, d), F32), jax.ShapeDtypeStruct((m, d), MXU_DTYPE)],
        scratch_shapes=[pltpu.VMEM((tm, d), F32)],
        compiler_params=_cparams(("parallel",)),
        name="merge_out_proj",
    )(ha, hb, hc, merge_logits, x2d, wc(w_a), wc(w_b), wc(w_c), wc(w_o), norm_w.reshape(1, d),
      next_norm_w.reshape(1, d))


def _ffn_up_kernel(h_ref, halo_ref, wg_ref, wv_ref, cwg_ref, cwv_ref, cbg_ref, cbv_ref, o_ref,
                   hbuf, ugbuf, uvbuf, *, tm, seq):
    i = pl.program_id(0)
    j = pl.program_id(1)

    @pl.when(j == 0)
    def _():
        hbuf[pl.ds(FFN_HALO, tm), :] = h_ref[...]
        first = (i * tm) % seq == 0
        hbuf[pl.ds(0, FFN_HALO), :] = jnp.where(first, jnp.zeros_like(halo_ref[...]), halo_ref[...])

    lhs = hbuf[...]
    ugbuf[...] = jnp.dot(lhs, wg_ref[...], preferred_element_type=F32)
    uvbuf[...] = jnp.dot(lhs, wv_ref[...], preferred_element_type=F32)

    def conv(buf, cw_ref, cb_ref):
        return (cb_ref[...] + cw_ref[pl.ds(2, 1), :] * buf[pl.ds(FFN_HALO, tm), :]
                + cw_ref[pl.ds(1, 1), :] * buf[pl.ds(FFN_HALO - 1, tm), :]
                + cw_ref[pl.ds(0, 1), :] * buf[pl.ds(FFN_HALO - 2, tm), :])

    cg = conv(ugbuf, cwg_ref, cbg_ref)
    cv = conv(uvbuf, cwv_ref, cbv_ref)
    o_ref[...] = (cg * _sigmoid(cg) * cv).astype(o_ref.dtype)


def _ffn_up(h, w_up, conv_w, conv_b, seq):
    m, d = h.shape
    dff = w_up.shape[1] // 2
    tm = min(1024, seq)
    tn = _pick(dff, (512, 256, 128))
    nj = dff // tn
    per = tm // FFN_HALO
    kern = functools.partial(_ffn_up_kernel, tm=tm, seq=seq)
    cb = conv_b.reshape(1, 2 * dff)
    return pl.pallas_call(
        kern,
        grid=(m // tm, nj),
        in_specs=[pl.BlockSpec((tm, d), lambda i, j: (i, 0)),
                  pl.BlockSpec((FFN_HALO, d), lambda i, j: (jnp.maximum(i * per - 1, 0), 0)),
                  pl.BlockSpec((d, tn), lambda i, j: (0, j)),
                  pl.BlockSpec((d, tn), lambda i, j: (0, j + nj)),
                  pl.BlockSpec((3, tn), lambda i, j: (0, j)),
                  pl.BlockSpec((3, tn), lambda i, j: (0, j + nj)),
                  pl.BlockSpec((1, tn), lambda i, j: (0, j)),
                  pl.BlockSpec((1, tn), lambda i, j: (0, j + nj))],
        out_specs=pl.BlockSpec((tm, tn), lambda i, j: (i, j)),
        out_shape=jax.ShapeDtypeStruct((m, dff), MXU_DTYPE),
        scratch_shapes=[pltpu.VMEM((FFN_HALO + tm, d), MXU_DTYPE),
                        pltpu.VMEM((FFN_HALO + tm, tn), F32),
                        pltpu.VMEM((FFN_HALO + tm, tn), F32)],
        compiler_params=_cparams(("parallel", "arbitrary")),
        name="ffn_up_conv_gate",
    )(h, h, w_up, w_up, conv_w, conv_w, cb, cb)


def _ffn_down_kernel(a_ref, w_ref, x_ref, nw_ref, o_ref):
    r = jnp.dot(a_ref[...], w_ref[...], preferred_element_type=F32)
    ms = jnp.mean(r * r, axis=-1, keepdims=True)
    o_ref[...] = x_ref[...] + r * lax.rsqrt(ms + EPS) * nw_ref[...]


def _ffn_down_next_kernel(a_ref, w_ref, x_ref, nw_ref, next_w_ref, o_ref, hn_ref):
    r = jnp.dot(a_ref[...], w_ref[...], preferred_element_type=F32)
    _residual_and_next_norm(x_ref, r, nw_ref, next_w_ref, o_ref, hn_ref)


def _ffn_down(act, w_down, x2d, norm_w, next_norm_w=None):
    m, dff = act.shape
    d = w_down.shape[1]
    tm = 256
    row = pl.BlockSpec((tm, d), lambda i: (i, 0))
    vec = pl.BlockSpec((1, d), lambda i: (0, 0))
    in_specs = [pl.BlockSpec((tm, dff), lambda i: (i, 0)),
                pl.BlockSpec((dff, d), lambda i: (0, 0), pipeline_mode=pl.Buffered(1)),
                row, vec]
    args = [act, w_down, x2d, norm_w.reshape(1, d)]
    x_shape = jax.ShapeDtypeStruct((m, d), F32)
    if next_norm_w is None:
        kern, out_specs, out_shape = _ffn_down_kernel, row, x_shape
    else:
        kern, out_specs = _ffn_down_next_kernel, [row, row]
        out_shape = [x_shape, jax.ShapeDtypeStruct((m, d), MXU_DTYPE)]
        in_specs.append(vec)
        args.append(next_norm_w.reshape(1, d))
    return pl.pallas_call(
        kern,
        grid=(m // tm,),
        in_specs=in_specs,
        out_specs=out_specs,
        out_shape=out_shape,
        compiler_params=_cparams(("parallel",)),
        name="ffn_down_norm_res",
    )(*args)


def _gate_weight(w_gate):
    d = w_gate.shape[0]
    per = HEADS_PER_GROUP * 3
    w = w_gate.reshape(d, N_KV_GROUPS, per)
    w = jnp.pad(w, ((0, 0), (0, 0), (0, GATE_PAD - per)))
    return w.reshape(d, N_KV_GROUPS * GATE_PAD)


def _mixer_layer(x, h, p, l):
    bsz, seq, d = x.shape
    m = bsz * seq
    x2d = x.reshape(m, d)
    g, hpg, dh = N_KV_GROUPS, HEADS_PER_GROUP, HEAD_DIM
    a_in_w = 2 * CONV_WIDTH
    b_in_w = 2 * SG_WIDTH
    q_w = N_HEADS * dh
    kv_w = 6 * g * dh
    gate_w = 3 * N_HEADS
    o0 = 0
    o1 = o0 + a_in_w
    o2 = o1 + b_in_w
    o3 = o2 + q_w
    o4 = o3 + kv_w
    o5 = o4 + gate_w
    w_in = p["w_in"][l]

    wc = lambda w: w.astype(MXU_DTYPE)
    a_in = _matmul(h, wc(w_in[:, o0:o1]), name="proj_a")
    b_in = _matmul(h, wc(w_in[:, o1:o2]), name="proj_b")
    h3 = h.reshape(bsz, seq, d)
    aug = 2 * dh
    w_q = w_in[:, o2:o3].reshape(d, N_HEADS, dh)
    w_kv = w_in[:, o3:o4].reshape(d, 6, g, dh)
    pad_lanes = lambda w: jnp.pad(w, ((0, 0), (0, 0), (0, aug - dh))).reshape(d, -1)
    w_rows = wc(jnp.concatenate([pad_lanes(w_q), pad_lanes(w_kv[:, 2]), pad_lanes(w_kv[:, 4])], axis=1))
    onehot = np.zeros((seq, aug), np.float32)
    onehot[np.arange(seq), dh + np.arange(seq) // SLC_BLOCK] = 1.0
    slc_slabs = (N_HEADS * aug // PROJ_SLAB, (N_HEADS + g) * aug // PROJ_SLAB)
    rows_op = _proj_planes(h3, w_rows, jnp.asarray(onehot), MXU_DTYPE, aug, slc_slabs, "proj_attn_rows")
    w_cmp = wc(jnp.concatenate([w_kv[:, 0].reshape(d, g * dh), w_kv[:, 1].reshape(d, g * dh)], axis=1))
    cmp_op = _proj_planes(h3, w_cmp, jnp.zeros((seq, dh), F32), F32, dh, (0, 0), "proj_attn_cmp")
    pad_rows = lambda w: jnp.pad(w.transpose(1, 2, 0), ((0, 0), (0, aug - dh), (0, 0))).reshape(-1, d)
    w_cols = wc(jnp.concatenate([pad_rows(w_kv[:, 3]), pad_rows(w_kv[:, 5])], axis=0))
    ones_rows = np.zeros((2 * g, aug, min(512, seq)), np.float32)
    ones_rows[:, dh] = 1.0
    cols_op = _proj_cols(h3, w_cols, jnp.asarray(ones_rows.reshape(2 * g * aug, -1)), 2 * g)
    gate_logits = _matmul(h, wc(_gate_weight(w_in[:, o4:o5])), name="proj_gate")
    merge_logits = _matmul(h, wc(w_in[:, o5:]), name="proj_merge")

    h_a = _conformer(a_in.reshape(bsz, seq, a_in_w), p["conv_a_w"][l], p["conv_a_b"][l],
                     p["ln_a_g"][l], p["ln_a_b"][l]).reshape(m, CONV_WIDTH)
    h_b = _gmlp(b_in, p["ln_b_g"][l], p["ln_b_b"][l], p["sg_w"][l], p["sg_b"][l])

    n_piece = seq // CMP_STRIDE
    kc = _compress(cmp_op[:, :g].reshape(bsz, g, n_piece, CMP_STRIDE * dh), p["cmp_pe_k"][l], p["cmp_w1_k"][l],
                   p["cmp_w2_k"][l])
    vc = _compress(cmp_op[:, g:].reshape(bsz, g, n_piece, CMP_STRIDE * dh), p["cmp_pe_v"][l], p["cmp_w1_v"][l],
                   p["cmp_w2_v"][l])
    o = _nsa_attention(rows_op, cols_op, kc, vc, gate_logits.reshape(bsz, seq, g * GATE_PAD))
    h_c = o.reshape(m, N_HEADS * dh)

    x_new, h_ffn = _merge(h_a, h_b, h_c, merge_logits, x2d, p["w_a_out"][l], p["w_b_out"][l], p["w_c_out"][l],
                          p["w_o"][l], p["norm_mix_post"][l], p["norm_ffn_pre"][l])
    return x_new.reshape(bsz, seq, d), h_ffn


def _ffn_layer(x, h, p, l, next_norm_w):
    bsz, seq, d = x.shape
    m = bsz * seq
    act = _ffn_up(h, p["w_up"][l].astype(MXU_DTYPE), p["ffn_conv_w"][l], p["ffn_conv_b"][l], seq)
    out = _ffn_down(act, p["w_down"][l].astype(MXU_DTYPE), x.reshape(m, d), p["norm_ffn_post"][l], next_norm_w)
    if next_norm_w is None:
        return out.reshape(bsz, seq, d), None
    return out[0].reshape(bsz, seq, d), out[1]


def kernel(x, norm_mix_pre, norm_mix_post, norm_ffn_pre, norm_ffn_post, w_in, conv_a_w, conv_a_b, ln_a_g, ln_a_b,
           w_a_out, ln_b_g, ln_b_b, sg_w, sg_b, w_b_out, cmp_pe_k, cmp_w1_k, cmp_w2_k, cmp_pe_v, cmp_w1_v,
           cmp_w2_v, w_c_out, w_o, w_up, ffn_conv_w, ffn_conv_b, w_down):
    p = dict(norm_mix_pre=norm_mix_pre, norm_mix_post=norm_mix_post, norm_ffn_pre=norm_ffn_pre,
             norm_ffn_post=norm_ffn_post, w_in=w_in, conv_a_w=conv_a_w, conv_a_b=conv_a_b, ln_a_g=ln_a_g,
             ln_a_b=ln_a_b, w_a_out=w_a_out, ln_b_g=ln_b_g, ln_b_b=ln_b_b, sg_w=sg_w, sg_b=sg_b, w_b_out=w_b_out,
             cmp_pe_k=cmp_pe_k, cmp_w1_k=cmp_w1_k, cmp_w2_k=cmp_w2_k, cmp_pe_v=cmp_pe_v, cmp_w1_v=cmp_w1_v,
             cmp_w2_v=cmp_w2_v, w_c_out=w_c_out, w_o=w_o, w_up=w_up, ffn_conv_w=ffn_conv_w,
             ffn_conv_b=ffn_conv_b, w_down=w_down)
    depth = w_in.shape[0]
    h = _rmsnorm_cast(x.reshape(-1, x.shape[-1]), norm_mix_pre[0])
    for l in range(depth):
        x, h = _mixer_layer(x, h, p, l)
        x, h = _ffn_layer(x, h, p, l, norm_mix_pre[l + 1] if l + 1 < depth else None)
    return x
```

```python
import functools

import numpy as np
import jax
import jax.numpy as jnp
from jax import lax
from jax.experimental import pallas as pl
from jax.experimental.pallas import tpu as pltpu

F32 = jnp.float32
MXU_DTYPE = jnp.bfloat16

EPS = 1e-6
NEG = -1e30
FORCE = 1e4

CONV_WIDTH = 1024
DW_CONV_SIZE = 31
CONV_HALO = 32
CONV_SUB = 64
SG_WIDTH = 1024
SG_CHUNK = 128
SG_GROUPS = 8
N_HEADS = 16
N_KV_GROUPS = 4
HEADS_PER_GROUP = 4
HEAD_DIM = 64
CMP_BLOCK = 32
CMP_STRIDE = 16
CMP_HIDDEN = 256
SLC_BLOCK = 64
SLC_TOP_N = 16
WINDOW = 512
D_FF = 5632
FFN_HALO = 16
GATE_PAD = 128
PROJ_SLAB = 512
GATE_ROWS = 16

VMEM_LIMIT = 56 * 1024 * 1024


def _cparams(sem):
    return pltpu.CompilerParams(dimension_semantics=sem, vmem_limit_bytes=VMEM_LIMIT)


def _sigmoid(x):
    return jax.nn.sigmoid(x)


def _rmsnorm_kernel(x_ref, w_ref, o_ref):
    x = x_ref[...]
    ms = jnp.mean(x * x, axis=-1, keepdims=True)
    o_ref[...] = (x * lax.rsqrt(ms + EPS) * w_ref[...]).astype(o_ref.dtype)


def _rmsnorm_cast(x2d, w):
    m, d = x2d.shape
    tm = min(512, m)
    return pl.pallas_call(
        _rmsnorm_kernel,
        grid=(m // tm,),
        in_specs=[pl.BlockSpec((tm, d), lambda i: (i, 0)),
                  pl.BlockSpec((1, d), lambda i: (0, 0))],
        out_specs=pl.BlockSpec((tm, d), lambda i: (i, 0)),
        out_shape=jax.ShapeDtypeStruct((m, d), MXU_DTYPE),
        compiler_params=_cparams(("parallel",)),
        name="rmsnorm_cast",
    )(x2d, w.reshape(1, d))


def _mm_kernel(a_ref, w_ref, o_ref):
    o_ref[...] = jnp.dot(a_ref[...], w_ref[...], preferred_element_type=F32).astype(o_ref.dtype)


def _pick(n, cands):
    for c in cands:
        if n % c == 0:
            return c
    raise ValueError(f"no tile for {n}")


def _matmul(a, w, out_dtype=F32, name="matmul"):
    m, k = a.shape
    n = w.shape[1]
    tm = min(1024, m)
    tn = _pick(n, (1024, 512, 384, 256, 128))
    return pl.pallas_call(
        _mm_kernel,
        grid=(m // tm, n // tn),
        in_specs=[pl.BlockSpec((tm, k), lambda i, j: (i, 0)),
                  pl.BlockSpec((k, tn), lambda i, j: (0, j))],
        out_specs=pl.BlockSpec((tm, tn), lambda i, j: (i, j)),
        out_shape=jax.ShapeDtypeStruct((m, n), out_dtype),
        compiler_params=_cparams(("parallel", "arbitrary")),
        name=name,
    )(a, w)


def _proj_planes_kernel(h_ref, w_ref, add_ref, o_ref, *, add_lo, add_hi, width):
    j = pl.program_id(2)
    res = jnp.dot(h_ref[...], w_ref[...], preferred_element_type=F32)
    if add_hi > add_lo:
        add = jnp.where((j >= add_lo) & (j < add_hi), add_ref[...], 0.0)
    for p in range(o_ref.shape[0]):
        piece = res[:, p * width:(p + 1) * width]
        o_ref[p] = (piece + add if add_hi > add_lo else piece).astype(o_ref.dtype)


def _proj_planes(h3, w, add, out_dtype, width, add_range, name):
    bsz, seq, d = h3.shape
    n = w.shape[1]
    slab = PROJ_SLAB
    per = slab // width
    tm = min(1024, seq)
    kern = functools.partial(_proj_planes_kernel, add_lo=add_range[0], add_hi=add_range[1], width=width)
    return pl.pallas_call(
        kern,
        grid=(bsz, seq // tm, n // slab),
        in_specs=[pl.BlockSpec((None, tm, d), lambda b, i, j: (b, i, 0)),
                  pl.BlockSpec((d, slab), lambda b, i, j: (0, j)),
                  pl.BlockSpec((tm, width), lambda b, i, j: (i, 0))],
        out_specs=pl.BlockSpec((None, per, tm, width), lambda b, i, j: (b, j, i, 0)),
        out_shape=jax.ShapeDtypeStruct((bsz, n // width, seq, width), out_dtype),
        compiler_params=_cparams(("parallel", "parallel", "arbitrary")),
        name=name,
    )(h3, w, add)


def _proj_cols_kernel(h_ref, wt_ref, add_ref, o_ref):
    res = lax.dot_general(wt_ref[...], h_ref[...], (((1,), (1,)), ((), ())), preferred_element_type=F32)
    o_ref[...] = (res + add_ref[...]).astype(o_ref.dtype).reshape(o_ref.shape)


def _proj_cols(h3, w_t, add, planes):
    bsz, seq, d = h3.shape
    n = w_t.shape[0]
    tm = min(512, seq)
    return pl.pallas_call(
        _proj_cols_kernel,
        grid=(bsz, seq // tm),
        in_specs=[pl.BlockSpec((None, tm, d), lambda b, i: (b, i, 0)),
                  pl.BlockSpec((n, d), lambda b, i: (0, 0), pipeline_mode=pl.Buffered(1)),
                  pl.BlockSpec((n, tm), lambda b, i: (0, 0), pipeline_mode=pl.Buffered(1))],
        out_specs=pl.BlockSpec((None, planes, n // planes, tm), lambda b, i: (b, 0, 0, i)),
        out_shape=jax.ShapeDtypeStruct((bsz, planes, n // planes, seq), MXU_DTYPE),
        compiler_params=_cparams(("parallel", "parallel")),
        name="proj_attn_cols",
    )(h3, w_t, add)


def _conformer_kernel(cur_ref, halo_ref, cw_ref, cb_ref, lg_ref, lb_ref, o_ref, hbuf, cbuf, *, ts):
    c = CONV_WIDTH
    i = pl.program_id(1)
    cur = cur_ref[...]
    hbuf[pl.ds(CONV_HALO, ts), :] = cur[:, :c] * _sigmoid(cur[:, c:])
    hal = halo_ref[...]
    hh = hal[:, :c] * _sigmoid(hal[:, c:])
    hbuf[pl.ds(0, CONV_HALO), :] = jnp.where(i == 0, 0.0, hh)

    first = CONV_HALO - (DW_CONV_SIZE - 1)
    span = CONV_SUB + CONV_HALO

    def blk_body(idx, carry):
        off = pl.multiple_of((idx // (ts // CONV_SUB)) * 128, 128)
        r0 = pl.multiple_of((idx % (ts // CONV_SUB)) * CONV_SUB, CONV_SUB)
        x = hbuf[pl.ds(r0, span), pl.ds(off, 128)]
        acc = jnp.broadcast_to(cb_ref[:, pl.ds(off, 128)], (CONV_SUB, 128))
        for r in range(8):
            xr = x if r == 0 else pltpu.roll(x, span - r, 0)
            for a in range(CONV_HALO // 8 + 1):
                k = 8 * a + r - first
                if 0 <= k < DW_CONV_SIZE:
                    acc = acc + cw_ref[pl.ds(k, 1), pl.ds(off, 128)] * xr[8 * a:8 * a + CONV_SUB]
        cbuf[pl.ds(r0, CONV_SUB), pl.ds(off, 128)] = acc
        return carry

    lax.fori_loop(0, (c // 128) * (ts // CONV_SUB), blk_body, 0)
    y = cbuf[...]
    mean = jnp.mean(y, axis=-1, keepdims=True)
    yc = y - mean
    var = jnp.mean(yc * yc, axis=-1, keepdims=True)
    z = yc * lax.rsqrt(var + EPS) * lg_ref[...] + lb_ref[...]
    o_ref[...] = (z * _sigmoid(z)).astype(o_ref.dtype)


def _conformer(a_in, conv_w, conv_b, ln_g, ln_b):
    bsz, seq, _ = a_in.shape
    c = CONV_WIDTH
    ts = min(256, seq)
    per = ts // CONV_HALO
    kern = functools.partial(_conformer_kernel, ts=ts)
    return pl.pallas_call(
        kern,
        grid=(bsz, seq // ts),
        in_specs=[pl.BlockSpec((None, ts, 2 * c), lambda b, i: (b, i, 0)),
                  pl.BlockSpec((None, CONV_HALO, 2 * c), lambda b, i: (b, jnp.maximum(i * per - 1, 0), 0)),
                  pl.BlockSpec((DW_CONV_SIZE, c), lambda b, i: (0, 0)),
                  pl.BlockSpec((1, c), lambda b, i: (0, 0)),
                  pl.BlockSpec((1, c), lambda b, i: (0, 0)),
                  pl.BlockSpec((1, c), lambda b, i: (0, 0))],
        out_specs=pl.BlockSpec((None, ts, c), lambda b, i: (b, i, 0)),
        out_shape=jax.ShapeDtypeStruct((bsz, seq, c), MXU_DTYPE),
        scratch_shapes=[pltpu.VMEM((CONV_HALO + ts, c), F32), pltpu.VMEM((ts, c), F32)],
        compiler_params=_cparams(("parallel", "parallel")),
        name="conformer_conv",
    )(a_in, a_in, conv_w, conv_b.reshape(1, c), ln_g.reshape(1, c), ln_b.reshape(1, c))


def _gmlp_kernel(x_ref, lg_ref, lb_ref, sgw_ref, sgbt_ref, o_ref, *, ts):
    c = SG_WIDTH
    gd = c // SG_GROUPS
    x = x_ref[...]
    z = 0.5 * x * (1.0 + lax.erf(x * np.float32(np.sqrt(0.5))))
    u = z[:, :c]
    v = z[:, c:]
    mean = jnp.mean(v, axis=-1, keepdims=True)
    vc = v - mean
    var = jnp.mean(vc * vc, axis=-1, keepdims=True)
    v = vc * lax.rsqrt(var + EPS) * lg_ref[...] + lb_ref[...]
    row = lax.broadcasted_iota(jnp.int32, (SG_CHUNK, SG_CHUNK), 0)
    col = lax.broadcasted_iota(jnp.int32, (SG_CHUNK, SG_CHUNK), 1)
    causal = col <= row
    for g in range(SG_GROUPS):
        w = jnp.where(causal, sgw_ref[g], 0.0).astype(MXU_DTYPE)
        bias = sgbt_ref[:, g:g + 1]
        for ch in range(ts // SG_CHUNK):
            r0 = ch * SG_CHUNK
            vg = v[r0:r0 + SG_CHUNK, g * gd:(g + 1) * gd].astype(MXU_DTYPE)
            f = jnp.dot(w, vg, preferred_element_type=F32) + bias
            o_ref[r0:r0 + SG_CHUNK, g * gd:(g + 1) * gd] = (u[r0:r0 + SG_CHUNK, g * gd:(g + 1) * gd]
                                                            * f).astype(o_ref.dtype)


def _gmlp(b_in, ln_g, ln_b, sg_w, sg_b):
    m, _ = b_in.shape
    c = SG_WIDTH
    ts = 256
    kern = functools.partial(_gmlp_kernel, ts=ts)
    return pl.pallas_call(
        kern,
        grid=(m // ts,),
        in_specs=[pl.BlockSpec((ts, 2 * c), lambda i: (i, 0)),
                  pl.BlockSpec((1, c), lambda i: (0, 0)),
                  pl.BlockSpec((1, c), lambda i: (0, 0)),
                  pl.BlockSpec((SG_GROUPS, SG_CHUNK, SG_CHUNK), lambda i: (0, 0, 0)),
                  pl.BlockSpec((SG_CHUNK, SG_GROUPS), lambda i: (0, 0))],
        out_specs=pl.BlockSpec((ts, c), lambda i: (i, 0)),
        out_shape=jax.ShapeDtypeStruct((m, c), MXU_DTYPE),
        compiler_params=_cparams(("parallel",)),
        name="gmlp_gating",
    )(b_in, ln_g.reshape(1, c), ln_b.reshape(1, c), sg_w, sg_b.T)


def _compress_kernel(p_ref, pelo_ref, pehi_ref, w1a_ref, w1b_ref, w2_ref, o_ref):
    p = p_ref[...]
    a = jnp.dot((p + pelo_ref[...]).astype(MXU_DTYPE), w1a_ref[...], preferred_element_type=F32)
    b = jnp.dot((p + pehi_ref[...]).astype(MXU_DTYPE), w1b_ref[...], preferred_element_type=F32)
    n = p.shape[0]
    b_next = pltpu.roll(b, n - 1, 0)
    hid = a + b_next
    hid = hid * _sigmoid(hid)
    o_ref[...] = jnp.dot(hid.astype(MXU_DTYPE), w2_ref[...], preferred_element_type=F32)


def _compress(pieces, pe, w1, w2):
    bsz, g, n_piece, width = pieces.shape
    half = CMP_BLOCK // 2
    pe_lo = pe[:half].reshape(1, width)
    pe_hi = pe[half:].reshape(1, width)
    w1a = w1[:half].reshape(width, CMP_HIDDEN).astype(MXU_DTYPE)
    w1b = w1[half:].reshape(width, CMP_HIDDEN).astype(MXU_DTYPE)
    return pl.pallas_call(
        _compress_kernel,
        grid=(bsz, g),
        in_specs=[pl.BlockSpec((None, None, n_piece, width), lambda b, gg: (b, gg, 0, 0)),
                  pl.BlockSpec((1, width), lambda b, gg: (0, 0)),
                  pl.BlockSpec((1, width), lambda b, gg: (0, 0)),
                  pl.BlockSpec((width, CMP_HIDDEN), lambda b, gg: (0, 0)),
                  pl.BlockSpec((width, CMP_HIDDEN), lambda b, gg: (0, 0)),
                  pl.BlockSpec((CMP_HIDDEN, HEAD_DIM), lambda b, gg: (0, 0))],
        out_specs=pl.BlockSpec((None, None, n_piece, HEAD_DIM), lambda b, gg: (b, gg, 0, 0)),
        out_shape=jax.ShapeDtypeStruct((bsz, g, n_piece, HEAD_DIM), F32),
        compiler_params=_cparams(("parallel", "parallel")),
        name="nsa_compress",
    )(pieces, pe_lo, pe_hi, w1a, w1b, w2.astype(MXU_DTYPE))


ROW_BLK = 32
M_INIT = -3e38
NSA_PROLOGUE_PHASES = 5
NSA_PHASE_LAG = 2


def _masked_softmax(s, mask):
    s = jnp.where(mask, s, NEG)
    m = jnp.max(s, axis=-1, keepdims=True)
    e = jnp.where(mask, jnp.exp(s - m), 0.0)
    return e / jnp.maximum(jnp.sum(e, axis=-1, keepdims=True), 1e-30)


def _nt_dot(a, b, precision=None):
    return lax.dot_general(a, b, (((1,), (1,)), ((), ())), preferred_element_type=F32, precision=precision)


def _nsa_kernel(*refs, tq, ck, nt):
    n_in = 12
    shared, scratch = refs[:n_in], refs[n_in:]
    per_tile = len(scratch) // nt
    tiles = [_nsa_tile(u, *shared, *scratch[u * per_tile:(u + 1) * per_tile], tq=tq, ck=ck, nt=nt)
             for u in range(nt)]
    loops = [None] * nt
    for step in range(NSA_PROLOGUE_PHASES + NSA_PHASE_LAG * (nt - 1)):
        for u in range(nt):
            if 0 <= step - NSA_PHASE_LAG * u < NSA_PROLOGUE_PHASES:
                loops[u] = next(tiles[u])

    def pair_body(i, carry):
        for stage in zip(*[stages(i) for _, stages in loops]):
            for run in stage:
                run()
        return carry

    lax.fori_loop(0, loops[0][0], pair_body, 0)
    for t in tiles:
        next(t, None)


def _nsa_tile(u, q_ref, kc_ref, vct_ref, ks_ref, vst_ref, kw_ref, vwt_ref, gate_ref, mapt_ref, bt_ref, bandt_ref,
              o_ref, sc_scr, pc_scr, sw_scr, pw_scr, sa_scr, sb_scr, pa_scr, pb_scr, m_scr, ala_scr, alb_scr, acc_scr,
              mw_scr, al_scr, accw_scr, rank_scr, *, tq, ck, nt):
    hpg = HEADS_PER_GROUP
    dh = HEAD_DIM
    aug = 2 * dh
    cols = hpg * tq
    n_blk = mapt_ref.shape[0]
    q0 = (pl.program_id(2) * nt + u) * tq
    tile_rows = pl.ds(u * tq, tq)
    qp = (q_ref[:, tile_rows, :] * np.float32(dh ** -0.5).astype(q_ref.dtype)).reshape(cols, aug)
    eye4 = ((lax.broadcasted_iota(jnp.int32, (cols, tq), 0) & (tq - 1))
            == lax.broadcasted_iota(jnp.int32, (cols, tq), 1)).astype(MXU_DTYPE)
    t_lane = q0 + lax.broadcasted_iota(jnp.int32, (1, tq), 1)

    def online(s_ref, p_ref, kr, m_ref, acc_ref, v_t):
        for h in range(hpg):
            cb = pl.ds(h * tq, tq)
            s = s_ref[kr, cb]
            m_old = m_ref[:, cb]
            m_new = jnp.maximum(m_old, jnp.max(s, axis=0, keepdims=True))
            p_ref[kr, cb] = jnp.exp(s - m_new).astype(p_ref.dtype)
            al_scr[:, cb] = jnp.exp(m_old - m_new)
            m_ref[:, cb] = m_new
        acc_ref[...] = al_scr[...] * acc_ref[...] + jnp.dot(v_t, p_ref[kr, :], preferred_element_type=F32)

    def finish(acc):
        return acc[:dh] / jnp.maximum(acc[dh:dh + 1], 1e-30)

    n_cmp = kc_ref.shape[0]
    sc_scr[...] = _nt_dot(kc_ref[...].astype(MXU_DTYPE), qp)
    wlen = WINDOW + tq
    w0 = pl.multiple_of(jnp.maximum(q0 - WINDOW, 0), tq)
    lhs_win = jnp.concatenate([qp, eye4], axis=1)
    rhs_win = jnp.concatenate([kw_ref[pl.ds(w0, wlen), :], bandt_ref[:, pl.ds(pl.multiple_of(q0 - w0, tq), tq)]],
                              axis=1)
    sw_scr[...] = _nt_dot(rhs_win, lhs_win)
    yield None

    cmask = (lax.broadcasted_iota(jnp.int32, (n_cmp, 1), 0) * CMP_STRIDE + (CMP_BLOCK - 1)) <= t_lane
    psum = jnp.zeros((n_cmp, tq), F32)
    for h in range(hpg):
        cb = pl.ds(h * tq, tq)
        s = jnp.where(cmask, sc_scr[:, cb], NEG)
        e = jnp.where(cmask, jnp.exp(s - jnp.max(s, axis=0, keepdims=True)), 0.0)
        p = e / jnp.maximum(jnp.sum(e, axis=0, keepdims=True), 1e-30)
        pc_scr[:, cb] = p.astype(pc_scr.dtype)
        psum = psum + p
    o_cmp = jnp.dot(vct_ref[...].astype(MXU_DTYPE), pc_scr[...], preferred_element_type=F32)
    yield None

    mw_scr[...] = jnp.full(mw_scr.shape, M_INIT, F32)
    accw_scr[...] = jnp.zeros(accw_scr.shape, F32)
    for k_lo in range(0, wlen, ck):
        nk = min(ck, wlen - k_lo)
        online(sw_scr, pw_scr, pl.ds(k_lo, nk), mw_scr, accw_scr,
               vwt_ref[:, pl.ds(pl.multiple_of(w0 + k_lo, tq), nk)])
    o_win = finish(accw_scr[...])
    yield None

    imp_t = jnp.dot(mapt_ref[...], psum, preferred_element_type=F32, precision=lax.Precision.HIGHEST)
    j_t = lax.broadcasted_iota(jnp.int32, (n_blk, tq), 0)
    cur_t = (q0 + lax.broadcasted_iota(jnp.int32, (n_blk, tq), 1)) >> 6
    forced = (j_t == 0) | (j_t == cur_t) | (j_t == cur_t - 1)
    score_t = jnp.where(j_t <= cur_t, jnp.where(forced, FORCE, imp_t), NEG)
    rank_scr[...] = score_t
    sel_rows = []
    for v in range(n_blk // 8):
        sj = score_t[8 * v:8 * v + 8]
        jv = j_t[8 * v:8 * v + 8]
        rank = jnp.zeros((8, tq), jnp.int32)
        for i in range(n_blk):
            ci = rank_scr[pl.ds(i, 1), :]
            if i < 8 * v:
                ahead = ci >= sj
            elif i > 8 * v + 7:
                ahead = ci > sj
            else:
                ahead = (ci > sj) | ((ci == sj) & (jv > i))
            rank = rank + ahead.astype(jnp.int32)
        sel_rows.append(jnp.where(rank < SLC_TOP_N, 0.0, NEG))
    yield None
    selb_t = jnp.concatenate([jnp.zeros((dh, tq), F32)] + sel_rows, axis=0).astype(MXU_DTYPE)
    selb = _nt_dot(eye4[:tq], selb_t).astype(MXU_DTYPE)
    lhs_slc = jnp.concatenate([(qp.reshape(hpg, tq, aug) + selb[None]).reshape(cols, aug), eye4], axis=1)

    n_full = q0 // ck

    def scores(c, s_ref):
        k0 = pl.multiple_of(jnp.minimum(c, n_full) * ck, ck)
        start = pl.multiple_of(jnp.clip(q0 - c * ck, -tq, ck) + tq, tq)
        rhs = jnp.concatenate([ks_ref[pl.ds(k0, ck), :], bt_ref[:, pl.ds(start, tq)]], axis=1)
        s_ref[...] = _nt_dot(rhs, lhs_slc)

    def softmax(s_ref, p_ref, a_ref):
        for h in range(hpg):
            cb = pl.ds(h * tq, tq)
            s = s_ref[:, cb]
            m_old = m_scr[:, cb]
            m_new = jnp.maximum(m_old, jnp.max(s, axis=0, keepdims=True))
            p_ref[:, cb] = jnp.exp(s - m_new).astype(p_ref.dtype)
            a_ref[:, cb] = jnp.exp(m_old - m_new)
            m_scr[:, cb] = m_new

    def accumulate(c, p_ref, a_ref):
        k0 = pl.multiple_of(jnp.clip(c, 0, n_full) * ck, ck)
        acc_scr[...] = a_ref[...] * acc_scr[...] + jnp.dot(vst_ref[:, pl.ds(k0, ck)], p_ref[...],
                                                           preferred_element_type=F32)

    m_scr[...] = jnp.full(m_scr.shape, M_INIT, F32)
    acc_scr[...] = jnp.zeros(acc_scr.shape, F32)
    pb_scr[...] = jnp.zeros(pb_scr.shape, pb_scr.dtype)
    alb_scr[...] = jnp.ones(alb_scr.shape, F32)
    scores(0, sa_scr)

    def pair_stages(i):
        k = 2 * i
        return (lambda: softmax(sa_scr, pa_scr, ala_scr),
                lambda: accumulate(k - 1, pb_scr, alb_scr),
                lambda: scores(k + 1, sb_scr),
                lambda: softmax(sb_scr, pb_scr, alb_scr),
                lambda: accumulate(k, pa_scr, ala_scr),
                lambda: scores(k + 2, sa_scr))

    n_pairs = (n_full + 2) // 2
    yield n_pairs, pair_stages
    accumulate(2 * n_pairs - 1, pb_scr, alb_scr)
    o_slc = finish(acc_scr[...])

    gl = _sigmoid(gate_ref[:, tile_rows])
    outs = []
    for h in range(hpg):
        cb = slice(h * tq, (h + 1) * tq)
        outs.append(gl[3 * h:3 * h + 1] * o_cmp[:, cb] + gl[3 * h + 1:3 * h + 2] * o_slc[:, cb]
                    + gl[3 * h + 2:3 * h + 3] * o_win[:, cb])
    o_t = jnp.concatenate(outs, axis=0).astype(MXU_DTYPE)
    o_ref[tile_rows, :] = _nt_dot(eye4[:tq], o_t).astype(o_ref.dtype)


def _slc_map_t(n_piece, n_slc, n_blk):
    r = CMP_BLOCK // CMP_STRIDE
    a = SLC_BLOCK // CMP_STRIDE
    m = np.zeros((n_blk, n_piece), np.float32)
    for n in range(n_piece - r + 1):
        for i in range(r):
            m[(n + i) // a, n] += 1.0
    return m


def _nsa_attention(rows_op, cols_op, kc, vc, gate_logits):
    bsz, _, seq, aug = rows_op.shape
    g, hpg, dh = N_KV_GROUPS, HEADS_PER_GROUP, HEAD_DIM
    n_piece = kc.shape[2]
    n_slc = seq // SLC_BLOCK
    n_blk = dh
    assert n_slc <= n_blk
    tq = 128
    ck = 256
    wlen = WINDOW + tq
    cols = hpg * tq

    kc_p = jnp.pad(kc, ((0, 0), (0, 0), (0, 0), (0, dh)))
    vc_t = vc.astype(MXU_DTYPE).transpose(0, 1, 3, 2)
    gate_t = gate_logits.reshape(bsz, seq, g, GATE_PAD)[..., :GATE_ROWS].transpose(0, 2, 3, 1)

    kk = np.arange(ck)[:, None]
    bt = np.where(kk <= np.arange(-tq, ck + tq)[None, :], 0.0, NEG).astype(np.float32)
    kw_i = np.arange(wlen)[:, None]
    tau = np.arange(wlen)[None, :]
    bandt = np.where((kw_i <= tau) & (kw_i > tau - WINDOW), 0.0, NEG).astype(np.float32)

    nt = ck // tq
    kern = functools.partial(_nsa_kernel, tq=tq, ck=ck, nt=nt)
    per_group = lambda r, c, first=0: pl.BlockSpec((None, None, r, c), lambda b, gg, i: (b, first + gg, 0, 0))
    const = lambda shape: pl.BlockSpec(shape, lambda b, gg, i: (0, 0))
    per_tile = pltpu.VMEM
    return pl.pallas_call(
        kern,
        grid=(bsz, g, seq // (nt * tq)),
        in_specs=[pl.BlockSpec((None, hpg, nt * tq, aug), lambda b, gg, i: (b, gg, i, 0)),
                  per_group(n_piece, aug), per_group(dh, n_piece),
                  per_group(seq, aug, g * hpg), per_group(aug, seq),
                  per_group(seq, aug, g * hpg + g), per_group(aug, seq, g),
                  pl.BlockSpec((None, None, GATE_ROWS, nt * tq), lambda b, gg, i: (b, gg, 0, i)),
                  const((n_blk, n_piece)), const((ck, ck + 2 * tq)), const((wlen, wlen))],
        out_specs=pl.BlockSpec((None, nt * tq, hpg * dh), lambda b, gg, i: (b, i, gg)),
        out_shape=jax.ShapeDtypeStruct((bsz, seq, g * hpg * dh), MXU_DTYPE),
        scratch_shapes=[per_tile((n_piece, cols), F32),
                        per_tile((n_piece, cols), MXU_DTYPE),
                        per_tile((wlen, cols), F32),
                        per_tile((wlen, cols), MXU_DTYPE),
                        per_tile((ck, cols), F32),
                        per_tile((ck, cols), F32),
                        per_tile((ck, cols), MXU_DTYPE),
                        per_tile((ck, cols), MXU_DTYPE),
                        per_tile((1, cols), F32),
                        per_tile((1, cols), F32),
                        per_tile((1, cols), F32),
                        per_tile((aug, cols), F32),
                        per_tile((1, cols), F32),
                        per_tile((1, cols), F32),
                        per_tile((aug, cols), F32),
                        per_tile((n_blk, tq), F32)] * nt,
        compiler_params=_cparams(("parallel", "parallel", "arbitrary")),
        name="nsa_attention",
    )(rows_op, kc_p, vc_t, rows_op, cols_op, rows_op, cols_op, gate_t,
      jnp.asarray(_slc_map_t(n_piece, n_slc, n_blk)), jnp.asarray(bt, MXU_DTYPE), jnp.asarray(bandt, MXU_DTYPE))


def _residual_and_next_norm(x_ref, r, nw_ref, next_w_ref, o_ref, hn_ref):
    ms = jnp.mean(r * r, axis=-1, keepdims=True)
    o = x_ref[...] + r * lax.rsqrt(ms + EPS) * nw_ref[...]
    o_ref[...] = o
    ms_o = jnp.mean(o * o, axis=-1, keepdims=True)
    hn_ref[...] = (o * lax.rsqrt(ms_o + EPS) * next_w_ref[...]).astype(hn_ref.dtype)


def _merge_kernel(ha_ref, hb_ref, hc_ref, gl_ref, x_ref, wa_ref, wb_ref, wc_ref, wo_ref, nw_ref, next_w_ref,
                  o_ref, hn_ref, mix):
    d = x_ref.shape[1]
    branches = ((ha_ref, wa_ref), (hb_ref, wb_ref), (hc_ref, wc_ref))
    for n, (h_ref, w_ref) in enumerate(branches):
        y = _sigmoid(gl_ref[:, n * d:(n + 1) * d]) * jnp.dot(h_ref[...], w_ref[...], preferred_element_type=F32)
        if n == 0:
            mix[...] = y
        else:
            mix[...] += y
    r = jnp.dot(mix[...].astype(MXU_DTYPE), wo_ref[...], preferred_element_type=F32)
    _residual_and_next_norm(x_ref, r, nw_ref, next_w_ref, o_ref, hn_ref)


def _merge(ha, hb, hc, merge_logits, x2d, w_a, w_b, w_c, w_o, norm_w, next_norm_w):
    m, d = x2d.shape
    c = ha.shape[1]
    tm = 256
    row = lambda i: (i, 0)
    resident = lambda shape: pl.BlockSpec(shape, lambda i: (0, 0), pipeline_mode=pl.Buffered(1))
    wc = lambda w: w.astype(MXU_DTYPE)
    return pl.pallas_call(
        _merge_kernel,
        grid=(m // tm,),
        in_specs=[pl.BlockSpec((tm, c), row), pl.BlockSpec((tm, c), row), pl.BlockSpec((tm, c), row),
                  pl.BlockSpec((tm, 3 * d), row), pl.BlockSpec((tm, d), row),
                  resident((c, d)), resident((c, d)), resident((c, d)), resident((d, d)),
                  pl.BlockSpec((1, d), lambda i: (0, 0)), pl.BlockSpec((1, d), lambda i: (0, 0))],
        out_specs=[pl.BlockSpec((tm, d), row), pl.BlockSpec((tm, d), row)],
        out_shape=[jax.ShapeDtypeStruct((m, d), F32), jax.ShapeDtypeStruct((m, d), MXU_DTYPE)],
        scratch_shapes=[pltpu.VMEM((tm, d), F32)],
        compiler_params=_cparams(("parallel",)),
        name="merge_out_proj",
    )(ha, hb, hc, merge_logits, x2d, wc(w_a), wc(w_b), wc(w_c), wc(w_o), norm_w.reshape(1, d),
      next_norm_w.reshape(1, d))


def _ffn_up_kernel(h_ref, halo_ref, wg_ref, wv_ref, cwg_ref, cwv_ref, cbg_ref, cbv_ref, o_ref,
                   hbuf, ugbuf, uvbuf, *, tm, seq):
    i = pl.program_id(0)
    j = pl.program_id(1)

    @pl.when(j == 0)
    def _():
        hbuf[pl.ds(FFN_HALO, tm), :] = h_ref[...]
        first = (i * tm) % seq == 0
        hbuf[pl.ds(0, FFN_HALO), :] = jnp.where(first, jnp.zeros_like(halo_ref[...]), halo_ref[...])

    lhs = hbuf[...]
    ugbuf[...] = jnp.dot(lhs, wg_ref[...], preferred_element_type=F32)
    uvbuf[...] = jnp.dot(lhs, wv_ref[...], preferred_element_type=F32)

    def conv(buf, cw_ref, cb_ref):
        return (cb_ref[...] + cw_ref[pl.ds(2, 1), :] * buf[pl.ds(FFN_HALO, tm), :]
                + cw_ref[pl.ds(1, 1), :] * buf[pl.ds(FFN_HALO - 1, tm), :]
                + cw_ref[pl.ds(0, 1), :] * buf[pl.ds(FFN_HALO - 2, tm), :])

    cg = conv(ugbuf, cwg_ref, cbg_ref)
    cv = conv(uvbuf, cwv_ref, cbv_ref)
    o_ref[...] = (cg * _sigmoid(cg) * cv).astype(o_ref.dtype)


def _ffn_up(h, w_up, conv_w, conv_b, seq):
    m, d = h.shape
    dff = w_up.shape[1] // 2
    tm = min(1024, seq)
    tn = _pick(dff, (512, 256, 128))
    nj = dff // tn
    per = tm // FFN_HALO
    kern = functools.partial(_ffn_up_kernel, tm=tm, seq=seq)
    cb = conv_b.reshape(1, 2 * dff)
    return pl.pallas_call(
        kern,
        grid=(m // tm, nj),
        in_specs=[pl.BlockSpec((tm, d), lambda i, j: (i, 0)),
                  pl.BlockSpec((FFN_HALO, d), lambda i, j: (jnp.maximum(i * per - 1, 0), 0)),
                  pl.BlockSpec((d, tn), lambda i, j: (0, j)),
                  pl.BlockSpec((d, tn), lambda i, j: (0, j + nj)),
                  pl.BlockSpec((3, tn), lambda i, j: (0, j)),
                  pl.BlockSpec((3, tn), lambda i, j: (0, j + nj)),
                  pl.BlockSpec((1, tn), lambda i, j: (0, j)),
                  pl.BlockSpec((1, tn), lambda i, j: (0, j + nj))],
        out_specs=pl.BlockSpec((tm, tn), lambda i, j: (i, j)),
        out_shape=jax.ShapeDtypeStruct((m, dff), MXU_DTYPE),
        scratch_shapes=[pltpu.VMEM((FFN_HALO + tm, d), MXU_DTYPE),
                        pltpu.VMEM((FFN_HALO + tm, tn), F32),
                        pltpu.VMEM((FFN_HALO + tm, tn), F32)],
        compiler_params=_cparams(("parallel", "arbitrary")),
        name="ffn_up_conv_gate",
    )(h, h, w_up, w_up, conv_w, conv_w, cb, cb)


def _ffn_down_kernel(a_ref, w_ref, x_ref, nw_ref, o_ref):
    r = jnp.dot(a_ref[...], w_ref[...], preferred_element_type=F32)
    ms = jnp.mean(r * r, axis=-1, keepdims=True)
    o_ref[...] = x_ref[...] + r * lax.rsqrt(ms + EPS) * nw_ref[...]


def _ffn_down_next_kernel(a_ref, w_ref, x_ref, nw_ref, next_w_ref, o_ref, hn_ref):
    r = jnp.dot(a_ref[...], w_ref[...], preferred_element_type=F32)
    _residual_and_next_norm(x_ref, r, nw_ref, next_w_ref, o_ref, hn_ref)


def _ffn_down(act, w_down, x2d, norm_w, next_norm_w=None):
    m, dff = act.shape
    d = w_down.shape[1]
    tm = 256
    row = pl.BlockSpec((tm, d), lambda i: (i, 0))
    vec = pl.BlockSpec((1, d), lambda i: (0, 0))
    in_specs = [pl.BlockSpec((tm, dff), lambda i: (i, 0)),
                pl.BlockSpec((dff, d), lambda i: (0, 0), pipeline_mode=pl.Buffered(1)),
                row, vec]
    args = [act, w_down, x2d, norm_w.reshape(1, d)]
    x_shape = jax.ShapeDtypeStruct((m, d), F32)
    if next_norm_w is None:
        kern, out_specs, out_shape = _ffn_down_kernel, row, x_shape
    else:
        kern, out_specs = _ffn_down_next_kernel, [row, row]
        out_shape = [x_shape, jax.ShapeDtypeStruct((m, d), MXU_DTYPE)]
        in_specs.append(vec)
        args.append(next_norm_w.reshape(1, d))
    return pl.pallas_call(
        kern,
        grid=(m // tm,),
        in_specs=in_specs,
        out_specs=out_specs,
        out_shape=out_shape,
        compiler_params=_cparams(("parallel",)),
        name="ffn_down_norm_res",
    )(*args)


def _gate_weight(w_gate):
    d = w_gate.shape[0]
    per = HEADS_PER_GROUP * 3
    w = w_gate.reshape(d, N_KV_GROUPS, per)
    w = jnp.pad(w, ((0, 0), (0, 0), (0, GATE_PAD - per)))
    return w.reshape(d, N_KV_GROUPS * GATE_PAD)


def _mixer_layer(x, h, p, l):
    bsz, seq, d = x.shape
    m = bsz * seq
    x2d = x.reshape(m, d)
    g, hpg, dh = N_KV_GROUPS, HEADS_PER_GROUP, HEAD_DIM
    a_in_w = 2 * CONV_WIDTH
    b_in_w = 2 * SG_WIDTH
    q_w = N_HEADS * dh
    kv_w = 6 * g * dh
    gate_w = 3 * N_HEADS
    o0 = 0
    o1 = o0 + a_in_w
    o2 = o1 + b_in_w
    o3 = o2 + q_w
    o4 = o3 + kv_w
    o5 = o4 + gate_w
    w_in = p["w_in"][l]

    wc = lambda w: w.astype(MXU_DTYPE)
    a_in = _matmul(h, wc(w_in[:, o0:o1]), name="proj_a")
    b_in = _matmul(h, wc(w_in[:, o1:o2]), name="proj_b")
    h3 = h.reshape(bsz, seq, d)
    aug = 2 * dh
    w_q = w_in[:, o2:o3].reshape(d, N_HEADS, dh)
    w_kv = w_in[:, o3:o4].reshape(d, 6, g, dh)
    pad_lanes = lambda w: jnp.pad(w, ((0, 0), (0, 0), (0, aug - dh))).reshape(d, -1)
    w_rows = wc(jnp.concatenate([pad_lanes(w_q), pad_lanes(w_kv[:, 2]), pad_lanes(w_kv[:, 4])], axis=1))
    onehot = np.zeros((seq, aug), np.float32)
    onehot[np.arange(seq), dh + np.arange(seq) // SLC_BLOCK] = 1.0
    slc_slabs = (N_HEADS * aug // PROJ_SLAB, (N_HEADS + g) * aug // PROJ_SLAB)
    rows_op = _proj_planes(h3, w_rows, jnp.asarray(onehot), MXU_DTYPE, aug, slc_slabs, "proj_attn_rows")
    w_cmp = wc(jnp.concatenate([w_kv[:, 0].reshape(d, g * dh), w_kv[:, 1].reshape(d, g * dh)], axis=1))
    cmp_op = _proj_planes(h3, w_cmp, jnp.zeros((seq, dh), F32), F32, dh, (0, 0), "proj_attn_cmp")
    pad_rows = lambda w: jnp.pad(w.transpose(1, 2, 0), ((0, 0), (0, aug - dh), (0, 0))).reshape(-1, d)
    w_cols = wc(jnp.concatenate([pad_rows(w_kv[:, 3]), pad_rows(w_kv[:, 5])], axis=0))
    ones_rows = np.zeros((2 * g, aug, min(512, seq)), np.float32)
    ones_rows[:, dh] = 1.0
    cols_op = _proj_cols(h3, w_cols, jnp.asarray(ones_rows.reshape(2 * g * aug, -1)), 2 * g)
    gate_logits = _matmul(h, wc(_gate_weight(w_in[:, o4:o5])), name="proj_gate")
    merge_logits = _matmul(h, wc(w_in[:, o5:]), name="proj_merge")

    h_a = _conformer(a_in.reshape(bsz, seq, a_in_w), p["conv_a_w"][l], p["conv_a_b"][l],
                     p["ln_a_g"][l], p["ln_a_b"][l]).reshape(m, CONV_WIDTH)
    h_b = _gmlp(b_in, p["ln_b_g"][l], p["ln_b_b"][l], p["sg_w"][l], p["sg_b"][l])

    n_piece = seq // CMP_STRIDE
    kc = _compress(cmp_op[:, :g].reshape(bsz, g, n_piece, CMP_STRIDE * dh), p["cmp_pe_k"][l], p["cmp_w1_k"][l],
                   p["cmp_w2_k"][l])
    vc = _compress(cmp_op[:, g:].reshape(bsz, g, n_piece, CMP_STRIDE * dh), p["cmp_pe_v"][l], p["cmp_w1_v"][l],
                   p["cmp_w2_v"][l])
    o = _nsa_attention(rows_op, cols_op, kc, vc, gate_logits.reshape(bsz, seq, g * GATE_PAD))
    h_c = o.reshape(m, N_HEADS * dh)

    x_new, h_ffn = _merge(h_a, h_b, h_c, merge_logits, x2d, p["w_a_out"][l], p["w_b_out"][l], p["w_c_out"][l],
                          p["w_o"][l], p["norm_mix_post"][l], p["norm_ffn_pre"][l])
    return x_new.reshape(bsz, seq, d), h_ffn


def _ffn_layer(x, h, p, l, next_norm_w):
    bsz, seq, d = x.shape
    m = bsz * seq
    act = _ffn_up(h, p["w_up"][l].astype(MXU_DTYPE), p["ffn_conv_w"][l], p["ffn_conv_b"][l], seq)
    out = _ffn_down(act, p["w_down"][l].astype(MXU_DTYPE), x.reshape(m, d), p["norm_ffn_post"][l], next_norm_w)
    if next_norm_w is None:
        return out.reshape(bsz, seq, d), None
    return out[0].reshape(bsz, seq, d), out[1]


def kernel(x, norm_mix_pre, norm_mix_post, norm_ffn_pre, norm_ffn_post, w_in, conv_a_w, conv_a_b, ln_a_g, ln_a_b,
           w_a_out, ln_b_g, ln_b_b, sg_w, sg_b, w_b_out, cmp_pe_k, cmp_w1_k, cmp_w2_k, cmp_pe_v, cmp_w1_v,
           cmp_w2_v, w_c_out, w_o, w_up, ffn_conv_w, ffn_conv_b, w_down):
    p = dict(norm_mix_pre=norm_mix_pre, norm_mix_post=norm_mix_post, norm_ffn_pre=norm_ffn_pre,
             norm_ffn_post=norm_ffn_post, w_in=w_in, conv_a_w=conv_a_w, conv_a_b=conv_a_b, ln_a_g=ln_a_g,
             ln_a_b=ln_a_b, w_a_out=w_a_out, ln_b_g=ln_b_g, ln_b_b=ln_b_b, sg_w=sg_w, sg_b=sg_b, w_b_out=w_b_out,
             cmp_pe_k=cmp_pe_k, cmp_w1_k=cmp_w1_k, cmp_w2_k=cmp_w2_k, cmp_pe_v=cmp_pe_v, cmp_w1_v=cmp_w1_v,
             cmp_w2_v=cmp_w2_v, w_c_out=w_c_out, w_o=w_o, w_up=w_up, ffn_conv_w=ffn_conv_w,
             ffn_conv_b=ffn_conv_b, w_down=w_down)
    depth = w_in.shape[0]
    h = _rmsnorm_cast(x.reshape(-1, x.shape[-1]), norm_mix_pre[0])
    for l in range(depth):
        x, h = _mixer_layer(x, h, p, l)
        x, h = _ffn_layer(x, h, p, l, norm_mix_pre[l + 1] if l + 1 < depth else None)
    return x
```

```python
import functools

import numpy as np
import jax
import jax.numpy as jnp
from jax import lax
from jax.experimental import pallas as pl
from jax.experimental.pallas import tpu as pltpu

F32 = jnp.float32
MXU_DTYPE = jnp.bfloat16

EPS = 1e-6
NEG = -1e30
FORCE = 1e4

CONV_WIDTH = 1024
DW_CONV_SIZE = 31
CONV_HALO = 32
CONV_SUB = 64
SG_WIDTH = 1024
SG_CHUNK = 128
SG_GROUPS = 8
N_HEADS = 16
N_KV_GROUPS = 4
HEADS_PER_GROUP = 4
HEAD_DIM = 64
CMP_BLOCK = 32
CMP_STRIDE = 16
CMP_HIDDEN = 256
SLC_BLOCK = 64
SLC_TOP_N = 16
WINDOW = 512
D_FF = 5632
FFN_HALO = 16
GATE_PAD = 128
PROJ_SLAB = 512
GATE_ROWS = 16

VMEM_LIMIT = 56 * 1024 * 1024


def _cparams(sem):
    return pltpu.CompilerParams(dimension_semantics=sem, vmem_limit_bytes=VMEM_LIMIT)


def _sigmoid(x):
    return jax.nn.sigmoid(x)


def _rmsnorm_kernel(x_ref, w_ref, o_ref):
    x = x_ref[...]
    ms = jnp.mean(x * x, axis=-1, keepdims=True)
    o_ref[...] = (x * lax.rsqrt(ms + EPS) * w_ref[...]).astype(o_ref.dtype)


def _rmsnorm_cast(x2d, w):
    m, d = x2d.shape
    tm = min(512, m)
    return pl.pallas_call(
        _rmsnorm_kernel,
        grid=(m // tm,),
        in_specs=[pl.BlockSpec((tm, d), lambda i: (i, 0)),
                  pl.BlockSpec((1, d), lambda i: (0, 0))],
        out_specs=pl.BlockSpec((tm, d), lambda i: (i, 0)),
        out_shape=jax.ShapeDtypeStruct((m, d), MXU_DTYPE),
        compiler_params=_cparams(("parallel",)),
        name="rmsnorm_cast",
    )(x2d, w.reshape(1, d))


def _mm_kernel(a_ref, w_ref, o_ref):
    o_ref[...] = jnp.dot(a_ref[...], w_ref[...], preferred_element_type=F32).astype(o_ref.dtype)


def _pick(n, cands):
    for c in cands:
        if n % c == 0:
            return c
    raise ValueError(f"no tile for {n}")


def _matmul(a, w, out_dtype=F32, name="matmul"):
    m, k = a.shape
    n = w.shape[1]
    tm = min(2048, m)
    tn = _pick(n, (1024, 512, 384, 256, 128))
    return pl.pallas_call(
        _mm_kernel,
        grid=(m // tm, n // tn),
        in_specs=[pl.BlockSpec((tm, k), lambda i, j: (i, 0)),
                  pl.BlockSpec((k, tn), lambda i, j: (0, j))],
        out_specs=pl.BlockSpec((tm, tn), lambda i, j: (i, j)),
        out_shape=jax.ShapeDtypeStruct((m, n), out_dtype),
        compiler_params=_cparams(("parallel", "arbitrary")),
        name=name,
    )(a, w)


def _proj_planes_kernel(h_ref, w_ref, add_ref, o_ref, *, add_lo, add_hi, width):
    j = pl.program_id(2)
    res = jnp.dot(h_ref[...], w_ref[...], preferred_element_type=F32)
    if add_hi > add_lo:
        add = jnp.where((j >= add_lo) & (j < add_hi), add_ref[...], 0.0)
    for p in range(o_ref.shape[0]):
        piece = res[:, p * width:(p + 1) * width]
        o_ref[p] = (piece + add if add_hi > add_lo else piece).astype(o_ref.dtype)


def _proj_planes(h3, w, add, out_dtype, width, add_range, name):
    bsz, seq, d = h3.shape
    n = w.shape[1]
    slab = PROJ_SLAB
    per = slab // width
    tm = min(1024, seq)
    kern = functools.partial(_proj_planes_kernel, add_lo=add_range[0], add_hi=add_range[1], width=width)
    return pl.pallas_call(
        kern,
        grid=(bsz, seq // tm, n // slab),
        in_specs=[pl.BlockSpec((None, tm, d), lambda b, i, j: (b, i, 0)),
                  pl.BlockSpec((d, slab), lambda b, i, j: (0, j)),
                  pl.BlockSpec((tm, width), lambda b, i, j: (i, 0))],
        out_specs=pl.BlockSpec((None, per, tm, width), lambda b, i, j: (b, j, i, 0)),
        out_shape=jax.ShapeDtypeStruct((bsz, n // width, seq, width), out_dtype),
        compiler_params=_cparams(("parallel", "parallel", "arbitrary")),
        name=name,
    )(h3, w, add)


def _proj_cols_kernel(h_ref, wt_ref, add_ref, o_ref):
    res = lax.dot_general(wt_ref[...], h_ref[...], (((1,), (1,)), ((), ())), preferred_element_type=F32)
    o_ref[...] = (res + add_ref[...]).astype(o_ref.dtype).reshape(o_ref.shape)


def _proj_cols(h3, w_t, add, planes):
    bsz, seq, d = h3.shape
    n = w_t.shape[0]
    tm = min(512, seq)
    return pl.pallas_call(
        _proj_cols_kernel,
        grid=(bsz, seq // tm),
        in_specs=[pl.BlockSpec((None, tm, d), lambda b, i: (b, i, 0)),
                  pl.BlockSpec((n, d), lambda b, i: (0, 0), pipeline_mode=pl.Buffered(1)),
                  pl.BlockSpec((n, tm), lambda b, i: (0, 0), pipeline_mode=pl.Buffered(1))],
        out_specs=pl.BlockSpec((None, planes, n // planes, tm), lambda b, i: (b, 0, 0, i)),
        out_shape=jax.ShapeDtypeStruct((bsz, planes, n // planes, seq), MXU_DTYPE),
        compiler_params=_cparams(("parallel", "parallel")),
        name="proj_attn_cols",
    )(h3, w_t, add)


def _conformer_kernel(cur_ref, halo_ref, cw_ref, cb_ref, lg_ref, lb_ref, o_ref, hbuf, cbuf, *, ts):
    c = CONV_WIDTH
    i = pl.program_id(1)
    cur = cur_ref[...]
    hbuf[pl.ds(CONV_HALO, ts), :] = cur[:, :c] * _sigmoid(cur[:, c:])
    hal = halo_ref[...]
    hh = hal[:, :c] * _sigmoid(hal[:, c:])
    hbuf[pl.ds(0, CONV_HALO), :] = jnp.where(i == 0, 0.0, hh)

    first = CONV_HALO - (DW_CONV_SIZE - 1)
    span = CONV_SUB + CONV_HALO

    def blk_body(idx, carry):
        off = pl.multiple_of((idx // (ts // CONV_SUB)) * 128, 128)
        r0 = pl.multiple_of((idx % (ts // CONV_SUB)) * CONV_SUB, CONV_SUB)
        x = hbuf[pl.ds(r0, span), pl.ds(off, 128)]
        acc = jnp.broadcast_to(cb_ref[:, pl.ds(off, 128)], (CONV_SUB, 128))
        for r in range(8):
            xr = x if r == 0 else pltpu.roll(x, span - r, 0)
            for a in range(CONV_HALO // 8 + 1):
                k = 8 * a + r - first
                if 0 <= k < DW_CONV_SIZE:
                    acc = acc + cw_ref[pl.ds(k, 1), pl.ds(off, 128)] * xr[8 * a:8 * a + CONV_SUB]
        cbuf[pl.ds(r0, CONV_SUB), pl.ds(off, 128)] = acc
        return carry

    lax.fori_loop(0, (c // 128) * (ts // CONV_SUB), blk_body, 0)
    y = cbuf[...]
    mean = jnp.mean(y, axis=-1, keepdims=True)
    yc = y - mean
    var = jnp.mean(yc * yc, axis=-1, keepdims=True)
    z = yc * lax.rsqrt(var + EPS) * lg_ref[...] + lb_ref[...]
    o_ref[...] = (z * _sigmoid(z)).astype(o_ref.dtype)


def _conformer(a_in, conv_w, conv_b, ln_g, ln_b):
    bsz, seq, _ = a_in.shape
    c = CONV_WIDTH
    ts = min(256, seq)
    per = ts // CONV_HALO
    kern = functools.partial(_conformer_kernel, ts=ts)
    return pl.pallas_call(
        kern,
        grid=(bsz, seq // ts),
        in_specs=[pl.BlockSpec((None, ts, 2 * c), lambda b, i: (b, i, 0)),
                  pl.BlockSpec((None, CONV_HALO, 2 * c), lambda b, i: (b, jnp.maximum(i * per - 1, 0), 0)),
                  pl.BlockSpec((DW_CONV_SIZE, c), lambda b, i: (0, 0)),
                  pl.BlockSpec((1, c), lambda b, i: (0, 0)),
                  pl.BlockSpec((1, c), lambda b, i: (0, 0)),
                  pl.BlockSpec((1, c), lambda b, i: (0, 0))],
        out_specs=pl.BlockSpec((None, ts, c), lambda b, i: (b, i, 0)),
        out_shape=jax.ShapeDtypeStruct((bsz, seq, c), MXU_DTYPE),
        scratch_shapes=[pltpu.VMEM((CONV_HALO + ts, c), F32), pltpu.VMEM((ts, c), F32)],
        compiler_params=_cparams(("parallel", "parallel")),
        name="conformer_conv",
    )(a_in, a_in, conv_w, conv_b.reshape(1, c), ln_g.reshape(1, c), ln_b.reshape(1, c))


def _gmlp_kernel(x_ref, lg_ref, lb_ref, sgw_ref, sgbt_ref, o_ref, *, ts):
    c = SG_WIDTH
    gd = c // SG_GROUPS
    x = x_ref[...]
    z = 0.5 * x * (1.0 + lax.erf(x * np.float32(np.sqrt(0.5))))
    u = z[:, :c]
    v = z[:, c:]
    mean = jnp.mean(v, axis=-1, keepdims=True)
    vc = v - mean
    var = jnp.mean(vc * vc, axis=-1, keepdims=True)
    v = vc * lax.rsqrt(var + EPS) * lg_ref[...] + lb_ref[...]
    row = lax.broadcasted_iota(jnp.int32, (SG_CHUNK, SG_CHUNK), 0)
    col = lax.broadcasted_iota(jnp.int32, (SG_CHUNK, SG_CHUNK), 1)
    causal = col <= row
    for g in range(SG_GROUPS):
        w = jnp.where(causal, sgw_ref[g], 0.0).astype(MXU_DTYPE)
        bias = sgbt_ref[:, g:g + 1]
        for ch in range(ts // SG_CHUNK):
            r0 = ch * SG_CHUNK
            vg = v[r0:r0 + SG_CHUNK, g * gd:(g + 1) * gd].astype(MXU_DTYPE)
            f = jnp.dot(w, vg, preferred_element_type=F32) + bias
            o_ref[r0:r0 + SG_CHUNK, g * gd:(g + 1) * gd] = (u[r0:r0 + SG_CHUNK, g * gd:(g + 1) * gd]
                                                            * f).astype(o_ref.dtype)


def _gmlp(b_in, ln_g, ln_b, sg_w, sg_b):
    m, _ = b_in.shape
    c = SG_WIDTH
    ts = 256
    kern = functools.partial(_gmlp_kernel, ts=ts)
    return pl.pallas_call(
        kern,
        grid=(m // ts,),
        in_specs=[pl.BlockSpec((ts, 2 * c), lambda i: (i, 0)),
                  pl.BlockSpec((1, c), lambda i: (0, 0)),
                  pl.BlockSpec((1, c), lambda i: (0, 0)),
                  pl.BlockSpec((SG_GROUPS, SG_CHUNK, SG_CHUNK), lambda i: (0, 0, 0)),
                  pl.BlockSpec((SG_CHUNK, SG_GROUPS), lambda i: (0, 0))],
        out_specs=pl.BlockSpec((ts, c), lambda i: (i, 0)),
        out_shape=jax.ShapeDtypeStruct((m, c), MXU_DTYPE),
        compiler_params=_cparams(("parallel",)),
        name="gmlp_gating",
    )(b_in, ln_g.reshape(1, c), ln_b.reshape(1, c), sg_w, sg_b.T)


def _compress_kernel(p_ref, pelo_ref, pehi_ref, w1a_ref, w1b_ref, w2_ref, o_ref):
    p = p_ref[...]
    a = jnp.dot((p + pelo_ref[...]).astype(MXU_DTYPE), w1a_ref[...], preferred_element_type=F32)
    b = jnp.dot((p + pehi_ref[...]).astype(MXU_DTYPE), w1b_ref[...], preferred_element_type=F32)
    n = p.shape[0]
    b_next = pltpu.roll(b, n - 1, 0)
    hid = a + b_next
    hid = hid * _sigmoid(hid)
    o_ref[...] = jnp.dot(hid.astype(MXU_DTYPE), w2_ref[...], preferred_element_type=F32)


def _compress(pieces, pe, w1, w2):
    bsz, g, n_piece, width = pieces.shape
    half = CMP_BLOCK // 2
    pe_lo = pe[:half].reshape(1, width)
    pe_hi = pe[half:].reshape(1, width)
    w1a = w1[:half].reshape(width, CMP_HIDDEN).astype(MXU_DTYPE)
    w1b = w1[half:].reshape(width, CMP_HIDDEN).astype(MXU_DTYPE)
    return pl.pallas_call(
        _compress_kernel,
        grid=(bsz, g),
        in_specs=[pl.BlockSpec((None, None, n_piece, width), lambda b, gg: (b, gg, 0, 0)),
                  pl.BlockSpec((1, width), lambda b, gg: (0, 0)),
                  pl.BlockSpec((1, width), lambda b, gg: (0, 0)),
                  pl.BlockSpec((width, CMP_HIDDEN), lambda b, gg: (0, 0)),
                  pl.BlockSpec((width, CMP_HIDDEN), lambda b, gg: (0, 0)),
                  pl.BlockSpec((CMP_HIDDEN, HEAD_DIM), lambda b, gg: (0, 0))],
        out_specs=pl.BlockSpec((None, None, n_piece, HEAD_DIM), lambda b, gg: (b, gg, 0, 0)),
        out_shape=jax.ShapeDtypeStruct((bsz, g, n_piece, HEAD_DIM), F32),
        compiler_params=_cparams(("parallel", "parallel")),
        name="nsa_compress",
    )(pieces, pe_lo, pe_hi, w1a, w1b, w2.astype(MXU_DTYPE))


ROW_BLK = 32
M_INIT = -3e38
NSA_PROLOGUE_PHASES = 5
NSA_PHASE_LAG = 2


def _masked_softmax(s, mask):
    s = jnp.where(mask, s, NEG)
    m = jnp.max(s, axis=-1, keepdims=True)
    e = jnp.where(mask, jnp.exp(s - m), 0.0)
    return e / jnp.maximum(jnp.sum(e, axis=-1, keepdims=True), 1e-30)


def _nt_dot(a, b, precision=None):
    return lax.dot_general(a, b, (((1,), (1,)), ((), ())), preferred_element_type=F32, precision=precision)


def _nsa_kernel(*refs, tq, ck, nt):
    n_in = 12
    shared, scratch = refs[:n_in], refs[n_in:]
    per_tile = len(scratch) // nt
    tiles = [_nsa_tile(u, *shared, *scratch[u * per_tile:(u + 1) * per_tile], tq=tq, ck=ck, nt=nt)
             for u in range(nt)]
    loops = [None] * nt
    for step in range(NSA_PROLOGUE_PHASES + NSA_PHASE_LAG * (nt - 1)):
        for u in range(nt):
            if 0 <= step - NSA_PHASE_LAG * u < NSA_PROLOGUE_PHASES:
                loops[u] = next(tiles[u])

    def pair_body(i, carry):
        for stage in zip(*[stages(i) for _, stages in loops]):
            for run in stage:
                run()
        return carry

    lax.fori_loop(0, loops[0][0], pair_body, 0)
    for t in tiles:
        next(t, None)


def _nsa_tile(u, q_ref, kc_ref, vct_ref, ks_ref, vst_ref, kw_ref, vwt_ref, gate_ref, mapt_ref, bt_ref, bandt_ref,
              o_ref, sc_scr, pc_scr, sw_scr, pw_scr, sa_scr, sb_scr, pa_scr, pb_scr, m_scr, ala_scr, alb_scr, acc_scr,
              mw_scr, al_scr, accw_scr, rank_scr, *, tq, ck, nt):
    hpg = HEADS_PER_GROUP
    dh = HEAD_DIM
    aug = 2 * dh
    cols = hpg * tq
    n_blk = mapt_ref.shape[0]
    q0 = (pl.program_id(2) * nt + u) * tq
    tile_rows = pl.ds(u * tq, tq)
    qp = (q_ref[:, tile_rows, :] * np.float32(dh ** -0.5).astype(q_ref.dtype)).reshape(cols, aug)
    eye4 = ((lax.broadcasted_iota(jnp.int32, (cols, tq), 0) & (tq - 1))
            == lax.broadcasted_iota(jnp.int32, (cols, tq), 1)).astype(MXU_DTYPE)
    t_lane = q0 + lax.broadcasted_iota(jnp.int32, (1, tq), 1)

    def online(s_ref, p_ref, kr, m_ref, acc_ref, v_t):
        for h in range(hpg):
            cb = pl.ds(h * tq, tq)
            s = s_ref[kr, cb]
            m_old = m_ref[:, cb]
            m_new = jnp.maximum(m_old, jnp.max(s, axis=0, keepdims=True))
            p_ref[kr, cb] = jnp.exp(s - m_new).astype(p_ref.dtype)
            al_scr[:, cb] = jnp.exp(m_old - m_new)
            m_ref[:, cb] = m_new
        acc_ref[...] = al_scr[...] * acc_ref[...] + jnp.dot(v_t, p_ref[kr, :], preferred_element_type=F32)

    def finish(acc):
        return acc[:dh] / jnp.maximum(acc[dh:dh + 1], 1e-30)

    n_cmp = kc_ref.shape[0]
    sc_scr[...] = _nt_dot(kc_ref[...].astype(MXU_DTYPE), qp)
    wlen = WINDOW + tq
    w0 = pl.multiple_of(jnp.maximum(q0 - WINDOW, 0), tq)
    lhs_win = jnp.concatenate([qp, eye4], axis=1)
    rhs_win = jnp.concatenate([kw_ref[pl.ds(w0, wlen), :], bandt_ref[:, pl.ds(pl.multiple_of(q0 - w0, tq), tq)]],
                              axis=1)
    sw_scr[...] = _nt_dot(rhs_win, lhs_win)
    yield None

    cmask = (lax.broadcasted_iota(jnp.int32, (n_cmp, 1), 0) * CMP_STRIDE + (CMP_BLOCK - 1)) <= t_lane
    psum = jnp.zeros((n_cmp, tq), F32)
    for h in range(hpg):
        cb = pl.ds(h * tq, tq)
        s = jnp.where(cmask, sc_scr[:, cb], NEG)
        e = jnp.where(cmask, jnp.exp(s - jnp.max(s, axis=0, keepdims=True)), 0.0)
        p = e / jnp.maximum(jnp.sum(e, axis=0, keepdims=True), 1e-30)
        pc_scr[:, cb] = p.astype(pc_scr.dtype)
        psum = psum + p
    o_cmp = jnp.dot(vct_ref[...].astype(MXU_DTYPE), pc_scr[...], preferred_element_type=F32)
    yield None

    mw_scr[...] = jnp.full(mw_scr.shape, M_INIT, F32)
    accw_scr[...] = jnp.zeros(accw_scr.shape, F32)
    for k_lo in range(0, wlen, ck):
        nk = min(ck, wlen - k_lo)
        online(sw_scr, pw_scr, pl.ds(k_lo, nk), mw_scr, accw_scr,
               vwt_ref[:, pl.ds(pl.multiple_of(w0 + k_lo, tq), nk)])
    o_win = finish(accw_scr[...])
    yield None

    imp_t = jnp.dot(mapt_ref[...], psum, preferred_element_type=F32, precision=lax.Precision.HIGHEST)
    j_t = lax.broadcasted_iota(jnp.int32, (n_blk, tq), 0)
    cur_t = (q0 + lax.broadcasted_iota(jnp.int32, (n_blk, tq), 1)) >> 6
    forced = (j_t == 0) | (j_t == cur_t) | (j_t == cur_t - 1)
    score_t = jnp.where(j_t <= cur_t, jnp.where(forced, FORCE, imp_t), NEG)
    rank_scr[...] = score_t
    sel_rows = []
    for v in range(n_blk // 8):
        sj = score_t[8 * v:8 * v + 8]
        jv = j_t[8 * v:8 * v + 8]
        rank = jnp.zeros((8, tq), jnp.int32)
        for i in range(n_blk):
            ci = rank_scr[pl.ds(i, 1), :]
            if i < 8 * v:
                ahead = ci >= sj
            elif i > 8 * v + 7:
                ahead = ci > sj
            else:
                ahead = (ci > sj) | ((ci == sj) & (jv > i))
            rank = rank + ahead.astype(jnp.int32)
        sel_rows.append(jnp.where(rank < SLC_TOP_N, 0.0, NEG))
    yield None
    selb_t = jnp.concatenate([jnp.zeros((dh, tq), F32)] + sel_rows, axis=0).astype(MXU_DTYPE)
    selb = _nt_dot(eye4[:tq], selb_t).astype(MXU_DTYPE)
    lhs_slc = jnp.concatenate([(qp.reshape(hpg, tq, aug) + selb[None]).reshape(cols, aug), eye4], axis=1)

    n_full = q0 // ck

    def scores(c, s_ref):
        k0 = pl.multiple_of(jnp.minimum(c, n_full) * ck, ck)
        start = pl.multiple_of(jnp.clip(q0 - c * ck, -tq, ck) + tq, tq)
        rhs = jnp.concatenate([ks_ref[pl.ds(k0, ck), :], bt_ref[:, pl.ds(start, tq)]], axis=1)
        s_ref[...] = _nt_dot(rhs, lhs_slc)

    def softmax(s_ref, p_ref, a_ref):
        for h in range(hpg):
            cb = pl.ds(h * tq, tq)
            s = s_ref[:, cb]
            m_old = m_scr[:, cb]
            m_new = jnp.maximum(m_old, jnp.max(s, axis=0, keepdims=True))
            p_ref[:, cb] = jnp.exp(s - m_new).astype(p_ref.dtype)
            a_ref[:, cb] = jnp.exp(m_old - m_new)
            m_scr[:, cb] = m_new

    def accumulate(c, p_ref, a_ref):
        k0 = pl.multiple_of(jnp.clip(c, 0, n_full) * ck, ck)
        acc_scr[...] = a_ref[...] * acc_scr[...] + jnp.dot(vst_ref[:, pl.ds(k0, ck)], p_ref[...],
                                                           preferred_element_type=F32)

    m_scr[...] = jnp.full(m_scr.shape, M_INIT, F32)
    acc_scr[...] = jnp.zeros(acc_scr.shape, F32)
    pb_scr[...] = jnp.zeros(pb_scr.shape, pb_scr.dtype)
    alb_scr[...] = jnp.ones(alb_scr.shape, F32)
    scores(0, sa_scr)

    def pair_stages(i):
        k = 2 * i
        return (lambda: softmax(sa_scr, pa_scr, ala_scr),
                lambda: accumulate(k - 1, pb_scr, alb_scr),
                lambda: scores(k + 1, sb_scr),
                lambda: softmax(sb_scr, pb_scr, alb_scr),
                lambda: accumulate(k, pa_scr, ala_scr),
                lambda: scores(k + 2, sa_scr))

    n_pairs = (n_full + 2) // 2
    yield n_pairs, pair_stages
    accumulate(2 * n_pairs - 1, pb_scr, alb_scr)
    o_slc = finish(acc_scr[...])

    gl = _sigmoid(gate_ref[:, tile_rows])
    outs = []
    for h in range(hpg):
        cb = slice(h * tq, (h + 1) * tq)
        outs.append(gl[3 * h:3 * h + 1] * o_cmp[:, cb] + gl[3 * h + 1:3 * h + 2] * o_slc[:, cb]
                    + gl[3 * h + 2:3 * h + 3] * o_win[:, cb])
    o_t = jnp.concatenate(outs, axis=0).astype(MXU_DTYPE)
    o_ref[tile_rows, :] = _nt_dot(eye4[:tq], o_t).astype(o_ref.dtype)


def _slc_map_t(n_piece, n_slc, n_blk):
    r = CMP_BLOCK // CMP_STRIDE
    a = SLC_BLOCK // CMP_STRIDE
    m = np.zeros((n_blk, n_piece), np.float32)
    for n in range(n_piece - r + 1):
        for i in range(r):
            m[(n + i) // a, n] += 1.0
    return m


def _nsa_attention(rows_op, cols_op, kc, vc, gate_logits):
    bsz, _, seq, aug = rows_op.shape
    g, hpg, dh = N_KV_GROUPS, HEADS_PER_GROUP, HEAD_DIM
    n_piece = kc.shape[2]
    n_slc = seq // SLC_BLOCK
    n_blk = dh
    assert n_slc <= n_blk
    tq = 128
    ck = 256
    wlen = WINDOW + tq
    cols = hpg * tq

    kc_p = jnp.pad(kc, ((0, 0), (0, 0), (0, 0), (0, dh)))
    vc_t = vc.astype(MXU_DTYPE).transpose(0, 1, 3, 2)
    gate_t = gate_logits.reshape(bsz, seq, g, GATE_PAD)[..., :GATE_ROWS].transpose(0, 2, 3, 1)

    kk = np.arange(ck)[:, None]
    bt = np.where(kk <= np.arange(-tq, ck + tq)[None, :], 0.0, NEG).astype(np.float32)
    kw_i = np.arange(wlen)[:, None]
    tau = np.arange(wlen)[None, :]
    bandt = np.where((kw_i <= tau) & (kw_i > tau - WINDOW), 0.0, NEG).astype(np.float32)

    nt = ck // tq
    kern = functools.partial(_nsa_kernel, tq=tq, ck=ck, nt=nt)
    per_group = lambda r, c, first=0: pl.BlockSpec((None, None, r, c), lambda b, gg, i: (b, first + gg, 0, 0))
    const = lambda shape: pl.BlockSpec(shape, lambda b, gg, i: (0, 0))
    per_tile = pltpu.VMEM
    return pl.pallas_call(
        kern,
        grid=(bsz, g, seq // (nt * tq)),
        in_specs=[pl.BlockSpec((None, hpg, nt * tq, aug), lambda b, gg, i: (b, gg, i, 0)),
                  per_group(n_piece, aug), per_group(dh, n_piece),
                  per_group(seq, aug, g * hpg), per_group(aug, seq),
                  per_group(seq, aug, g * hpg + g), per_group(aug, seq, g),
                  pl.BlockSpec((None, None, GATE_ROWS, nt * tq), lambda b, gg, i: (b, gg, 0, i)),
                  const((n_blk, n_piece)), const((ck, ck + 2 * tq)), const((wlen, wlen))],
        out_specs=pl.BlockSpec((None, nt * tq, hpg * dh), lambda b, gg, i: (b, i, gg)),
        out_shape=jax.ShapeDtypeStruct((bsz, seq, g * hpg * dh), MXU_DTYPE),
        scratch_shapes=[per_tile((n_piece, cols), F32),
                        per_tile((n_piece, cols), MXU_DTYPE),
                        per_tile((wlen, cols), F32),
                        per_tile((wlen, cols), MXU_DTYPE),
                        per_tile((ck, cols), F32),
                        per_tile((ck, cols), F32),
                        per_tile((ck, cols), MXU_DTYPE),
                        per_tile((ck, cols), MXU_DTYPE),
                        per_tile((1, cols), F32),
                        per_tile((1, cols), F32),
                        per_tile((1, cols), F32),
                        per_tile((aug, cols), F32),
                        per_tile((1, cols), F32),
                        per_tile((1, cols), F32),
                        per_tile((aug, cols), F32),
                        per_tile((n_blk, tq), F32)] * nt,
        compiler_params=_cparams(("parallel", "parallel", "arbitrary")),
        name="nsa_attention",
    )(rows_op, kc_p, vc_t, rows_op, cols_op, rows_op, cols_op, gate_t,
      jnp.asarray(_slc_map_t(n_piece, n_slc, n_blk)), jnp.asarray(bt, MXU_DTYPE), jnp.asarray(bandt, MXU_DTYPE))


def _residual_and_next_norm(x_ref, r, nw_ref, next_w_ref, o_ref, hn_ref):
    ms = jnp.mean(r * r, axis=-1, keepdims=True)
    o = x_ref[...] + r * lax.rsqrt(ms + EPS) * nw_ref[...]
    o_ref[...] = o
    ms_o = jnp.mean(o * o, axis=-1, keepdims=True)
    hn_ref[...] = (o * lax.rsqrt(ms_o + EPS) * next_w_ref[...]).astype(hn_ref.dtype)


def _merge_kernel(ha_ref, hb_ref, hc_ref, gl_ref, x_ref, wa_ref, wb_ref, wc_ref, wo_ref, nw_ref, next_w_ref,
                  o_ref, hn_ref, mix):
    d = x_ref.shape[1]
    branches = ((ha_ref, wa_ref), (hb_ref, wb_ref), (hc_ref, wc_ref))
    for n, (h_ref, w_ref) in enumerate(branches):
        y = _sigmoid(gl_ref[:, n * d:(n + 1) * d]) * jnp.dot(h_ref[...], w_ref[...], preferred_element_type=F32)
        if n == 0:
            mix[...] = y
        else:
            mix[...] += y
    r = jnp.dot(mix[...].astype(MXU_DTYPE), wo_ref[...], preferred_element_type=F32)
    _residual_and_next_norm(x_ref, r, nw_ref, next_w_ref, o_ref, hn_ref)


def _merge(ha, hb, hc, merge_logits, x2d, w_a, w_b, w_c, w_o, norm_w, next_norm_w):
    m, d = x2d.shape
    c = ha.shape[1]
    tm = 256
    row = lambda i: (i, 0)
    resident = lambda shape: pl.BlockSpec(shape, lambda i: (0, 0), pipeline_mode=pl.Buffered(1))
    wc = lambda w: w.astype(MXU_DTYPE)
    return pl.pallas_call(
        _merge_kernel,
        grid=(m // tm,),
        in_specs=[pl.BlockSpec((tm, c), row), pl.BlockSpec((tm, c), row), pl.BlockSpec((tm, c), row),
                  pl.BlockSpec((tm, 3 * d), row), pl.BlockSpec((tm, d), row),
                  resident((c, d)), resident((c, d)), resident((c, d)), resident((d, d)),
                  pl.BlockSpec((1, d), lambda i: (0, 0)), pl.BlockSpec((1, d), lambda i: (0, 0))],
        out_specs=[pl.BlockSpec((tm, d), row), pl.BlockSpec((tm, d), row)],
        out_shape=[jax.ShapeDtypeStruct((m, d), F32), jax.ShapeDtypeStruct((m, d), MXU_DTYPE)],
        scratch_shapes=[pltpu.VMEM((tm, d), F32)],
        compiler_params=_cparams(("parallel",)),
        name="merge_out_proj",
    )(ha, hb, hc, merge_logits, x2d, wc(w_a), wc(w_b), wc(w_c), wc(w_o), norm_w.reshape(1, d),
      next_norm_w.reshape(1, d))


def _ffn_up_kernel(h_ref, halo_ref, wg_ref, wv_ref, cwg_ref, cwv_ref, cbg_ref, cbv_ref, o_ref,
                   hbuf, ugbuf, uvbuf, *, tm, seq):
    i = pl.program_id(0)
    j = pl.program_id(1)

    @pl.when(j == 0)
    def _():
        hbuf[pl.ds(FFN_HALO, tm), :] = h_ref[...]
        first = (i * tm) % seq == 0
        hbuf[pl.ds(0, FFN_HALO), :] = jnp.where(first, jnp.zeros_like(halo_ref[...]), halo_ref[...])

    lhs = hbuf[...]
    ugbuf[...] = jnp.dot(lhs, wg_ref[...], preferred_element_type=F32)
    uvbuf[...] = jnp.dot(lhs, wv_ref[...], preferred_element_type=F32)

    def conv(buf, cw_ref, cb_ref):
        return (cb_ref[...] + cw_ref[pl.ds(2, 1), :] * buf[pl.ds(FFN_HALO, tm), :]
                + cw_ref[pl.ds(1, 1), :] * buf[pl.ds(FFN_HALO - 1, tm), :]
                + cw_ref[pl.ds(0, 1), :] * buf[pl.ds(FFN_HALO - 2, tm), :])

    cg = conv(ugbuf, cwg_ref, cbg_ref)
    cv = conv(uvbuf, cwv_ref, cbv_ref)
    o_ref[...] = (cg * _sigmoid(cg) * cv).astype(o_ref.dtype)


def _ffn_up(h, w_up, conv_w, conv_b, seq):
    m, d = h.shape
    dff = w_up.shape[1] // 2
    tm = min(1024, seq)
    tn = _pick(dff, (512, 256, 128))
    nj = dff // tn
    per = tm // FFN_HALO
    kern = functools.partial(_ffn_up_kernel, tm=tm, seq=seq)
    cb = conv_b.reshape(1, 2 * dff)
    return pl.pallas_call(
        kern,
        grid=(m // tm, nj),
        in_specs=[pl.BlockSpec((tm, d), lambda i, j: (i, 0)),
                  pl.BlockSpec((FFN_HALO, d), lambda i, j: (jnp.maximum(i * per - 1, 0), 0)),
                  pl.BlockSpec((d, tn), lambda i, j: (0, j)),
                  pl.BlockSpec((d, tn), lambda i, j: (0, j + nj)),
                  pl.BlockSpec((3, tn), lambda i, j: (0, j)),
                  pl.BlockSpec((3, tn), lambda i, j: (0, j + nj)),
                  pl.BlockSpec((1, tn), lambda i, j: (0, j)),
                  pl.BlockSpec((1, tn), lambda i, j: (0, j + nj))],
        out_specs=pl.BlockSpec((tm, tn), lambda i, j: (i, j)),
        out_shape=jax.ShapeDtypeStruct((m, dff), MXU_DTYPE),
        scratch_shapes=[pltpu.VMEM((FFN_HALO + tm, d), MXU_DTYPE),
                        pltpu.VMEM((FFN_HALO + tm, tn), F32),
                        pltpu.VMEM((FFN_HALO + tm, tn), F32)],
        compiler_params=_cparams(("parallel", "arbitrary")),
        name="ffn_up_conv_gate",
    )(h, h, w_up, w_up, conv_w, conv_w, cb, cb)


def _ffn_down_kernel(a_ref, w_ref, x_ref, nw_ref, o_ref):
    r = jnp.dot(a_ref[...], w_ref[...], preferred_element_type=F32)
    ms = jnp.mean(r * r, axis=-1, keepdims=True)
    o_ref[...] = x_ref[...] + r * lax.rsqrt(ms + EPS) * nw_ref[...]


def _ffn_down_next_kernel(a_ref, w_ref, x_ref, nw_ref, next_w_ref, o_ref, hn_ref):
    r = jnp.dot(a_ref[...], w_ref[...], preferred_element_type=F32)
    _residual_and_next_norm(x_ref, r, nw_ref, next_w_ref, o_ref, hn_ref)


def _ffn_down(act, w_down, x2d, norm_w, next_norm_w=None):
    m, dff = act.shape
    d = w_down.shape[1]
    tm = 256
    row = pl.BlockSpec((tm, d), lambda i: (i, 0))
    vec = pl.BlockSpec((1, d), lambda i: (0, 0))
    in_specs = [pl.BlockSpec((tm, dff), lambda i: (i, 0)),
                pl.BlockSpec((dff, d), lambda i: (0, 0), pipeline_mode=pl.Buffered(1)),
                row, vec]
    args = [act, w_down, x2d, norm_w.reshape(1, d)]
    x_shape = jax.ShapeDtypeStruct((m, d), F32)
    if next_norm_w is None:
        kern, out_specs, out_shape = _ffn_down_kernel, row, x_shape
    else:
        kern, out_specs = _ffn_down_next_kernel, [row, row]
        out_shape = [x_shape, jax.ShapeDtypeStruct((m, d), MXU_DTYPE)]
        in_specs.append(vec)
        args.append(next_norm_w.reshape(1, d))
    return pl.pallas_call(
        kern,
        grid=(m // tm,),
        in_specs=in_specs,
        out_specs=out_specs,
        out_shape=out_shape,
        compiler_params=_cparams(("parallel",)),
        name="ffn_down_norm_res",
    )(*args)


def _gate_weight(w_gate):
    d = w_gate.shape[0]
    per = HEADS_PER_GROUP * 3
    w = w_gate.reshape(d, N_KV_GROUPS, per)
    w = jnp.pad(w, ((0, 0), (0, 0), (0, GATE_PAD - per)))
    return w.reshape(d, N_KV_GROUPS * GATE_PAD)


def _mixer_layer(x, h, p, l):
    bsz, seq, d = x.shape
    m = bsz * seq
    x2d = x.reshape(m, d)
    g, hpg, dh = N_KV_GROUPS, HEADS_PER_GROUP, HEAD_DIM
    a_in_w = 2 * CONV_WIDTH
    b_in_w = 2 * SG_WIDTH
    q_w = N_HEADS * dh
    kv_w = 6 * g * dh
    gate_w = 3 * N_HEADS
    o0 = 0
    o1 = o0 + a_in_w
    o2 = o1 + b_in_w
    o3 = o2 + q_w
    o4 = o3 + kv_w
    o5 = o4 + gate_w
    w_in = p["w_in"][l]

    wc = lambda w: w.astype(MXU_DTYPE)
    a_in = _matmul(h, wc(w_in[:, o0:o1]), name="proj_a")
    b_in = _matmul(h, wc(w_in[:, o1:o2]), name="proj_b")
    h3 = h.reshape(bsz, seq, d)
    aug = 2 * dh
    w_q = w_in[:, o2:o3].reshape(d, N_HEADS, dh)
    w_kv = w_in[:, o3:o4].reshape(d, 6, g, dh)
    pad_lanes = lambda w: jnp.pad(w, ((0, 0), (0, 0), (0, aug - dh))).reshape(d, -1)
    w_rows = wc(jnp.concatenate([pad_lanes(w_q), pad_lanes(w_kv[:, 2]), pad_lanes(w_kv[:, 4])], axis=1))
    onehot = np.zeros((seq, aug), np.float32)
    onehot[np.arange(seq), dh + np.arange(seq) // SLC_BLOCK] = 1.0
    slc_slabs = (N_HEADS * aug // PROJ_SLAB, (N_HEADS + g) * aug // PROJ_SLAB)
    rows_op = _proj_planes(h3, w_rows, jnp.asarray(onehot), MXU_DTYPE, aug, slc_slabs, "proj_attn_rows")
    w_cmp = wc(jnp.concatenate([w_kv[:, 0].reshape(d, g * dh), w_kv[:, 1].reshape(d, g * dh)], axis=1))
    cmp_op = _proj_planes(h3, w_cmp, jnp.zeros((seq, dh), F32), F32, dh, (0, 0), "proj_attn_cmp")
    pad_rows = lambda w: jnp.pad(w.transpose(1, 2, 0), ((0, 0), (0, aug - dh), (0, 0))).reshape(-1, d)
    w_cols = wc(jnp.concatenate([pad_rows(w_kv[:, 3]), pad_rows(w_kv[:, 5])], axis=0))
    ones_rows = np.zeros((2 * g, aug, min(512, seq)), np.float32)
    ones_rows[:, dh] = 1.0
    cols_op = _proj_cols(h3, w_cols, jnp.asarray(ones_rows.reshape(2 * g * aug, -1)), 2 * g)
    gate_logits = _matmul(h, wc(_gate_weight(w_in[:, o4:o5])), name="proj_gate")
    merge_logits = _matmul(h, wc(w_in[:, o5:]), name="proj_merge")

    h_a = _conformer(a_in.reshape(bsz, seq, a_in_w), p["conv_a_w"][l], p["conv_a_b"][l],
                     p["ln_a_g"][l], p["ln_a_b"][l]).reshape(m, CONV_WIDTH)
    h_b = _gmlp(b_in, p["ln_b_g"][l], p["ln_b_b"][l], p["sg_w"][l], p["sg_b"][l])

    n_piece = seq // CMP_STRIDE
    kc = _compress(cmp_op[:, :g].reshape(bsz, g, n_piece, CMP_STRIDE * dh), p["cmp_pe_k"][l], p["cmp_w1_k"][l],
                   p["cmp_w2_k"][l])
    vc = _compress(cmp_op[:, g:].reshape(bsz, g, n_piece, CMP_STRIDE * dh), p["cmp_pe_v"][l], p["cmp_w1_v"][l],
                   p["cmp_w2_v"][l])
    o = _nsa_attention(rows_op, cols_op, kc, vc, gate_logits.reshape(bsz, seq, g * GATE_PAD))
    h_c = o.reshape(m, N_HEADS * dh)

    x_new, h_ffn = _merge(h_a, h_b, h_c, merge_logits, x2d, p["w_a_out"][l], p["w_b_out"][l], p["w_c_out"][l],
                          p["w_o"][l], p["norm_mix_post"][l], p["norm_ffn_pre"][l])
    return x_new.reshape(bsz, seq, d), h_ffn


def _ffn_layer(x, h, p, l, next_norm_w):
    bsz, seq, d = x.shape
    m = bsz * seq
    act = _ffn_up(h, p["w_up"][l].astype(MXU_DTYPE), p["ffn_conv_w"][l], p["ffn_conv_b"][l], seq)
    out = _ffn_down(act, p["w_down"][l].astype(MXU_DTYPE), x.reshape(m, d), p["norm_ffn_post"][l], next_norm_w)
    if next_norm_w is None:
        return out.reshape(bsz, seq, d), None
    return out[0].reshape(bsz, seq, d), out[1]


def kernel(x, norm_mix_pre, norm_mix_post, norm_ffn_pre, norm_ffn_post, w_in, conv_a_w, conv_a_b, ln_a_g, ln_a_b,
           w_a_out, ln_b_g, ln_b_b, sg_w, sg_b, w_b_out, cmp_pe_k, cmp_w1_k, cmp_w2_k, cmp_pe_v, cmp_w1_v,
           cmp_w2_v, w_c_out, w_o, w_up, ffn_conv_w, ffn_conv_b, w_down):
    p = dict(norm_mix_pre=norm_mix_pre, norm_mix_post=norm_mix_post, norm_ffn_pre=norm_ffn_pre,
             norm_ffn_post=norm_ffn_post, w_in=w_in, conv_a_w=conv_a_w, conv_a_b=conv_a_b, ln_a_g=ln_a_g,
             ln_a_b=ln_a_b, w_a_out=w_a_out, ln_b_g=ln_b_g, ln_b_b=ln_b_b, sg_w=sg_w, sg_b=sg_b, w_b_out=w_b_out,
             cmp_pe_k=cmp_pe_k, cmp_w1_k=cmp_w1_k, cmp_w2_k=cmp_w2_k, cmp_pe_v=cmp_pe_v, cmp_w1_v=cmp_w1_v,
             cmp_w2_v=cmp_w2_v, w_c_out=w_c_out, w_o=w_o, w_up=w_up, ffn_conv_w=ffn_conv_w,
             ffn_conv_b=ffn_conv_b, w_down=w_down)
    depth = w_in.shape[0]
    h = _rmsnorm_cast(x.reshape(-1, x.shape[-1]), norm_mix_pre[0])
    for l in range(depth):
        x, h = _mixer_layer(x, h, p, l)
        x, h = _ffn_layer(x, h, p, l, norm_mix_pre[l + 1] if l + 1 < depth else None)
    return x
```

```python
import functools

import numpy as np
import jax
import jax.numpy as jnp
from jax import lax
from jax.experimental import pallas as pl
from jax.experimental.pallas import tpu as pltpu

F32 = jnp.float32
MXU_DTYPE = jnp.bfloat16

EPS = 1e-6
NEG = -1e30
FORCE = 1e4

CONV_WIDTH = 1024
DW_CONV_SIZE = 31
CONV_HALO = 32
CONV_SUB = 64
SG_WIDTH = 1024
SG_CHUNK = 128
SG_GROUPS = 8
N_HEADS = 16
N_KV_GROUPS = 4
HEADS_PER_GROUP = 4
HEAD_DIM = 64
CMP_BLOCK = 32
CMP_STRIDE = 16
CMP_HIDDEN = 256
SLC_BLOCK = 64
SLC_TOP_N = 16
WINDOW = 512
D_FF = 5632
FFN_HALO = 16
GATE_PAD = 128
PROJ_SLAB = 512
GATE_ROWS = 16

VMEM_LIMIT = 56 * 1024 * 1024


def _cparams(sem):
    return pltpu.CompilerParams(dimension_semantics=sem, vmem_limit_bytes=VMEM_LIMIT)


def _sigmoid(x):
    return jax.nn.sigmoid(x)


def _rmsnorm_kernel(x_ref, w_ref, o_ref):
    x = x_ref[...]
    ms = jnp.mean(x * x, axis=-1, keepdims=True)
    o_ref[...] = (x * lax.rsqrt(ms + EPS) * w_ref[...]).astype(o_ref.dtype)


def _rmsnorm_cast(x2d, w):
    m, d = x2d.shape
    tm = min(512, m)
    return pl.pallas_call(
        _rmsnorm_kernel,
        grid=(m // tm,),
        in_specs=[pl.BlockSpec((tm, d), lambda i: (i, 0)),
                  pl.BlockSpec((1, d), lambda i: (0, 0))],
        out_specs=pl.BlockSpec((tm, d), lambda i: (i, 0)),
        out_shape=jax.ShapeDtypeStruct((m, d), MXU_DTYPE),
        compiler_params=_cparams(("parallel",)),
        name="rmsnorm_cast",
    )(x2d, w.reshape(1, d))


def _mm_kernel(a_ref, w_ref, o_ref):
    o_ref[...] = jnp.dot(a_ref[...], w_ref[...], preferred_element_type=F32).astype(o_ref.dtype)


def _pick(n, cands):
    for c in cands:
        if n % c == 0:
            return c
    raise ValueError(f"no tile for {n}")


def _matmul(a, w, out_dtype=F32, name="matmul"):
    m, k = a.shape
    n = w.shape[1]
    tm = min(2048, m)
    tn = _pick(n, (1024, 512, 384, 256, 128))
    return pl.pallas_call(
        _mm_kernel,
        grid=(m // tm, n // tn),
        in_specs=[pl.BlockSpec((tm, k), lambda i, j: (i, 0)),
                  pl.BlockSpec((k, tn), lambda i, j: (0, j))],
        out_specs=pl.BlockSpec((tm, tn), lambda i, j: (i, j)),
        out_shape=jax.ShapeDtypeStruct((m, n), out_dtype),
        compiler_params=_cparams(("parallel", "arbitrary")),
        name=name,
    )(a, w)


def _proj_planes_kernel(h_ref, w_ref, add_ref, o_ref, *, add_lo, add_hi, width):
    j = pl.program_id(2)
    res = jnp.dot(h_ref[...], w_ref[...], preferred_element_type=F32)
    if add_hi > add_lo:
        add = jnp.where((j >= add_lo) & (j < add_hi), add_ref[...], 0.0)
    for p in range(o_ref.shape[0]):
        piece = res[:, p * width:(p + 1) * width]
        o_ref[p] = (piece + add if add_hi > add_lo else piece).astype(o_ref.dtype)


def _proj_planes(h3, w, add, out_dtype, width, add_range, name):
    bsz, seq, d = h3.shape
    n = w.shape[1]
    slab = PROJ_SLAB
    per = slab // width
    tm = min(1024, seq)
    kern = functools.partial(_proj_planes_kernel, add_lo=add_range[0], add_hi=add_range[1], width=width)
    return pl.pallas_call(
        kern,
        grid=(bsz, seq // tm, n // slab),
        in_specs=[pl.BlockSpec((None, tm, d), lambda b, i, j: (b, i, 0)),
                  pl.BlockSpec((d, slab), lambda b, i, j: (0, j)),
                  pl.BlockSpec((tm, width), lambda b, i, j: (i, 0))],
        out_specs=pl.BlockSpec((None, per, tm, width), lambda b, i, j: (b, j, i, 0)),
        out_shape=jax.ShapeDtypeStruct((bsz, n // width, seq, width), out_dtype),
        compiler_params=_cparams(("parallel", "parallel", "arbitrary")),
        name=name,
    )(h3, w, add)


def _proj_cols_kernel(h_ref, wt_ref, add_ref, o_ref):
    res = lax.dot_general(wt_ref[...], h_ref[...], (((1,), (1,)), ((), ())), preferred_element_type=F32)
    o_ref[...] = (res + add_ref[...]).astype(o_ref.dtype).reshape(o_ref.shape)


def _proj_cols(h3, w_t, add, planes):
    bsz, seq, d = h3.shape
    n = w_t.shape[0]
    tm = min(512, seq)
    return pl.pallas_call(
        _proj_cols_kernel,
        grid=(bsz, seq // tm),
        in_specs=[pl.BlockSpec((None, tm, d), lambda b, i: (b, i, 0)),
                  pl.BlockSpec((n, d), lambda b, i: (0, 0), pipeline_mode=pl.Buffered(1)),
                  pl.BlockSpec((n, tm), lambda b, i: (0, 0), pipeline_mode=pl.Buffered(1))],
        out_specs=pl.BlockSpec((None, planes, n // planes, tm), lambda b, i: (b, 0, 0, i)),
        out_shape=jax.ShapeDtypeStruct((bsz, planes, n // planes, seq), MXU_DTYPE),
        compiler_params=_cparams(("parallel", "parallel")),
        name="proj_attn_cols",
    )(h3, w_t, add)


def _conformer_kernel(cur_ref, halo_ref, cw_ref, cb_ref, lg_ref, lb_ref, o_ref, hbuf, cbuf, *, ts):
    c = CONV_WIDTH
    i = pl.program_id(1)
    cur = cur_ref[...]
    hbuf[pl.ds(CONV_HALO, ts), :] = cur[:, :c] * _sigmoid(cur[:, c:])
    hal = halo_ref[...]
    hh = hal[:, :c] * _sigmoid(hal[:, c:])
    hbuf[pl.ds(0, CONV_HALO), :] = jnp.where(i == 0, 0.0, hh)

    first = CONV_HALO - (DW_CONV_SIZE - 1)
    span = CONV_SUB + CONV_HALO

    def blk_body(idx, carry):
        off = pl.multiple_of((idx // (ts // CONV_SUB)) * 128, 128)
        r0 = pl.multiple_of((idx % (ts // CONV_SUB)) * CONV_SUB, CONV_SUB)
        x = hbuf[pl.ds(r0, span), pl.ds(off, 128)]
        acc = jnp.broadcast_to(cb_ref[:, pl.ds(off, 128)], (CONV_SUB, 128))
        for r in range(8):
            xr = x if r == 0 else pltpu.roll(x, span - r, 0)
            for a in range(CONV_HALO // 8 + 1):
                k = 8 * a + r - first
                if 0 <= k < DW_CONV_SIZE:
                    acc = acc + cw_ref[pl.ds(k, 1), pl.ds(off, 128)] * xr[8 * a:8 * a + CONV_SUB]
        cbuf[pl.ds(r0, CONV_SUB), pl.ds(off, 128)] = acc
        return carry

    lax.fori_loop(0, (c // 128) * (ts // CONV_SUB), blk_body, 0)
    y = cbuf[...]
    mean = jnp.mean(y, axis=-1, keepdims=True)
    yc = y - mean
    var = jnp.mean(yc * yc, axis=-1, keepdims=True)
    z = yc * lax.rsqrt(var + EPS) * lg_ref[...] + lb_ref[...]
    o_ref[...] = (z * _sigmoid(z)).astype(o_ref.dtype)


def _conformer(a_in, conv_w, conv_b, ln_g, ln_b):
    bsz, seq, _ = a_in.shape
    c = CONV_WIDTH
    ts = min(256, seq)
    per = ts // CONV_HALO
    kern = functools.partial(_conformer_kernel, ts=ts)
    return pl.pallas_call(
        kern,
        grid=(bsz, seq // ts),
        in_specs=[pl.BlockSpec((None, ts, 2 * c), lambda b, i: (b, i, 0)),
                  pl.BlockSpec((None, CONV_HALO, 2 * c), lambda b, i: (b, jnp.maximum(i * per - 1, 0), 0)),
                  pl.BlockSpec((DW_CONV_SIZE, c), lambda b, i: (0, 0)),
                  pl.BlockSpec((1, c), lambda b, i: (0, 0)),
                  pl.BlockSpec((1, c), lambda b, i: (0, 0)),
                  pl.BlockSpec((1, c), lambda b, i: (0, 0))],
        out_specs=pl.BlockSpec((None, ts, c), lambda b, i: (b, i, 0)),
        out_shape=jax.ShapeDtypeStruct((bsz, seq, c), MXU_DTYPE),
        scratch_shapes=[pltpu.VMEM((CONV_HALO + ts, c), F32), pltpu.VMEM((ts, c), F32)],
        compiler_params=_cparams(("parallel", "parallel")),
        name="conformer_conv",
    )(a_in, a_in, conv_w, conv_b.reshape(1, c), ln_g.reshape(1, c), ln_b.reshape(1, c))


def _gmlp_kernel(x_ref, lg_ref, lb_ref, sgw_ref, sgbt_ref, o_ref, *, ts):
    c = SG_WIDTH
    gd = c // SG_GROUPS
    x = x_ref[...]
    z = 0.5 * x * (1.0 + lax.erf(x * np.float32(np.sqrt(0.5))))
    u = z[:, :c]
    v = z[:, c:]
    mean = jnp.mean(v, axis=-1, keepdims=True)
    vc = v - mean
    var = jnp.mean(vc * vc, axis=-1, keepdims=True)
    v = vc * lax.rsqrt(var + EPS) * lg_ref[...] + lb_ref[...]
    row = lax.broadcasted_iota(jnp.int32, (SG_CHUNK, SG_CHUNK), 0)
    col = lax.broadcasted_iota(jnp.int32, (SG_CHUNK, SG_CHUNK), 1)
    causal = col <= row
    for g in range(SG_GROUPS):
        w = jnp.where(causal, sgw_ref[g], 0.0).astype(MXU_DTYPE)
        bias = sgbt_ref[:, g:g + 1]
        for ch in range(ts // SG_CHUNK):
            r0 = ch * SG_CHUNK
            vg = v[r0:r0 + SG_CHUNK, g * gd:(g + 1) * gd].astype(MXU_DTYPE)
            f = jnp.dot(w, vg, preferred_element_type=F32) + bias
            o_ref[r0:r0 + SG_CHUNK, g * gd:(g + 1) * gd] = (u[r0:r0 + SG_CHUNK, g * gd:(g + 1) * gd]
                                                            * f).astype(o_ref.dtype)


def _gmlp(b_in, ln_g, ln_b, sg_w, sg_b, col_block=0):
    m, _ = b_in.shape
    c = SG_WIDTH
    ts = 256
    kern = functools.partial(_gmlp_kernel, ts=ts)
    return pl.pallas_call(
        kern,
        grid=(m // ts,),
        in_specs=[pl.BlockSpec((ts, 2 * c), lambda i: (i, col_block)),
                  pl.BlockSpec((1, c), lambda i: (0, 0)),
                  pl.BlockSpec((1, c), lambda i: (0, 0)),
                  pl.BlockSpec((SG_GROUPS, SG_CHUNK, SG_CHUNK), lambda i: (0, 0, 0)),
                  pl.BlockSpec((SG_CHUNK, SG_GROUPS), lambda i: (0, 0))],
        out_specs=pl.BlockSpec((ts, c), lambda i: (i, 0)),
        out_shape=jax.ShapeDtypeStruct((m, c), MXU_DTYPE),
        compiler_params=_cparams(("parallel",)),
        name="gmlp_gating",
    )(b_in, ln_g.reshape(1, c), ln_b.reshape(1, c), sg_w, sg_b.T)


def _compress_kernel(p_ref, pelo_ref, pehi_ref, w1a_ref, w1b_ref, w2_ref, o_ref):
    p = p_ref[...]
    a = jnp.dot((p + pelo_ref[...]).astype(MXU_DTYPE), w1a_ref[...], preferred_element_type=F32)
    b = jnp.dot((p + pehi_ref[...]).astype(MXU_DTYPE), w1b_ref[...], preferred_element_type=F32)
    n = p.shape[0]
    b_next = pltpu.roll(b, n - 1, 0)
    hid = a + b_next
    hid = hid * _sigmoid(hid)
    o_ref[...] = jnp.dot(hid.astype(MXU_DTYPE), w2_ref[...], preferred_element_type=F32)


def _compress(pieces, pe, w1, w2):
    bsz, g, n_piece, width = pieces.shape
    half = CMP_BLOCK // 2
    pe_lo = pe[:half].reshape(1, width)
    pe_hi = pe[half:].reshape(1, width)
    w1a = w1[:half].reshape(width, CMP_HIDDEN).astype(MXU_DTYPE)
    w1b = w1[half:].reshape(width, CMP_HIDDEN).astype(MXU_DTYPE)
    return pl.pallas_call(
        _compress_kernel,
        grid=(bsz, g),
        in_specs=[pl.BlockSpec((None, None, n_piece, width), lambda b, gg: (b, gg, 0, 0)),
                  pl.BlockSpec((1, width), lambda b, gg: (0, 0)),
                  pl.BlockSpec((1, width), lambda b, gg: (0, 0)),
                  pl.BlockSpec((width, CMP_HIDDEN), lambda b, gg: (0, 0)),
                  pl.BlockSpec((width, CMP_HIDDEN), lambda b, gg: (0, 0)),
                  pl.BlockSpec((CMP_HIDDEN, HEAD_DIM), lambda b, gg: (0, 0))],
        out_specs=pl.BlockSpec((None, None, n_piece, HEAD_DIM), lambda b, gg: (b, gg, 0, 0)),
        out_shape=jax.ShapeDtypeStruct((bsz, g, n_piece, HEAD_DIM), F32),
        compiler_params=_cparams(("parallel", "parallel")),
        name="nsa_compress",
    )(pieces, pe_lo, pe_hi, w1a, w1b, w2.astype(MXU_DTYPE))


ROW_BLK = 32
M_INIT = -3e38
NSA_PROLOGUE_PHASES = 5
NSA_PHASE_LAG = 2


def _masked_softmax(s, mask):
    s = jnp.where(mask, s, NEG)
    m = jnp.max(s, axis=-1, keepdims=True)
    e = jnp.where(mask, jnp.exp(s - m), 0.0)
    return e / jnp.maximum(jnp.sum(e, axis=-1, keepdims=True), 1e-30)


def _nt_dot(a, b, precision=None):
    return lax.dot_general(a, b, (((1,), (1,)), ((), ())), preferred_element_type=F32, precision=precision)


def _nsa_kernel(*refs, tq, ck, nt):
    n_in = 12
    shared, scratch = refs[:n_in], refs[n_in:]
    per_tile = len(scratch) // nt
    tiles = [_nsa_tile(u, *shared, *scratch[u * per_tile:(u + 1) * per_tile], tq=tq, ck=ck, nt=nt)
             for u in range(nt)]
    loops = [None] * nt
    for step in range(NSA_PROLOGUE_PHASES + NSA_PHASE_LAG * (nt - 1)):
        for u in range(nt):
            if 0 <= step - NSA_PHASE_LAG * u < NSA_PROLOGUE_PHASES:
                loops[u] = next(tiles[u])

    def pair_body(i, carry):
        for stage in zip(*[stages(i) for _, stages in loops]):
            for run in stage:
                run()
        return carry

    lax.fori_loop(0, loops[0][0], pair_body, 0)
    for t in tiles:
        next(t, None)


def _nsa_tile(u, q_ref, kc_ref, vct_ref, ks_ref, vst_ref, kw_ref, vwt_ref, gate_ref, mapt_ref, bt_ref, bandt_ref,
              o_ref, sc_scr, pc_scr, sw_scr, pw_scr, sa_scr, sb_scr, pa_scr, pb_scr, m_scr, ala_scr, alb_scr, acc_scr,
              mw_scr, al_scr, accw_scr, rank_scr, *, tq, ck, nt):
    hpg = HEADS_PER_GROUP
    dh = HEAD_DIM
    aug = 2 * dh
    cols = hpg * tq
    n_blk = mapt_ref.shape[0]
    q0 = (pl.program_id(2) * nt + u) * tq
    tile_rows = pl.ds(u * tq, tq)
    qp = (q_ref[:, tile_rows, :] * np.float32(dh ** -0.5).astype(q_ref.dtype)).reshape(cols, aug)
    eye4 = ((lax.broadcasted_iota(jnp.int32, (cols, tq), 0) & (tq - 1))
            == lax.broadcasted_iota(jnp.int32, (cols, tq), 1)).astype(MXU_DTYPE)
    t_lane = q0 + lax.broadcasted_iota(jnp.int32, (1, tq), 1)

    def online(s_ref, p_ref, kr, m_ref, acc_ref, v_t):
        for h in range(hpg):
            cb = pl.ds(h * tq, tq)
            s = s_ref[kr, cb]
            m_old = m_ref[:, cb]
            m_new = jnp.maximum(m_old, jnp.max(s, axis=0, keepdims=True))
            p_ref[kr, cb] = jnp.exp(s - m_new).astype(p_ref.dtype)
            al_scr[:, cb] = jnp.exp(m_old - m_new)
            m_ref[:, cb] = m_new
        acc_ref[...] = al_scr[...] * acc_ref[...] + jnp.dot(v_t, p_ref[kr, :], preferred_element_type=F32)

    def finish(acc):
        return acc[:dh] / jnp.maximum(acc[dh:dh + 1], 1e-30)

    n_cmp = kc_ref.shape[0]
    sc_scr[...] = _nt_dot(kc_ref[...].astype(MXU_DTYPE), qp)
    wlen = WINDOW + tq
    w0 = pl.multiple_of(jnp.maximum(q0 - WINDOW, 0), tq)
    lhs_win = jnp.concatenate([qp, eye4], axis=1)
    rhs_win = jnp.concatenate([kw_ref[pl.ds(w0, wlen), :], bandt_ref[:, pl.ds(pl.multiple_of(q0 - w0, tq), tq)]],
                              axis=1)
    sw_scr[...] = _nt_dot(rhs_win, lhs_win)
    yield None

    cmask = (lax.broadcasted_iota(jnp.int32, (n_cmp, 1), 0) * CMP_STRIDE + (CMP_BLOCK - 1)) <= t_lane
    psum = jnp.zeros((n_cmp, tq), F32)
    for h in range(hpg):
        cb = pl.ds(h * tq, tq)
        s = jnp.where(cmask, sc_scr[:, cb], NEG)
        e = jnp.where(cmask, jnp.exp(s - jnp.max(s, axis=0, keepdims=True)), 0.0)
        p = e / jnp.maximum(jnp.sum(e, axis=0, keepdims=True), 1e-30)
        pc_scr[:, cb] = p.astype(pc_scr.dtype)
        psum = psum + p
    o_cmp = jnp.dot(vct_ref[...].astype(MXU_DTYPE), pc_scr[...], preferred_element_type=F32)
    yield None

    mw_scr[...] = jnp.full(mw_scr.shape, M_INIT, F32)
    accw_scr[...] = jnp.zeros(accw_scr.shape, F32)
    for k_lo in range(0, wlen, ck):
        nk = min(ck, wlen - k_lo)
        online(sw_scr, pw_scr, pl.ds(k_lo, nk), mw_scr, accw_scr,
               vwt_ref[:, pl.ds(pl.multiple_of(w0 + k_lo, tq), nk)])
    o_win = finish(accw_scr[...])
    yield None

    imp_t = jnp.dot(mapt_ref[...], psum, preferred_element_type=F32, precision=lax.Precision.HIGHEST)
    j_t = lax.broadcasted_iota(jnp.int32, (n_blk, tq), 0)
    cur_t = (q0 + lax.broadcasted_iota(jnp.int32, (n_blk, tq), 1)) >> 6
    forced = (j_t == 0) | (j_t == cur_t) | (j_t == cur_t - 1)
    score_t = jnp.where(j_t <= cur_t, jnp.where(forced, FORCE, imp_t), NEG)
    rank_scr[...] = score_t
    sel_rows = []
    for v in range(n_blk // 8):
        sj = score_t[8 * v:8 * v + 8]
        jv = j_t[8 * v:8 * v + 8]
        rank = jnp.zeros((8, tq), jnp.int32)
        for i in range(n_blk):
            ci = rank_scr[pl.ds(i, 1), :]
            if i < 8 * v:
                ahead = ci >= sj
            elif i > 8 * v + 7:
                ahead = ci > sj
            else:
                ahead = (ci > sj) | ((ci == sj) & (jv > i))
            rank = rank + ahead.astype(jnp.int32)
        sel_rows.append(jnp.where(rank < SLC_TOP_N, 0.0, NEG))
    yield None
    selb_t = jnp.concatenate([jnp.zeros((dh, tq), F32)] + sel_rows, axis=0).astype(MXU_DTYPE)
    selb = _nt_dot(eye4[:tq], selb_t).astype(MXU_DTYPE)
    lhs_slc = jnp.concatenate([(qp.reshape(hpg, tq, aug) + selb[None]).reshape(cols, aug), eye4], axis=1)

    n_full = q0 // ck

    def scores(c, s_ref):
        k0 = pl.multiple_of(jnp.minimum(c, n_full) * ck, ck)
        start = pl.multiple_of(jnp.clip(q0 - c * ck, -tq, ck) + tq, tq)
        rhs = jnp.concatenate([ks_ref[pl.ds(k0, ck), :], bt_ref[:, pl.ds(start, tq)]], axis=1)
        s_ref[...] = _nt_dot(rhs, lhs_slc)

    def softmax(s_ref, p_ref, a_ref):
        for h in range(hpg):
            cb = pl.ds(h * tq, tq)
            s = s_ref[:, cb]
            m_old = m_scr[:, cb]
            m_new = jnp.maximum(m_old, jnp.max(s, axis=0, keepdims=True))
            p_ref[:, cb] = jnp.exp(s - m_new).astype(p_ref.dtype)
            a_ref[:, cb] = jnp.exp(m_old - m_new)
            m_scr[:, cb] = m_new

    def accumulate(c, p_ref, a_ref):
        k0 = pl.multiple_of(jnp.clip(c, 0, n_full) * ck, ck)
        acc_scr[...] = a_ref[...] * acc_scr[...] + jnp.dot(vst_ref[:, pl.ds(k0, ck)], p_ref[...],
                                                           preferred_element_type=F32)

    m_scr[...] = jnp.full(m_scr.shape, M_INIT, F32)
    acc_scr[...] = jnp.zeros(acc_scr.shape, F32)
    pb_scr[...] = jnp.zeros(pb_scr.shape, pb_scr.dtype)
    alb_scr[...] = jnp.ones(alb_scr.shape, F32)
    scores(0, sa_scr)

    def pair_stages(i):
        k = 2 * i
        return (lambda: softmax(sa_scr, pa_scr, ala_scr),
                lambda: accumulate(k - 1, pb_scr, alb_scr),
                lambda: scores(k + 1, sb_scr),
                lambda: softmax(sb_scr, pb_scr, alb_scr),
                lambda: accumulate(k, pa_scr, ala_scr),
                lambda: scores(k + 2, sa_scr))

    n_pairs = (n_full + 2) // 2
    yield n_pairs, pair_stages
    accumulate(2 * n_pairs - 1, pb_scr, alb_scr)
    o_slc = finish(acc_scr[...])

    gl = _sigmoid(gate_ref[:, tile_rows])
    outs = []
    for h in range(hpg):
        cb = slice(h * tq, (h + 1) * tq)
        outs.append(gl[3 * h:3 * h + 1] * o_cmp[:, cb] + gl[3 * h + 1:3 * h + 2] * o_slc[:, cb]
                    + gl[3 * h + 2:3 * h + 3] * o_win[:, cb])
    o_t = jnp.concatenate(outs, axis=0).astype(MXU_DTYPE)
    o_ref[tile_rows, :] = _nt_dot(eye4[:tq], o_t).astype(o_ref.dtype)


def _slc_map_t(n_piece, n_slc, n_blk):
    r = CMP_BLOCK // CMP_STRIDE
    a = SLC_BLOCK // CMP_STRIDE
    m = np.zeros((n_blk, n_piece), np.float32)
    for n in range(n_piece - r + 1):
        for i in range(r):
            m[(n + i) // a, n] += 1.0
    return m


def _nsa_attention(rows_op, cols_op, kc, vc, gate_logits):
    bsz, _, seq, aug = rows_op.shape
    g, hpg, dh = N_KV_GROUPS, HEADS_PER_GROUP, HEAD_DIM
    n_piece = kc.shape[2]
    n_slc = seq // SLC_BLOCK
    n_blk = dh
    assert n_slc <= n_blk
    tq = 128
    ck = 256
    wlen = WINDOW + tq
    cols = hpg * tq

    kc_p = jnp.pad(kc, ((0, 0), (0, 0), (0, 0), (0, dh)))
    vc_t = vc.astype(MXU_DTYPE).transpose(0, 1, 3, 2)
    gate_t = gate_logits.reshape(bsz, seq, g, GATE_PAD)[..., :GATE_ROWS].transpose(0, 2, 3, 1)

    kk = np.arange(ck)[:, None]
    bt = np.where(kk <= np.arange(-tq, ck + tq)[None, :], 0.0, NEG).astype(np.float32)
    kw_i = np.arange(wlen)[:, None]
    tau = np.arange(wlen)[None, :]
    bandt = np.where((kw_i <= tau) & (kw_i > tau - WINDOW), 0.0, NEG).astype(np.float32)

    nt = ck // tq
    kern = functools.partial(_nsa_kernel, tq=tq, ck=ck, nt=nt)
    per_group = lambda r, c, first=0: pl.BlockSpec((None, None, r, c), lambda b, gg, i: (b, first + gg, 0, 0))
    const = lambda shape: pl.BlockSpec(shape, lambda b, gg, i: (0, 0))
    per_tile = pltpu.VMEM
    return pl.pallas_call(
        kern,
        grid=(bsz, g, seq // (nt * tq)),
        in_specs=[pl.BlockSpec((None, hpg, nt * tq, aug), lambda b, gg, i: (b, gg, i, 0)),
                  per_group(n_piece, aug), per_group(dh, n_piece),
                  per_group(seq, aug, g * hpg), per_group(aug, seq),
                  per_group(seq, aug, g * hpg + g), per_group(aug, seq, g),
                  pl.BlockSpec((None, None, GATE_ROWS, nt * tq), lambda b, gg, i: (b, gg, 0, i)),
                  const((n_blk, n_piece)), const((ck, ck + 2 * tq)), const((wlen, wlen))],
        out_specs=pl.BlockSpec((None, nt * tq, hpg * dh), lambda b, gg, i: (b, i, gg)),
        out_shape=jax.ShapeDtypeStruct((bsz, seq, g * hpg * dh), MXU_DTYPE),
        scratch_shapes=[per_tile((n_piece, cols), F32),
                        per_tile((n_piece, cols), MXU_DTYPE),
                        per_tile((wlen, cols), F32),
                        per_tile((wlen, cols), MXU_DTYPE),
                        per_tile((ck, cols), F32),
                        per_tile((ck, cols), F32),
                        per_tile((ck, cols), MXU_DTYPE),
                        per_tile((ck, cols), MXU_DTYPE),
                        per_tile((1, cols), F32),
                        per_tile((1, cols), F32),
                        per_tile((1, cols), F32),
                        per_tile((aug, cols), F32),
                        per_tile((1, cols), F32),
                        per_tile((1, cols), F32),
                        per_tile((aug, cols), F32),
                        per_tile((n_blk, tq), F32)] * nt,
        compiler_params=_cparams(("parallel", "parallel", "arbitrary")),
        name="nsa_attention",
    )(rows_op, kc_p, vc_t, rows_op, cols_op, rows_op, cols_op, gate_t,
      jnp.asarray(_slc_map_t(n_piece, n_slc, n_blk)), jnp.asarray(bt, MXU_DTYPE), jnp.asarray(bandt, MXU_DTYPE))


def _residual_and_next_norm(x_ref, r, nw_ref, next_w_ref, o_ref, hn_ref):
    ms = jnp.mean(r * r, axis=-1, keepdims=True)
    o = x_ref[...] + r * lax.rsqrt(ms + EPS) * nw_ref[...]
    o_ref[...] = o
    ms_o = jnp.mean(o * o, axis=-1, keepdims=True)
    hn_ref[...] = (o * lax.rsqrt(ms_o + EPS) * next_w_ref[...]).astype(hn_ref.dtype)


def _merge_kernel(ha_ref, hb_ref, hc_ref, gl_ref, x_ref, wa_ref, wb_ref, wc_ref, wo_ref, nw_ref, next_w_ref,
                  o_ref, hn_ref, mix):
    d = x_ref.shape[1]
    branches = ((ha_ref, wa_ref), (hb_ref, wb_ref), (hc_ref, wc_ref))
    for n, (h_ref, w_ref) in enumerate(branches):
        y = _sigmoid(gl_ref[:, n * d:(n + 1) * d]) * jnp.dot(h_ref[...], w_ref[...], preferred_element_type=F32)
        if n == 0:
            mix[...] = y
        else:
            mix[...] += y
    r = jnp.dot(mix[...].astype(MXU_DTYPE), wo_ref[...], preferred_element_type=F32)
    _residual_and_next_norm(x_ref, r, nw_ref, next_w_ref, o_ref, hn_ref)


def _merge(ha, hb, hc, merge_logits, x2d, w_a, w_b, w_c, w_o, norm_w, next_norm_w):
    m, d = x2d.shape
    c = ha.shape[1]
    tm = 256
    row = lambda i: (i, 0)
    resident = lambda shape: pl.BlockSpec(shape, lambda i: (0, 0), pipeline_mode=pl.Buffered(1))
    wc = lambda w: w.astype(MXU_DTYPE)
    return pl.pallas_call(
        _merge_kernel,
        grid=(m // tm,),
        in_specs=[pl.BlockSpec((tm, c), row), pl.BlockSpec((tm, c), row), pl.BlockSpec((tm, c), row),
                  pl.BlockSpec((tm, 3 * d), row), pl.BlockSpec((tm, d), row),
                  resident((c, d)), resident((c, d)), resident((c, d)), resident((d, d)),
                  pl.BlockSpec((1, d), lambda i: (0, 0)), pl.BlockSpec((1, d), lambda i: (0, 0))],
        out_specs=[pl.BlockSpec((tm, d), row), pl.BlockSpec((tm, d), row)],
        out_shape=[jax.ShapeDtypeStruct((m, d), F32), jax.ShapeDtypeStruct((m, d), MXU_DTYPE)],
        scratch_shapes=[pltpu.VMEM((tm, d), F32)],
        compiler_params=_cparams(("parallel",)),
        name="merge_out_proj",
    )(ha, hb, hc, merge_logits, x2d, wc(w_a), wc(w_b), wc(w_c), wc(w_o), norm_w.reshape(1, d),
      next_norm_w.reshape(1, d))


def _ffn_up_kernel(h_ref, halo_ref, wg_ref, wv_ref, cwg_ref, cwv_ref, cbg_ref, cbv_ref, o_ref,
                   hbuf, ugbuf, uvbuf, *, tm, seq):
    i = pl.program_id(0)
    j = pl.program_id(1)

    @pl.when(j == 0)
    def _():
        hbuf[pl.ds(FFN_HALO, tm), :] = h_ref[...]
        first = (i * tm) % seq == 0
        hbuf[pl.ds(0, FFN_HALO), :] = jnp.where(first, jnp.zeros_like(halo_ref[...]), halo_ref[...])

    lhs = hbuf[...]
    ugbuf[...] = jnp.dot(lhs, wg_ref[...], preferred_element_type=F32)
    uvbuf[...] = jnp.dot(lhs, wv_ref[...], preferred_element_type=F32)

    def conv(buf, cw_ref, cb_ref):
        return (cb_ref[...] + cw_ref[pl.ds(2, 1), :] * buf[pl.ds(FFN_HALO, tm), :]
                + cw_ref[pl.ds(1, 1), :] * buf[pl.ds(FFN_HALO - 1, tm), :]
                + cw_ref[pl.ds(0, 1), :] * buf[pl.ds(FFN_HALO - 2, tm), :])

    cg = conv(ugbuf, cwg_ref, cbg_ref)
    cv = conv(uvbuf, cwv_ref, cbv_ref)
    o_ref[...] = (cg * _sigmoid(cg) * cv).astype(o_ref.dtype)


def _ffn_up(h, w_up, conv_w, conv_b, seq):
    m, d = h.shape
    dff = w_up.shape[1] // 2
    tm = min(1024, seq)
    tn = _pick(dff, (512, 256, 128))
    nj = dff // tn
    per = tm // FFN_HALO
    kern = functools.partial(_ffn_up_kernel, tm=tm, seq=seq)
    cb = conv_b.reshape(1, 2 * dff)
    return pl.pallas_call(
        kern,
        grid=(m // tm, nj),
        in_specs=[pl.BlockSpec((tm, d), lambda i, j: (i, 0)),
                  pl.BlockSpec((FFN_HALO, d), lambda i, j: (jnp.maximum(i * per - 1, 0), 0)),
                  pl.BlockSpec((d, tn), lambda i, j: (0, j)),
                  pl.BlockSpec((d, tn), lambda i, j: (0, j + nj)),
                  pl.BlockSpec((3, tn), lambda i, j: (0, j)),
                  pl.BlockSpec((3, tn), lambda i, j: (0, j + nj)),
                  pl.BlockSpec((1, tn), lambda i, j: (0, j)),
                  pl.BlockSpec((1, tn), lambda i, j: (0, j + nj))],
        out_specs=pl.BlockSpec((tm, tn), lambda i, j: (i, j)),
        out_shape=jax.ShapeDtypeStruct((m, dff), MXU_DTYPE),
        scratch_shapes=[pltpu.VMEM((FFN_HALO + tm, d), MXU_DTYPE),
                        pltpu.VMEM((FFN_HALO + tm, tn), F32),
                        pltpu.VMEM((FFN_HALO + tm, tn), F32)],
        compiler_params=_cparams(("parallel", "arbitrary")),
        name="ffn_up_conv_gate",
    )(h, h, w_up, w_up, conv_w, conv_w, cb, cb)


def _ffn_down_kernel(a_ref, w_ref, x_ref, nw_ref, o_ref):
    r = jnp.dot(a_ref[...], w_ref[...], preferred_element_type=F32)
    ms = jnp.mean(r * r, axis=-1, keepdims=True)
    o_ref[...] = x_ref[...] + r * lax.rsqrt(ms + EPS) * nw_ref[...]


def _ffn_down_next_kernel(a_ref, w_ref, x_ref, nw_ref, next_w_ref, o_ref, hn_ref):
    r = jnp.dot(a_ref[...], w_ref[...], preferred_element_type=F32)
    _residual_and_next_norm(x_ref, r, nw_ref, next_w_ref, o_ref, hn_ref)


def _ffn_down(act, w_down, x2d, norm_w, next_norm_w=None):
    m, dff = act.shape
    d = w_down.shape[1]
    tm = 256
    row = pl.BlockSpec((tm, d), lambda i: (i, 0))
    vec = pl.BlockSpec((1, d), lambda i: (0, 0))
    in_specs = [pl.BlockSpec((tm, dff), lambda i: (i, 0)),
                pl.BlockSpec((dff, d), lambda i: (0, 0), pipeline_mode=pl.Buffered(1)),
                row, vec]
    args = [act, w_down, x2d, norm_w.reshape(1, d)]
    x_shape = jax.ShapeDtypeStruct((m, d), F32)
    if next_norm_w is None:
        kern, out_specs, out_shape = _ffn_down_kernel, row, x_shape
    else:
        kern, out_specs = _ffn_down_next_kernel, [row, row]
        out_shape = [x_shape, jax.ShapeDtypeStruct((m, d), MXU_DTYPE)]
        in_specs.append(vec)
        args.append(next_norm_w.reshape(1, d))
    return pl.pallas_call(
        kern,
        grid=(m // tm,),
        in_specs=in_specs,
        out_specs=out_specs,
        out_shape=out_shape,
        compiler_params=_cparams(("parallel",)),
        name="ffn_down_norm_res",
    )(*args)


def _gate_weight(w_gate):
    d = w_gate.shape[0]
    per = HEADS_PER_GROUP * 3
    w = w_gate.reshape(d, N_KV_GROUPS, per)
    w = jnp.pad(w, ((0, 0), (0, 0), (0, GATE_PAD - per)))
    return w.reshape(d, N_KV_GROUPS * GATE_PAD)


def _mixer_layer(x, h, p, l):
    bsz, seq, d = x.shape
    m = bsz * seq
    x2d = x.reshape(m, d)
    g, hpg, dh = N_KV_GROUPS, HEADS_PER_GROUP, HEAD_DIM
    a_in_w = 2 * CONV_WIDTH
    b_in_w = 2 * SG_WIDTH
    q_w = N_HEADS * dh
    kv_w = 6 * g * dh
    gate_w = 3 * N_HEADS
    o0 = 0
    o1 = o0 + a_in_w
    o2 = o1 + b_in_w
    o3 = o2 + q_w
    o4 = o3 + kv_w
    o5 = o4 + gate_w
    w_in = p["w_in"][l]

    wc = lambda w: w.astype(MXU_DTYPE)
    ab_in = _matmul(h, wc(w_in[:, o0:o2]), name="proj_ab")
    h3 = h.reshape(bsz, seq, d)
    aug = 2 * dh
    w_q = w_in[:, o2:o3].reshape(d, N_HEADS, dh)
    w_kv = w_in[:, o3:o4].reshape(d, 6, g, dh)
    pad_lanes = lambda w: jnp.pad(w, ((0, 0), (0, 0), (0, aug - dh))).reshape(d, -1)
    w_rows = wc(jnp.concatenate([pad_lanes(w_q), pad_lanes(w_kv[:, 2]), pad_lanes(w_kv[:, 4])], axis=1))
    onehot = np.zeros((seq, aug), np.float32)
    onehot[np.arange(seq), dh + np.arange(seq) // SLC_BLOCK] = 1.0
    slc_slabs = (N_HEADS * aug // PROJ_SLAB, (N_HEADS + g) * aug // PROJ_SLAB)
    rows_op = _proj_planes(h3, w_rows, jnp.asarray(onehot), MXU_DTYPE, aug, slc_slabs, "proj_attn_rows")
    w_cmp = wc(jnp.concatenate([w_kv[:, 0].reshape(d, g * dh), w_kv[:, 1].reshape(d, g * dh)], axis=1))
    cmp_op = _proj_planes(h3, w_cmp, jnp.zeros((seq, dh), F32), F32, dh, (0, 0), "proj_attn_cmp")
    pad_rows = lambda w: jnp.pad(w.transpose(1, 2, 0), ((0, 0), (0, aug - dh), (0, 0))).reshape(-1, d)
    w_cols = wc(jnp.concatenate([pad_rows(w_kv[:, 3]), pad_rows(w_kv[:, 5])], axis=0))
    ones_rows = np.zeros((2 * g, aug, min(512, seq)), np.float32)
    ones_rows[:, dh] = 1.0
    cols_op = _proj_cols(h3, w_cols, jnp.asarray(ones_rows.reshape(2 * g * aug, -1)), 2 * g)
    gate_logits = _matmul(h, wc(_gate_weight(w_in[:, o4:o5])), name="proj_gate")
    merge_logits = _matmul(h, wc(w_in[:, o5:]), name="proj_merge")

    h_a = _conformer(ab_in.reshape(bsz, seq, a_in_w + b_in_w), p["conv_a_w"][l], p["conv_a_b"][l],
                     p["ln_a_g"][l], p["ln_a_b"][l]).reshape(m, CONV_WIDTH)
    h_b = _gmlp(ab_in, p["ln_b_g"][l], p["ln_b_b"][l], p["sg_w"][l], p["sg_b"][l], col_block=a_in_w // b_in_w)

    n_piece = seq // CMP_STRIDE
    kc = _compress(cmp_op[:, :g].reshape(bsz, g, n_piece, CMP_STRIDE * dh), p["cmp_pe_k"][l], p["cmp_w1_k"][l],
                   p["cmp_w2_k"][l])
    vc = _compress(cmp_op[:, g:].reshape(bsz, g, n_piece, CMP_STRIDE * dh), p["cmp_pe_v"][l], p["cmp_w1_v"][l],
                   p["cmp_w2_v"][l])
    o = _nsa_attention(rows_op, cols_op, kc, vc, gate_logits.reshape(bsz, seq, g * GATE_PAD))
    h_c = o.reshape(m, N_HEADS * dh)

    x_new, h_ffn = _merge(h_a, h_b, h_c, merge_logits, x2d, p["w_a_out"][l], p["w_b_out"][l], p["w_c_out"][l],
                          p["w_o"][l], p["norm_mix_post"][l], p["norm_ffn_pre"][l])
    return x_new.reshape(bsz, seq, d), h_ffn


def _ffn_layer(x, h, p, l, next_norm_w):
    bsz, seq, d = x.shape
    m = bsz * seq
    act = _ffn_up(h, p["w_up"][l].astype(MXU_DTYPE), p["ffn_conv_w"][l], p["ffn_conv_b"][l], seq)
    out = _ffn_down(act, p["w_down"][l].astype(MXU_DTYPE), x.reshape(m, d), p["norm_ffn_post"][l], next_norm_w)
    if next_norm_w is None:
        return out.reshape(bsz, seq, d), None
    return out[0].reshape(bsz, seq, d), out[1]


def kernel(x, norm_mix_pre, norm_mix_post, norm_ffn_pre, norm_ffn_post, w_in, conv_a_w, conv_a_b, ln_a_g, ln_a_b,
           w_a_out, ln_b_g, ln_b_b, sg_w, sg_b, w_b_out, cmp_pe_k, cmp_w1_k, cmp_w2_k, cmp_pe_v, cmp_w1_v,
           cmp_w2_v, w_c_out, w_o, w_up, ffn_conv_w, ffn_conv_b, w_down):
    p = dict(norm_mix_pre=norm_mix_pre, norm_mix_post=norm_mix_post, norm_ffn_pre=norm_ffn_pre,
             norm_ffn_post=norm_ffn_post, w_in=w_in, conv_a_w=conv_a_w, conv_a_b=conv_a_b, ln_a_g=ln_a_g,
             ln_a_b=ln_a_b, w_a_out=w_a_out, ln_b_g=ln_b_g, ln_b_b=ln_b_b, sg_w=sg_w, sg_b=sg_b, w_b_out=w_b_out,
             cmp_pe_k=cmp_pe_k, cmp_w1_k=cmp_w1_k, cmp_w2_k=cmp_w2_k, cmp_pe_v=cmp_pe_v, cmp_w1_v=cmp_w1_v,
             cmp_w2_v=cmp_w2_v, w_c_out=w_c_out, w_o=w_o, w_up=w_up, ffn_conv_w=ffn_conv_w,
             ffn_conv_b=ffn_conv_b, w_down=w_down)
    depth = w_in.shape[0]
    h = _rmsnorm_cast(x.reshape(-1, x.shape[-1]), norm_mix_pre[0])
    for l in range(depth):
        x, h = _mixer_layer(x, h, p, l)
        x, h = _ffn_layer(x, h, p, l, norm_mix_pre[l + 1] if l + 1 < depth else None)
    return x
```
